```python
import math
import jax
import jax.numpy as jnp
from jax import lax
import numpy as np

D_MODEL = 2048
BATCH = 4
SEQ = 2048
DEPTH = 2

GRID_W = 64
CTX_LEN = 256
EPS = 1e-6
N_MIXERS = 4
MIX_WIDTH = D_MODEL
GROUP_W = MIX_WIDTH // N_MIXERS
DN_HEADS = 4
DN_HEAD_DIM = GROUP_W // DN_HEADS
DN_CHUNK = 64
SHORT_CONV = 5
SG_CHUNK = 128
SG_HEADS = 4
SG_HEAD_DIM = GROUP_W // SG_HEADS
POOL_WINDOWS = (2, 4, 8, 16)
POOL_GROUP_DIM = GROUP_W // len(POOL_WINDOWS)
CONF_CONV = 31
QKV_COLS = 3 * GROUP_W
AB_COLS = 4 * DN_HEADS
DN_STATE_COLS = QKV_COLS + AB_COLS
Z_COLS = GROUP_W
SG_COLS = 2 * GROUP_W
POOL_COLS = GROUP_W
CV_COLS = 2 * GROUP_W
IN_WIDTH = DN_STATE_COLS + Z_COLS + SG_COLS + POOL_COLS + CV_COLS
D_FF = 256 * ((8 * D_MODEL // 3 + 255) // 256)
N_EXPERTS = 8
TOP_K = 2
D_FF_EXPERT = 7 * D_MODEL // 2
N_DENSE = (DEPTH + 1) // 2
N_MOE = DEPTH // 2

kernel_name = 'hybrid_headgroup_diffusion_trunk'

F32 = jnp.float32


def rmsnorm(x, g):
    xf = x.astype(F32)
    y = xf * lax.rsqrt(jnp.mean(xf * xf, axis=-1, keepdims=True) + EPS)
    return (y * g.astype(F32)).astype(x.dtype)


def layernorm(x, g, b):
    xf = x.astype(F32)
    mu = jnp.mean(xf, axis=-1, keepdims=True)
    xc = xf - mu
    var = jnp.mean(xc * xc, axis=-1, keepdims=True)
    return (xc * lax.rsqrt(var + EPS) * g.astype(F32) + b.astype(F32)).astype(x.dtype)


def l2norm(x):
    return x * lax.rsqrt(jnp.sum(x * x, axis=-1, keepdims=True) + EPS)


def modulate(h, shift, scale):
    return h * (1 + scale) + shift


def depthwise_conv(x, w):
    pad = w.shape[0] // 2
    return lax.conv_general_dilated(
        x, w.astype(x.dtype)[:, None, :], window_strides=(1,), padding=[(pad, pad)],
        dimension_numbers=('NWC', 'WIO', 'NWC'), feature_group_count=x.shape[-1])


def grid_pos_embed(rows, dim):
    r = jnp.repeat(jnp.arange(rows, dtype=F32), GRID_W)
    col = jnp.tile(jnp.arange(GRID_W, dtype=F32), rows)
    quarter = dim // 4
    freq = jnp.exp(-math.log(10000.0) * jnp.arange(quarter, dtype=F32) / quarter)
    ar = r[:, None] * freq
    ac = col[:, None] * freq
    return jnp.concatenate([jnp.sin(ar), jnp.cos(ar), jnp.sin(ac), jnp.cos(ac)], axis=-1)


def delta_chunked(q, k, v, log_a, beta, s0):
    B_, T, H, _ = q.shape
    C = DN_CHUNK
    N = T // C

    def chunk(t):
        t = t.reshape((B_, N, C, H) + t.shape[3:])
        return jnp.moveaxis(t, (1, 3), (0, 2))

    qc, kc, vc = chunk(q), chunk(k), chunk(v)
    g = jnp.cumsum(chunk(log_a), axis=-1)
    bc = chunk(beta)[..., None]
    kb = kc * bc
    vb = vc * bc
    idx = jnp.arange(C)
    lower = idx[:, None] >= idx[None, :]
    strict = idx[:, None] > idx[None, :]
    gdiff = g[..., :, None] - g[..., None, :]
    decay = jnp.where(lower, jnp.exp(jnp.where(lower, gdiff, 0.0)), 0.0)
    lmat = jnp.where(strict, jnp.einsum('nbhid,nbhjd->nbhij', kb, kc) * decay, 0.0)
    amat = jnp.eye(C, dtype=F32) + lmat
    rhs = jnp.concatenate([vb, kb * jnp.exp(g)[..., None]], axis=-1)
    sol = lax.linalg.triangular_solve(amat, rhs, left_side=True, lower=True)
    dv = v.shape[-1]
    u = sol[..., :dv]
    w = sol[..., dv:]
    qk = jnp.where(lower, jnp.einsum('nbhid,nbhjd->nbhij', qc, kc) * decay, 0.0)
    g_last = g[..., -1]

    def step(S, inp):
        qi, ki, ui, wi, gi, gl, qki = inp
        v_new = ui - jnp.einsum('bhcd,bhde->bhce', wi, S)
        o = (jnp.einsum('bhcd,bhde->bhce', qi * jnp.exp(gi)[..., None], S)
             + jnp.einsum('bhij,bhje->bhie', qki, v_new))
        S = (S * jnp.exp(gl)[..., None, None]
             + jnp.einsum('bhcd,bhce->bhde', ki * jnp.exp(gl[..., None] - gi)[..., None], v_new))
        return S, o

    S, o = lax.scan(step, s0, (qc, kc, u, w, g, g_last, qk))
    o = jnp.moveaxis(o, (0, 2), (1, 3)).reshape(B_, T, H, dv)
    return o, S


def dn_prepare(p, conv_w, a_log, dt_bias):
    B_, T, _ = p.shape
    qkv = jax.nn.silu(depthwise_conv(p[..., :QKV_COLS], conv_w)).astype(F32)
    qkv = qkv.reshape(B_, T, 3, DN_HEADS, DN_HEAD_DIM)
    q = l2norm(qkv[:, :, 0]) * (DN_HEAD_DIM ** -0.5)
    k = l2norm(qkv[:, :, 1])
    v = qkv[:, :, 2]
    ab = p[..., QKV_COLS:DN_STATE_COLS].astype(F32).reshape(B_, T, 4, DN_HEADS)
    log_a = -jnp.exp(a_log.astype(F32)) * jax.nn.softplus(ab[:, :, 0:2] + dt_bias.astype(F32))
    beta = jax.nn.sigmoid(ab[:, :, 2:4])
    return q, k, v, log_a, beta


def dn_bidir_scan(q, k, v, log_a, beta, s0_f, s0_b):
    o_f, s_f = delta_chunked(q, k, v, log_a[:, :, 0], beta[:, :, 0], s0_f)
    fl = lambda t: jnp.flip(t, axis=1)
    o_b, s_b = delta_chunked(fl(q), fl(k), fl(v), fl(log_a[:, :, 1]), fl(beta[:, :, 1]), s0_b)
    return o_f + fl(o_b), s_f, s_b


def gated_head_norm(o, z, g):
    B_, T = z.shape[:2]
    zf = z.astype(F32).reshape(B_, T, DN_HEADS, DN_HEAD_DIM)
    y = o * lax.rsqrt(jnp.mean(o * o, axis=-1, keepdims=True) + EPS) * g.astype(F32) * jax.nn.silu(zf)
    return y.reshape(B_, T, GROUP_W).astype(z.dtype)


def spatial_gating(p, ln_g, ln_b, w_s, b_s):
    B_, T, _ = p.shape
    n = T // SG_CHUNK
    u, v = jnp.split(jax.nn.gelu(p), 2, axis=-1)
    v = layernorm(v, ln_g, ln_b).reshape(B_, n, SG_CHUNK, SG_HEADS, SG_HEAD_DIM)
    s = jnp.einsum('hij,bnjhd->bnihd', w_s, v) + b_s.T[None, None, :, :, None]
    return u * s.reshape(B_, T, GROUP_W)


def multiscale_pool(p, pool_w, pool_scale):
    B_, T, _ = p.shape
    xg = p.reshape(B_, T, len(POOL_WINDOWS), POOL_GROUP_DIM)
    cs = jnp.pad(jnp.cumsum(xg.astype(F32), axis=1), ((0, 0), (1, 0), (0, 0), (0, 0)))
    t = jnp.arange(T)
    means = []
    for gi, w in enumerate(POOL_WINDOWS):
        lo = jnp.clip(t - w // 2, 0, T)
        hi = jnp.clip(t + w // 2, 0, T)
        csg = cs[:, :, gi]
        means.append((csg[:, hi] - csg[:, lo]) / (hi - lo).astype(F32)[None, :, None])
    y = jnp.stack(means, axis=2).astype(p.dtype) - xg
    y = jnp.einsum('btgd,gde->btge', y, pool_w).reshape(B_, T, GROUP_W)
    return y * pool_scale


def conformer_conv(p, cv_w, cv_b, ln_g, ln_b):
    a, g = jnp.split(p, 2, axis=-1)
    h = depthwise_conv(a * jax.nn.sigmoid(g), cv_w) + cv_b
    return jax.nn.silu(layernorm(h, ln_g, ln_b))


def mixers_out(p, o_dn, dn_norm_g, sg_ln_g, sg_ln_b, sg_w, sg_b, pool_w, pool_scale,
               cv_w, cv_b, cv_ln_g, cv_ln_b, w_out):
    off = DN_STATE_COLS
    z = p[..., off:off + Z_COLS]
    off += Z_COLS
    p_sg = p[..., off:off + SG_COLS]
    off += SG_COLS
    p_pool = p[..., off:off + POOL_COLS]
    off += POOL_COLS
    p_cv = p[..., off:off + CV_COLS]
    y = jnp.concatenate([
        gated_head_norm(o_dn, z, dn_norm_g),
        spatial_gating(p_sg, sg_ln_g, sg_ln_b, sg_w, sg_b),
        multiscale_pool(p_pool, pool_w, pool_scale),
        conformer_conv(p_cv, cv_w, cv_b, cv_ln_g, cv_ln_b),
    ], axis=-1)
    return y @ w_out


def swiglu(h, w1, w3, w2):
    return (jax.nn.silu(h @ w1) * (h @ w3)) @ w2


def moe_swiglu(h, router_w, router_b, w1, w3, w2):
    logits = (h @ router_w).astype(F32) + router_b.astype(F32)
    top_val, top_idx = lax.top_k(logits, TOP_K)
    gates = jax.nn.softmax(top_val, axis=-1)
    comb = jnp.sum(jax.nn.one_hot(top_idx, N_EXPERTS, dtype=F32) * gates[..., None], axis=-2)
    comb = comb.astype(h.dtype)
    out = jnp.zeros_like(h)
    for e in range(N_EXPERTS):
        out = out + comb[..., e:e + 1] * swiglu(h, w1[e], w3[e], w2[e])
    return out


def setup_inputs(seed: int = 0) -> dict:
    key = jax.random.key(seed)
    ks = iter(jax.random.split(key, 40))
    nrm = lambda shape, s: jax.random.normal(next(ks), shape, F32) * s
    gain = lambda shape: 1.0 + nrm(shape, 0.02)
    D, L = D_MODEL, DEPTH
    x = nrm((BATCH, SEQ, D), 1.0)
    c = nrm((BATCH, D), 1.0)
    ctx = nrm((BATCH, CTX_LEN, D), 1.0)
    c_ctx = nrm((D,), 1.0)
    ada_w = nrm((L, D, 6 * D), 0.5 * D ** -0.5)
    ada_b = nrm((L, 6 * D), 0.01)
    norm_mix_g = gain((L, D))
    w_in = nrm((L, D, IN_WIDTH), D ** -0.5)
    dn_conv_w = nrm((L, SHORT_CONV, QKV_COLS), SHORT_CONV ** -0.5)
    dn_a_log = jnp.log(jax.random.uniform(next(ks), (L, 2, DN_HEADS), F32, 1.0, 16.0))
    dt = jnp.exp(jax.random.uniform(next(ks), (L, 2, DN_HEADS), F32, math.log(1e-3), math.log(1e-1)))
    dn_dt_bias = dt + jnp.log(-jnp.expm1(-dt))
    dn_norm_g = gain((L, DN_HEAD_DIM))
    sg_ln_g = gain((L, GROUP_W))
    sg_ln_b = nrm((L, GROUP_W), 0.01)
    sg_w = nrm((L, SG_HEADS, SG_CHUNK, SG_CHUNK), SG_CHUNK ** -0.5)
    sg_b = 1.0 + nrm((L, SG_HEADS, SG_CHUNK), 0.02)
    pool_w = nrm((L, len(POOL_WINDOWS), POOL_GROUP_DIM, POOL_GROUP_DIM), POOL_GROUP_DIM ** -0.5)
    pool_scale = gain((L, GROUP_W))
    cv_w = nrm((L, CONF_CONV, GROUP_W), CONF_CONV ** -0.5)
    cv_b = nrm((L, GROUP_W), 0.01)
    cv_ln_g = gain((L, GROUP_W))
    cv_ln_b = nrm((L, GROUP_W), 0.01)
    w_out = nrm((L, MIX_WIDTH, D), MIX_WIDTH ** -0.5)
    norm_ffn_g = gain((L, D))
    ffn_w1 = nrm((N_DENSE, D, D_FF), D ** -0.5)
    ffn_w3 = nrm((N_DENSE, D, D_FF), D ** -0.5)
    ffn_w2 = nrm((N_DENSE, D_FF, D), D_FF ** -0.5)
    router_w = nrm((N_MOE, D, N_EXPERTS), D ** -0.5)
    router_b = nrm((N_MOE, N_EXPERTS), 0.01)
    moe_w1 = nrm((N_MOE, N_EXPERTS, D, D_FF_EXPERT), D ** -0.5)
    moe_w3 = nrm((N_MOE, N_EXPERTS, D, D_FF_EXPERT), D ** -0.5)
    moe_w2 = nrm((N_MOE, N_EXPERTS, D_FF_EXPERT, D), D_FF_EXPERT ** -0.5)
    final_norm_g = gain((D,))
    return {'x': x, 'c': c, 'ctx': ctx, 'c_ctx': c_ctx, 'ada_w': ada_w, 'ada_b': ada_b,
            'norm_mix_g': norm_mix_g, 'w_in': w_in, 'dn_conv_w': dn_conv_w, 'dn_a_log': dn_a_log,
            'dn_dt_bias': dn_dt_bias, 'dn_norm_g': dn_norm_g, 'sg_ln_g': sg_ln_g, 'sg_ln_b': sg_ln_b,
            'sg_w': sg_w, 'sg_b': sg_b, 'pool_w': pool_w, 'pool_scale': pool_scale, 'cv_w': cv_w,
            'cv_b': cv_b, 'cv_ln_g': cv_ln_g, 'cv_ln_b': cv_ln_b, 'w_out': w_out,
            'norm_ffn_g': norm_ffn_g, 'ffn_w1': ffn_w1, 'ffn_w3': ffn_w3, 'ffn_w2': ffn_w2,
            'router_w': router_w, 'router_b': router_b, 'moe_w1': moe_w1, 'moe_w3': moe_w3,
            'moe_w2': moe_w2, 'final_norm_g': final_norm_g}


def reference(x, c, ctx, c_ctx, ada_w, ada_b, norm_mix_g, w_in, dn_conv_w, dn_a_log, dn_dt_bias,
              dn_norm_g, sg_ln_g, sg_ln_b, sg_w, sg_b, pool_w, pool_scale, cv_w, cv_b, cv_ln_g,
              cv_ln_b, w_out, norm_ffn_g, ffn_w1, ffn_w3, ffn_w2, router_w, router_b, moe_w1,
              moe_w3, moe_w2, final_norm_g):
    B_, T, D = x.shape
    ROWS = T // GRID_W
    x = x + grid_pos_embed(ROWS, D).astype(x.dtype)[None]
    h_ctx = ctx
    s_c = jax.nn.silu(c)
    s_cc = jax.nn.silu(c_ctx)
    zero_state = jnp.zeros((B_, DN_HEADS, DN_HEAD_DIM, DN_HEAD_DIM), F32)
    for l in range(DEPTH):
        last = l == DEPTH - 1
        mod_x = (s_c @ ada_w[l] + ada_b[l])[:, None, :]
        mod_c = (s_cc @ ada_w[l] + ada_b[l])[None, None, :]
        shm, scm, gm, shf, scf, gf = jnp.split(mod_x, 6, axis=-1)
        cshm, cscm, cgm, cshf, cscf, cgf = jnp.split(mod_c, 6, axis=-1)
        dn_args = (dn_conv_w[l], dn_a_log[l], dn_dt_bias[l])
        bcd_args = (dn_norm_g[l], sg_ln_g[l], sg_ln_b[l], sg_w[l], sg_b[l], pool_w[l], pool_scale[l],
                    cv_w[l], cv_b[l], cv_ln_g[l], cv_ln_b[l], w_out[l])

        nc = modulate(rmsnorm(h_ctx, norm_mix_g[l]), cshm, cscm)
        pc = nc @ (w_in[l][:, :DN_STATE_COLS] if last else w_in[l])
        o_c, s_f, s_b = dn_bidir_scan(*dn_prepare(pc, *dn_args), zero_state, zero_state)
        nx = modulate(rmsnorm(x, norm_mix_g[l]), shm, scm)
        px = nx @ w_in[l]
        o_x, _, _ = dn_bidir_scan(*dn_prepare(px, *dn_args), s_f, s_b)
        x = x + gm * mixers_out(px, o_x, *bcd_args)
        if not last:
            h_ctx = h_ctx + cgm * mixers_out(pc, o_c, *bcd_args)

        if l % 2 == 0:
            i = l // 2
            ffn = lambda h: swiglu(h, ffn_w1[i], ffn_w3[i], ffn_w2[i])
        else:
            i = l // 2
            ffn = lambda h: moe_swiglu(h, router_w[i], router_b[i], moe_w1[i], moe_w3[i], moe_w2[i])
        x = x + gf * ffn(modulate(rmsnorm(x, norm_ffn_g[l]), shf, scf))
        if not last:
            h_ctx = h_ctx + cgf * ffn(modulate(rmsnorm(h_ctx, norm_ffn_g[l]), cshf, cscf))
    return rmsnorm(x, final_norm_g)
```

```python
import functools
import math

import jax
import jax.numpy as jnp
from jax import lax
from jax.experimental import pallas as pl
from jax.experimental.pallas import tpu as pltpu

F32 = jnp.float32
BF16 = jnp.bfloat16
EPS = 1e-6

GRID_W = 64
N_HEADS = 4
HEAD_DIM = 128
GROUP_W = N_HEADS * HEAD_DIM
DN_CHUNK = 64
SEQ_TILE = 256
CHUNKS_PER_TILE = SEQ_TILE // DN_CHUNK
SG_CHUNK = 128
POOL_WINDOWS = (2, 4, 8, 16)
SHORT_CONV = 5
CONF_CONV = 31
N_EXPERTS = 8
LANES = 128
SUBLANES = 8
VMEM_LIMIT = 56 * 1024 * 1024

QKV_OFF, QKV_W = 0, 3 * GROUP_W
Z_OFF = QKV_OFF + QKV_W
SG_OFF = Z_OFF + GROUP_W
CV_OFF = SG_OFF + 2 * GROUP_W
POOL_OFF = CV_OFF + 2 * GROUP_W
AB_OFF = POOL_OFF + GROUP_W
P_WIDTH = AB_OFF + LANES
AB_COLS = 4 * N_HEADS

SHM, SCM, GM, SHF, SCF, GF = range(6)


def _dot(a, b):
    return jnp.dot(a, b, preferred_element_type=F32)


def _dot_nt(a, b):
    return lax.dot_general(a, b, (((1,), (1,)), ((), ())), preferred_element_type=F32)


def _dot_tn(a, b):
    return lax.dot_general(a, b, (((0,), (0,)), ((), ())), preferred_element_type=F32)


def _sigmoid(x):
    return 1.0 / (1.0 + jnp.exp(-x))


def _silu(x):
    return x * _sigmoid(x)


def _cparams(*sem):
    return pltpu.CompilerParams(dimension_semantics=sem, vmem_limit_bytes=VMEM_LIMIT)


def _mod_row(i, *, tile, ctx_rows, seq, batch):
    start = i * tile
    return jnp.where(start < ctx_rows, batch, (start - ctx_rows) // seq)


def _mod_spec(d, chunk, row_fn):
    return pl.BlockSpec((None, 1, d), lambda i, *_: (row_fn(i), 0, chunk))


def _norm_modulate(x, g, scale, shift):
    y = x * lax.rsqrt(jnp.mean(x * x, axis=-1, keepdims=True) + EPS) * g
    return y * (1.0 + scale) + shift


def _ada_kernel(c_ref, w_ref, b_ref, o_ref):
    s = _silu(c_ref[...]).astype(BF16)
    o_ref[...] = _dot(s, w_ref[...].astype(BF16)) + b_ref[...]


def _ada(cc, ada_w, ada_b):
    nl, d, n = ada_w.shape
    tn = 1024
    return pl.pallas_call(
        _ada_kernel,
        grid=(nl, n // tn),
        in_specs=[pl.BlockSpec((SUBLANES, d), lambda l, j: (0, 0)),
                  pl.BlockSpec((None, d, tn), lambda l, j: (l, 0, j)),
                  pl.BlockSpec((None, 1, tn), lambda l, j: (l, 0, j))],
        out_specs=pl.BlockSpec((None, SUBLANES, tn), lambda l, j: (l, 0, j)),
        out_shape=jax.ShapeDtypeStruct((nl, SUBLANES, n), F32),
        compiler_params=_cparams("parallel", "parallel"),
        name="ada_modulation",
    )(cc, ada_w, ada_b.reshape(nl, 1, n))


def _proj_kernel(h_ref, g_ref, sh_ref, sc_ref, w_ref, o_ref):
    y = _norm_modulate(h_ref[...], g_ref[...], sc_ref[...], sh_ref[...])
    o_ref[...] = _dot(y.astype(BF16), w_ref[...])


def _proj(h, g, mod, w, row_fn):
    m, d = h.shape
    n = w.shape[1]
    tm = SEQ_TILE
    return pl.pallas_call(
        _proj_kernel,
        grid=(m // tm,),
        in_specs=[pl.BlockSpec((tm, d), lambda i: (i, 0)),
                  pl.BlockSpec((1, d), lambda i: (0, 0)),
                  _mod_spec(d, SHM, row_fn),
                  _mod_spec(d, SCM, row_fn),
                  pl.BlockSpec((d, n), lambda i: (0, 0), pipeline_mode=pl.Buffered(1))],
        out_specs=pl.BlockSpec((tm, n), lambda i: (i, 0)),
        out_shape=jax.ShapeDtypeStruct((m, n), F32),
        compiler_params=_cparams("parallel"),
        name="in_proj",
    )(h, g.reshape(1, d), mod, mod, w)


def _segment_pos(i, n_ctx_tiles, ctx_seg_tiles, x_seg_tiles):
    in_ctx = i < n_ctx_tiles
    tpos = jnp.where(in_ctx, i % ctx_seg_tiles, (i - n_ctx_tiles) % x_seg_tiles)
    tlen = jnp.where(in_ctx, ctx_seg_tiles, x_seg_tiles)
    return tpos, tlen


def _fill_ext(ext_ref, prev, cur, nxt, first, last, halo):
    ext_ref[0:halo, :] = jnp.where(first, 0.0, prev)
    ext_ref[halo:halo + SEQ_TILE, :] = cur
    ext_ref[halo + SEQ_TILE:2 * halo + SEQ_TILE, :] = jnp.where(last, 0.0, nxt)


def _dn_local_kernel(qkv_ref, prev_ref, next_ref, ab_ref, cw_ref, alog_ref, dtb_ref,
                     uf_ref, wf_ref, qgf_ref, kdf_ref, qkf_ref,
                     ub_ref, wb_ref, qgb_ref, kdb_ref, qkb_ref, gam_ref,
                     ext_ref, *, seg):
    tpos, tlen = _segment_pos(pl.program_id(0), *seg)
    halo = SUBLANES
    _fill_ext(ext_ref, prev_ref[...], qkv_ref[...], next_ref[...], tpos == 0, tpos == tlen - 1, halo)

    pad = SHORT_CONV // 2
    acc = cw_ref[0:1, :] * ext_ref[halo - pad:halo - pad + SEQ_TILE, :]
    for j in range(1, SHORT_CONV):
        acc = acc + cw_ref[j:j + 1, :] * ext_ref[halo - pad + j:halo - pad + j + SEQ_TILE, :]
    s = _silu(acc)

    ab = ab_ref[...]
    lane = lax.broadcasted_iota(jnp.int32, (SEQ_TILE, LANES), 1)
    xs = ab + dtb_ref[...]
    softplus = jnp.maximum(xs, 0.0) + jnp.log1p(jnp.exp(-jnp.abs(xs)))
    la = jnp.where(lane < 2 * N_HEADS, -jnp.exp(alog_ref[...]) * softplus, 0.0)
    beta = _sigmoid(ab)

    r = lax.broadcasted_iota(jnp.int32, (SEQ_TILE, SEQ_TILE), 0)
    c = lax.broadcasted_iota(jnp.int32, (SEQ_TILE, SEQ_TILE), 1)
    same = (r // DN_CHUNK) == (c // DN_CHUNK)
    incl = (same & (c <= r), same & (c >= r))
    strict = (same & (c < r), same & (c > r))

    def off_block(s, d):
        hi, lo = (r, c) if d == 0 else (c, r)
        return ((hi // s) % 2 == 1) & ((lo // s) == (hi // s) - 1)

    la_hi = la.astype(BF16)
    rem = la - la_hi.astype(F32)
    la_mid = rem.astype(BF16)
    la_lo = (rem - la_mid.astype(F32)).astype(BF16)

    def chunk_sum(mask):
        m = jnp.where(mask, 1.0, 0.0).astype(BF16)
        return _dot(m, la_hi) + _dot(m, la_mid) + _dot(m, la_lo)

    g_fwd = chunk_sum(incl[0])
    g_bwd = chunk_sum(incl[1])
    g = jnp.where(lane < N_HEADS, g_fwd, g_bwd)
    g_tot = g_fwd + g_bwd - la
    g_t = g.T

    outs = ((uf_ref, wf_ref, qgf_ref, kdf_ref, qkf_ref), (ub_ref, wb_ref, qgb_ref, kdb_ref, qkb_ref))
    for h in range(N_HEADS):
        sl = slice(h * HEAD_DIM, (h + 1) * HEAD_DIM)
        qh = s[:, h * HEAD_DIM:(h + 1) * HEAD_DIM]
        kh = s[:, GROUP_W + h * HEAD_DIM:GROUP_W + (h + 1) * HEAD_DIM]
        vh = s[:, 2 * GROUP_W + h * HEAD_DIM:2 * GROUP_W + (h + 1) * HEAD_DIM]
        qh = qh * lax.rsqrt(jnp.sum(qh * qh, axis=-1, keepdims=True) + EPS) * (HEAD_DIM ** -0.5)
        kh = kh * lax.rsqrt(jnp.sum(kh * kh, axis=-1, keepdims=True) + EPS)
        qh16 = qh.astype(BF16)
        kh16 = kh.astype(BF16)
        qq = _dot_nt(qh16, kh16)
        for d in range(2):
            gi = d * N_HEADS + h
            u_ref, w_ref, qg_ref, kd_ref, qk_ref = outs[d]
            gcol = g[:, gi:gi + 1]
            grow = g_t[gi:gi + 1, :]
            decay = jnp.where(incl[d], jnp.exp(jnp.where(incl[d], gcol - grow, 0.0)), 0.0)
            bcol = beta[:, 2 * N_HEADS + gi:2 * N_HEADS + gi + 1]
            kb = kh * bcol
            vb = vh * bcol
            lmat = jnp.where(strict[d], _dot_nt(kb.astype(BF16), kh16) * decay, 0.0)
            t_inv = jnp.where(r == c, 1.0, 0.0) - jnp.where(off_block(1, d), lmat, 0.0)
            s_blk = 2
            while s_blk < DN_CHUNK:
                t16 = t_inv.astype(BF16)
                l_s = jnp.where(off_block(s_blk, d), lmat, 0.0).astype(BF16)
                t_inv = t_inv - _dot(t16, _dot(l_s, t16).astype(BF16))
                s_blk *= 2
            eg = jnp.exp(gcol)
            rhs = jnp.concatenate([vb, kb * eg], axis=1)
            sol = rhs + _dot(jnp.where(strict[d], t_inv, 0.0).astype(BF16), rhs.astype(BF16))
            u_ref[:, sl] = sol[:, :HEAD_DIM]
            w_ref[:, sl] = sol[:, HEAD_DIM:].astype(BF16)
            qkm = jnp.where(incl[d], qq * decay, 0.0)
            qk = qkm[:, 0:DN_CHUNK]
            for cb in range(1, CHUNKS_PER_TILE):
                qk = qk + qkm[:, cb * DN_CHUNK:(cb + 1) * DN_CHUNK]
            qk_ref[:, sl] = jnp.concatenate([qk, jnp.zeros_like(qk)], axis=1).astype(BF16)
            gl = g_tot[:, gi:gi + 1]
            qg_ref[:, sl] = (qh * eg).astype(BF16)
            kd_ref[:, sl] = (kh * jnp.exp(gl - gcol)).astype(BF16)
            for cb in range(CHUNKS_PER_TILE):
                gam_ref[cb, :, gi * LANES:(gi + 1) * LANES] = jnp.broadcast_to(
                    jnp.exp(gl[cb * DN_CHUNK:cb * DN_CHUNK + 1, :]), (1, LANES))


def _dn_local(p, conv_w, a_log, dt_bias, seg):
    rows = p.shape[0]
    nt = rows // SEQ_TILE
    hb = SEQ_TILE // SUBLANES
    last_hb = rows // SUBLANES - 1
    cw = jnp.zeros((SUBLANES, QKV_W), F32).at[:SHORT_CONV].set(conv_w)
    pad_row = lambda v: jnp.zeros((1, LANES), F32).at[0, :2 * N_HEADS].set(v.reshape(-1))
    f32_out = jax.ShapeDtypeStruct((rows, GROUP_W), F32)
    b16_out = jax.ShapeDtypeStruct((rows, GROUP_W), BF16)
    gam_out = jax.ShapeDtypeStruct((rows // DN_CHUNK, 1, 2 * N_HEADS * LANES), F32)
    row_spec = pl.BlockSpec((SEQ_TILE, GROUP_W), lambda i: (i, 0))
    full = lambda shape: pl.BlockSpec(shape, lambda i: (0,) * len(shape))
    return pl.pallas_call(
        functools.partial(_dn_local_kernel, seg=seg),
        grid=(nt,),
        in_specs=[pl.BlockSpec((SEQ_TILE, QKV_W), lambda i: (i, QKV_OFF // QKV_W)),
                  pl.BlockSpec((SUBLANES, QKV_W), lambda i: (jnp.maximum(i * hb - 1, 0), 0)),
                  pl.BlockSpec((SUBLANES, QKV_W), lambda i: (jnp.minimum((i + 1) * hb, last_hb), 0)),
                  pl.BlockSpec((SEQ_TILE, LANES), lambda i: (i, AB_OFF // LANES)),
                  full((SUBLANES, QKV_W)), full((1, LANES)), full((1, LANES))],
        out_specs=[row_spec] * 10 + [pl.BlockSpec((CHUNKS_PER_TILE, 1, 2 * N_HEADS * LANES), lambda i: (i, 0, 0))],
        out_shape=[f32_out, b16_out, b16_out, b16_out, b16_out] * 2 + [gam_out],
        scratch_shapes=[pltpu.VMEM((SEQ_TILE + 2 * SUBLANES, QKV_W), F32)],
        compiler_params=_cparams("parallel"),
        name="dn_local",
    )(p, p, p, p, cw, pad_row(a_log), pad_row(dt_bias))


def _dn_scan_kernel(uf, wf, qgf, kdf, qkf, gamf, ub, wb, qgb, kdb, qkb, gamb, of_ref, ob_ref, s_ref):
    @pl.when(pl.program_id(1) == 0)
    def _():
        s_ref[...] = jnp.zeros_like(s_ref)

    dirs = ((uf, wf, qgf, kdf, qkf, gamf, of_ref), (ub, wb, qgb, kdb, qkb, gamb, ob_ref))
    for d, (u, w, qg, kd, qk, gam, o_ref) in enumerate(dirs):
        for h in range(N_HEADS):
            gi = d * N_HEADS + h
            sl = slice(h * HEAD_DIM, (h + 1) * HEAD_DIM)
            state = s_ref[gi]
            state16 = state.astype(BF16)
            v_new = u[:, sl] - _dot(w[:, sl], state16)
            v16 = v_new.astype(BF16)
            o_ref[:, sl] = _dot(qg[:, sl], state16) + _dot(qk[:, h * HEAD_DIM:h * HEAD_DIM + DN_CHUNK], v16)
            s_ref[gi] = state * gam[:, gi * LANES:(gi + 1) * LANES] + _dot_tn(kd[:, sl], v16)


def _dn_scan(local, *, batch, ctx_len, seq):
    uf, wf, qgf, kdf, qkf, ub, wb, qgb, kdb, qkb, gam = local
    rows = uf.shape[0]
    cc, xc = ctx_len // DN_CHUNK, seq // DN_CHUNK
    x0 = batch * cc

    def fwd(b, s):
        return jnp.where(s < cc, b * cc + s, x0 + b * xc + (s - cc))

    def bwd(b, s):
        return jnp.where(s < cc, b * cc + (cc - 1 - s), x0 + b * xc + (xc - 1 - (s - cc)))

    def specs(idx):
        row = pl.BlockSpec((DN_CHUNK, GROUP_W), lambda b, s: (idx(b, s), 0))
        return [row] * 5 + [pl.BlockSpec((None, 1, 2 * N_HEADS * LANES), lambda b, s: (idx(b, s), 0, 0))]

    out = jax.ShapeDtypeStruct((rows, GROUP_W), F32)
    return pl.pallas_call(
        _dn_scan_kernel,
        grid=(batch, cc + xc),
        in_specs=specs(fwd) + specs(bwd),
        out_specs=[pl.BlockSpec((DN_CHUNK, GROUP_W), lambda b, s: (fwd(b, s), 0)),
                   pl.BlockSpec((DN_CHUNK, GROUP_W), lambda b, s: (bwd(b, s), 0))],
        out_shape=[out, out],
        scratch_shapes=[pltpu.VMEM((2 * N_HEADS, HEAD_DIM, HEAD_DIM), F32)],
        compiler_params=_cparams("parallel", "arbitrary"),
        name="dn_scan",
    )(uf, wf, qgf, kdf, qkf, gam, ub, wb, qgb, kdb, qkb, gam)


def _layernorm(x, g, b):
    mu = jnp.mean(x, axis=-1, keepdims=True)
    xc = x - mu
    var = jnp.mean(xc * xc, axis=-1, keepdims=True)
    return xc * lax.rsqrt(var + EPS) * g + b


def _mix_kernel(h_ref, of_ref, ob_ref, z_ref, sg_ref, cv_ref, cvp_ref, cvn_ref, pool_ref, poolp_ref, pooln_ref,
                gm_ref, dng_ref, sglg_ref, sglb_ref, sgw_ref, sgbt_ref, pw_ref, ps_ref,
                cvw_ref, cvb_ref, cvlg_ref, cvlb_ref, wout_ref,
                o_ref, y_ref, pext_ref, cext_ref, *, seg, tile0):
    tpos, tlen = _segment_pos(pl.program_id(0) + tile0, *seg)
    first, last = tpos == 0, tpos == tlen - 1

    o = of_ref[...] + ob_ref[...]
    z = z_ref[...]
    for h in range(N_HEADS):
        sl = slice(h * HEAD_DIM, (h + 1) * HEAD_DIM)
        oh = o[:, sl]
        yh = oh * lax.rsqrt(jnp.mean(oh * oh, axis=-1, keepdims=True) + EPS) * dng_ref[...] * _silu(z[:, sl])
        y_ref[:, sl] = yh.astype(BF16)

    psg = sg_ref[...]
    psg = 0.5 * psg * (1.0 + jnp.tanh(math.sqrt(2.0 / math.pi) * (psg + 0.044715 * (psg * psg * psg))))
    u = psg[:, :GROUP_W]
    v = _layernorm(psg[:, GROUP_W:], sglg_ref[...], sglb_ref[...]).astype(BF16)
    for n in range(SEQ_TILE // SG_CHUNK):
        rs = slice(n * SG_CHUNK, (n + 1) * SG_CHUNK)
        for h in range(N_HEADS):
            sl = slice(h * HEAD_DIM, (h + 1) * HEAD_DIM)
            sv = _dot(sgw_ref[h], v[rs, sl]) + sgbt_ref[:, h:h + 1]
            y_ref[rs, GROUP_W + h * HEAD_DIM:GROUP_W + (h + 1) * HEAD_DIM] = (u[rs, sl] * sv).astype(BF16)

    halo = SUBLANES
    _fill_ext(pext_ref, poolp_ref[...], pool_ref[...], pooln_ref[...], first, last, halo)
    t = tpos * SEQ_TILE + lax.broadcasted_iota(jnp.int32, (SEQ_TILE, 1), 0)
    seg_len = tlen * SEQ_TILE
    for gi, win in enumerate(POOL_WINDOWS):
        sl = slice(gi * LANES, (gi + 1) * LANES)
        tot = pext_ref[halo - win // 2:halo - win // 2 + SEQ_TILE, sl]
        for m in range(1 - win // 2, win // 2):
            tot = tot + pext_ref[halo + m:halo + m + SEQ_TILE, sl]
        cnt = jnp.clip(t + win // 2, 0, seg_len) - jnp.clip(t - win // 2, 0, seg_len)
        yg = tot / cnt.astype(F32) - pool_ref[:, sl]
        yg = _dot(yg.astype(BF16), pw_ref[gi]) * ps_ref[:, sl]
        y_ref[:, 2 * GROUP_W + gi * LANES:2 * GROUP_W + (gi + 1) * LANES] = yg.astype(BF16)

    halo = 2 * SUBLANES
    glu = lambda p: p[:, :GROUP_W] * _sigmoid(p[:, GROUP_W:])
    _fill_ext(cext_ref, glu(cvp_ref[...]), glu(cv_ref[...]), glu(cvn_ref[...]), first, last, halo)
    pad = CONF_CONV // 2
    acc = cvw_ref[0:1, :] * cext_ref[halo - pad:halo - pad + SEQ_TILE, :]
    for j in range(1, CONF_CONV):
        acc = acc + cvw_ref[j:j + 1, :] * cext_ref[halo - pad + j:halo - pad + j + SEQ_TILE, :]
    yc = _silu(_layernorm(acc + cvb_ref[...], cvlg_ref[...], cvlb_ref[...]))
    y_ref[:, 3 * GROUP_W:] = yc.astype(BF16)

    o_ref[...] = h_ref[...] + gm_ref[...] * _dot(y_ref[...], wout_ref[...])


def _mix(h, p, o_f, o_b, mod, params, wout, *, seg, row_fn, tile0, n_tiles):
    d = h.shape[1]
    (dng, sglg, sglb, sgw, sgb, pw, ps, cvw, cvb, cvlg, cvlb) = params
    rows = p.shape[0]
    row = lambda v: v.reshape(1, -1)
    cvw_p = jnp.zeros((4 * SUBLANES, GROUP_W), F32).at[:CONF_CONV].set(cvw)
    sgbt = jnp.zeros((SG_CHUNK, LANES), F32).at[:, :N_HEADS].set(sgb.T)

    def halo_specs(width, col, halo):
        hb = SEQ_TILE // halo
        last_hb = rows // halo - 1
        return [pl.BlockSpec((SEQ_TILE, width), lambda i: (i + tile0, col)),
                pl.BlockSpec((halo, width), lambda i: (jnp.maximum((i + tile0) * hb - 1, 0), col)),
                pl.BlockSpec((halo, width), lambda i: (jnp.minimum((i + tile0 + 1) * hb, last_hb), col))]

    full = lambda a: pl.BlockSpec(a.shape, lambda i: (0,) * a.ndim)
    consts = [row(dng), row(sglg), row(sglb), sgw.astype(BF16), sgbt, pw.astype(BF16), row(ps),
              cvw_p, row(cvb), row(cvlg), row(cvlb)]
    return pl.pallas_call(
        functools.partial(_mix_kernel, seg=seg, tile0=tile0),
        grid=(n_tiles,),
        in_specs=[pl.BlockSpec((SEQ_TILE, d), lambda i: (i + tile0, 0)),
                  pl.BlockSpec((SEQ_TILE, GROUP_W), lambda i: (i + tile0, 0)),
                  pl.BlockSpec((SEQ_TILE, GROUP_W), lambda i: (i + tile0, 0)),
                  pl.BlockSpec((SEQ_TILE, GROUP_W), lambda i: (i + tile0, Z_OFF // GROUP_W)),
                  pl.BlockSpec((SEQ_TILE, 2 * GROUP_W), lambda i: (i + tile0, SG_OFF // (2 * GROUP_W)))]
                 + halo_specs(2 * GROUP_W, CV_OFF // (2 * GROUP_W), 2 * SUBLANES)
                 + halo_specs(GROUP_W, POOL_OFF // GROUP_W, SUBLANES)
                 + [_mod_spec(d, GM, lambda i: row_fn(i + tile0))]
                 + [full(a) for a in consts]
                 + [pl.BlockSpec(wout.shape, lambda i: (0, 0), pipeline_mode=pl.Buffered(1))],
        out_specs=pl.BlockSpec((SEQ_TILE, d), lambda i: (i, 0)),
        out_shape=jax.ShapeDtypeStruct((n_tiles * SEQ_TILE, d), F32),
        scratch_shapes=[pltpu.VMEM((SEQ_TILE, d), BF16),
                        pltpu.VMEM((SEQ_TILE + 2 * SUBLANES, GROUP_W), F32),
                        pltpu.VMEM((SEQ_TILE + 4 * SUBLANES, GROUP_W), F32)],
        compiler_params=_cparams("parallel"),
        name="mixers_out_proj",
    )(h, o_f, o_b, p, p, p, p, p, p, p, p, mod, *consts, wout)


def _ffn_kernel(h_ref, g_ref, sh_ref, sc_ref, gate_ref, w1_ref, w3_ref, w2_ref, o_ref, xn_ref, acc_ref):
    j = pl.program_id(1)

    @pl.when(j == 0)
    def _():
        xn_ref[...] = _norm_modulate(h_ref[...], g_ref[...], sc_ref[...], sh_ref[...]).astype(BF16)
        acc_ref[...] = jnp.zeros_like(acc_ref)

    xn = xn_ref[...]
    mid = _silu(_dot(xn, w1_ref[...])) * _dot(xn, w3_ref[...])
    acc_ref[...] += _dot(mid.astype(BF16), w2_ref[...])

    @pl.when(j == pl.num_programs(1) - 1)
    def _():
        o_ref[...] = h_ref[...] + gate_ref[...] * acc_ref[...]


def _ffn(h, g, mod, w1, w3, w2, row_fn_for):
    m, d = h.shape
    f = w1.shape[1]
    tm, tf = 512, 512
    row_fn = row_fn_for(tm)
    return pl.pallas_call(
        _ffn_kernel,
        grid=(m // tm, f // tf),
        in_specs=[pl.BlockSpec((tm, d), lambda i, j: (i, 0)),
                  pl.BlockSpec((1, d), lambda i, j: (0, 0)),
                  _mod_spec(d, SHF, row_fn), _mod_spec(d, SCF, row_fn), _mod_spec(d, GF, row_fn),
                  pl.BlockSpec((d, tf), lambda i, j: (0, j)),
                  pl.BlockSpec((d, tf), lambda i, j: (0, j)),
                  pl.BlockSpec((tf, d), lambda i, j: (j, 0))],
        out_specs=pl.BlockSpec((tm, d), lambda i, j: (i, 0)),
        out_shape=jax.ShapeDtypeStruct((m, d), F32),
        scratch_shapes=[pltpu.VMEM((tm, d), BF16), pltpu.VMEM((tm, d), F32)],
        compiler_params=_cparams("parallel", "arbitrary"),
        name="dense_swiglu",
    )(h, g.reshape(1, d), mod, mod, mod, w1, w3, w2)


ROUTE_TILE = 256
EXPERT_ROWS = 1024


def _router_kernel(h_ref, g_ref, sh_ref, sc_ref, rw_ref, rb_ref, hn_ref, idx_ref, gate_ref):
    hn = _norm_modulate(h_ref[...], g_ref[...], sc_ref[...], sh_ref[...])
    hn_ref[...] = hn
    logits = jnp.dot(hn, rw_ref[...], preferred_element_type=F32, precision=lax.Precision.HIGHEST) + rb_ref[...]
    lane = lax.broadcasted_iota(jnp.int32, logits.shape, 1)
    lane_f = lane.astype(F32)
    neg = -jnp.inf
    lg = jnp.where(lane < N_EXPERTS, logits, neg)
    m1 = jnp.max(lg, axis=-1, keepdims=True)
    i1 = jnp.min(jnp.where(lg == m1, lane_f, float(LANES)), axis=-1, keepdims=True)
    lg2 = jnp.where(lane_f == i1, neg, lg)
    m2 = jnp.max(lg2, axis=-1, keepdims=True)
    i2 = jnp.min(jnp.where(lg2 == m2, lane_f, float(LANES)), axis=-1, keepdims=True)
    e2 = jnp.exp(m2 - m1)
    g1 = 1.0 / (1.0 + e2)
    g2 = e2 / (1.0 + e2)
    idx_ref[...] = jnp.where(lane == 0, i1, jnp.where(lane == 1, i2, 0.0)).astype(jnp.int32)
    gate_ref[...] = jnp.where(lane == 0, g1, jnp.where(lane == 1, g2, 0.0))


def _router(h, g, mod, rw, rb, row_fn, tile0):
    d = h.shape[1]
    tm = ROUTE_TILE
    n_tiles = h.shape[0] // tm - tile0
    m = n_tiles * tm
    rw_p = jnp.zeros((d, LANES), F32).at[:, :N_EXPERTS].set(rw)
    rb_p = jnp.zeros((1, LANES), F32).at[0, :N_EXPERTS].set(rb)
    shifted = lambda i: row_fn(i + tile0)
    return pl.pallas_call(
        _router_kernel,
        grid=(n_tiles,),
        in_specs=[pl.BlockSpec((tm, d), lambda i: (i + tile0, 0)),
                  pl.BlockSpec((1, d), lambda i: (0, 0)),
                  _mod_spec(d, SHF, shifted), _mod_spec(d, SCF, shifted),
                  pl.BlockSpec((d, LANES), lambda i: (0, 0)),
                  pl.BlockSpec((1, LANES), lambda i: (0, 0))],
        out_specs=[pl.BlockSpec((tm, d), lambda i: (i, 0)),
                   pl.BlockSpec((tm, LANES), lambda i: (i, 0)),
                   pl.BlockSpec((tm, LANES), lambda i: (i, 0))],
        out_shape=[jax.ShapeDtypeStruct((m, d), F32),
                   jax.ShapeDtypeStruct((m, LANES), jnp.int32),
                   jax.ShapeDtypeStruct((m, LANES), F32)],
        compiler_params=_cparams("parallel"),
        name="moe_router",
    )(h, g.reshape(1, d), mod, mod, rw_p, rb_p)


def _routing_tables(top_idx, n_slots):
    n_tok = top_idx.shape[0]
    e_flat = top_idx.reshape(-1)
    onehot = (e_flat[:, None] == jnp.arange(N_EXPERTS)[None, :]).astype(jnp.int32)
    rank = jnp.take_along_axis(jnp.cumsum(onehot, axis=0), e_flat[:, None], axis=1)[:, 0] - 1
    counts = jnp.sum(onehot, axis=0)
    padded = (counts + EXPERT_ROWS - 1) // EXPERT_ROWS * EXPERT_ROWS
    ends = jnp.cumsum(padded)
    starts = ends - padded
    slot = starts[e_flat] + rank
    slot_token = jnp.zeros((n_slots,), jnp.int32).at[slot].set(jnp.arange(2 * n_tok, dtype=jnp.int32) // 2)
    n_super = n_slots // EXPERT_ROWS
    super_start = jnp.arange(n_super, dtype=jnp.int32) * EXPERT_ROWS
    super_expert = jnp.minimum(jnp.sum(super_start[:, None] >= ends[None, :], axis=1), N_EXPERTS - 1).astype(jnp.int32)
    valid_rows = jnp.clip(counts[super_expert] - (super_start - starts[super_expert]), 0, EXPERT_ROWS)
    valid_rows = jnp.where(super_start < ends[-1], valid_rows, 0)
    super_sub = ((valid_rows + ROUTE_TILE - 1) // ROUTE_TILE).astype(jnp.int32)
    tile_valid = (jnp.arange(n_slots // ROUTE_TILE, dtype=jnp.int32) % (EXPERT_ROWS // ROUTE_TILE)
                  < jnp.repeat(super_sub, EXPERT_ROWS // ROUTE_TILE)).astype(jnp.int32)
    return slot.astype(jnp.int32), slot_token, super_expert, super_sub, tile_valid


def _row_copy(src_hbm, dst_vmem, src_row, dst_row, sem):
    return pltpu.make_async_copy(src_hbm.at[pl.ds(src_row, 1), :], dst_vmem.at[pl.ds(dst_row, 1), :], sem)


def _gather_kernel(tok_ref, valid_ref, hn_hbm, o_ref, buf_ref, sem):
    i = pl.program_id(0)

    @pl.when(valid_ref[i] == 0)
    def _():
        o_ref[...] = jnp.zeros_like(o_ref)

    @pl.when(valid_ref[i] != 0)
    def _():
        base = i * ROUTE_TILE

        def start(r, carry):
            _row_copy(hn_hbm, buf_ref, tok_ref[base + r], r, sem).start()
            return carry

        def wait(r, carry):
            _row_copy(hn_hbm, buf_ref, 0, r, sem).wait()
            return carry

        lax.fori_loop(0, ROUTE_TILE, start, 0)
        lax.fori_loop(0, ROUTE_TILE, wait, 0)
        o_ref[...] = buf_ref[...].astype(BF16)


def _gather_rows(hn, slot_token, tile_valid):
    d = hn.shape[1]
    n_slots = slot_token.shape[0]
    return pl.pallas_call(
        _gather_kernel,
        grid_spec=pltpu.PrefetchScalarGridSpec(
            num_scalar_prefetch=2,
            grid=(n_slots // ROUTE_TILE,),
            in_specs=[pl.BlockSpec(memory_space=pl.ANY)],
            out_specs=pl.BlockSpec((ROUTE_TILE, d), lambda i, *_: (i, 0)),
            scratch_shapes=[pltpu.VMEM((ROUTE_TILE, d), F32), pltpu.SemaphoreType.DMA(())]),
        out_shape=jax.ShapeDtypeStruct((n_slots, d), BF16),
        compiler_params=_cparams("arbitrary"),
        name="moe_gather",
    )(slot_token, tile_valid, hn)


def _expert_kernel(exp_ref, sub_ref, x_ref, w1_ref, w3_ref, w2_ref, o_ref, w1b_ref, w3b_ref, w2b_ref):
    s, j = pl.program_id(0), pl.program_id(1)

    @pl.when(j == 0)
    def _():
        o_ref[...] = jnp.zeros_like(o_ref)

    @pl.when(sub_ref[s] > 0)
    def _():
        w1b_ref[...] = w1_ref[...].astype(BF16)
        w3b_ref[...] = w3_ref[...].astype(BF16)
        w2b_ref[...] = w2_ref[...].astype(BF16)

        def sub_tile(r, carry):
            rows = pl.ds(pl.multiple_of(r * ROUTE_TILE, ROUTE_TILE), ROUTE_TILE)
            xr = x_ref[rows, :]
            mid = _silu(_dot(xr, w1b_ref[...])) * _dot(xr, w3b_ref[...])
            o_ref[rows, :] += _dot(mid.astype(BF16), w2b_ref[...])
            return carry

        lax.fori_loop(0, sub_ref[s], sub_tile, 0)


def _experts(xs, super_expert, super_sub, w1, w3, w2):
    n_slots, d = xs.shape
    f = w1.shape[2]
    tf = 256
    nf = f // tf
    n_super = n_slots // EXPERT_ROWS

    def f_eff(s, j, sub):
        return jnp.where(sub[s] > 0, j, nf - 1)

    return pl.pallas_call(
        _expert_kernel,
        grid_spec=pltpu.PrefetchScalarGridSpec(
            num_scalar_prefetch=2,
            grid=(n_super, nf),
            in_specs=[pl.BlockSpec((EXPERT_ROWS, d), lambda s, j, e, sub: (s, 0)),
                      pl.BlockSpec((None, d, tf), lambda s, j, e, sub: (e[s], 0, f_eff(s, j, sub))),
                      pl.BlockSpec((None, d, tf), lambda s, j, e, sub: (e[s], 0, f_eff(s, j, sub))),
                      pl.BlockSpec((None, tf, d), lambda s, j, e, sub: (e[s], f_eff(s, j, sub), 0))],
            out_specs=pl.BlockSpec((EXPERT_ROWS, d), lambda s, j, e, sub: (s, 0)),
            scratch_shapes=[pltpu.VMEM((d, tf), BF16), pltpu.VMEM((d, tf), BF16), pltpu.VMEM((tf, d), BF16)]),
        out_shape=jax.ShapeDtypeStruct((n_slots, d), F32),
        compiler_params=_cparams("arbitrary", "arbitrary"),
        name="moe_experts",
    )(super_expert, super_sub, xs, w1, w3, w2)


def _combine_kernel(slot_ref, ys_hbm, h_ref, gate_ref, gf_ref, fg_ref, o_ref, buf_ref, sem):
    base = pl.program_id(0) * ROUTE_TILE

    def start(r, carry):
        for k in range(2):
            _row_copy(ys_hbm, buf_ref.at[k], slot_ref[2 * (base + r) + k], r, sem).start()
        return carry

    def wait(r, carry):
        for k in range(2):
            _row_copy(ys_hbm, buf_ref.at[k], 0, r, sem).wait()
        return carry

    lax.fori_loop(0, ROUTE_TILE, start, 0)
    lax.fori_loop(0, ROUTE_TILE, wait, 0)
    gates = gate_ref[...]
    y = gates[:, 0:1] * buf_ref[0] + gates[:, 1:2] * buf_ref[1]
    x = h_ref[...] + gf_ref[...] * y
    o_ref[...] = x * lax.rsqrt(jnp.mean(x * x, axis=-1, keepdims=True) + EPS) * fg_ref[...]


def _combine(ys, slot, h, gates, mod, final_g, row_fn):
    m, d = h.shape
    return pl.pallas_call(
        _combine_kernel,
        grid_spec=pltpu.PrefetchScalarGridSpec(
            num_scalar_prefetch=1,
            grid=(m // ROUTE_TILE,),
            in_specs=[pl.BlockSpec(memory_space=pl.ANY),
                      pl.BlockSpec((ROUTE_TILE, d), lambda i, *_: (i, 0)),
                      pl.BlockSpec((ROUTE_TILE, LANES), lambda i, *_: (i, 0)),
                      _mod_spec(d, GF, row_fn),
                      pl.BlockSpec((1, d), lambda i, *_: (0, 0))],
            out_specs=pl.BlockSpec((ROUTE_TILE, d), lambda i, *_: (i, 0)),
            scratch_shapes=[pltpu.VMEM((2, ROUTE_TILE, d), F32), pltpu.SemaphoreType.DMA(())]),
        out_shape=jax.ShapeDtypeStruct((m, d), F32),
        compiler_params=_cparams("arbitrary"),
        name="moe_combine_final_norm",
    )(slot, ys, h, gates, mod, final_g.reshape(1, d))


def _grid_pos_embed(rows, dim):
    r = jnp.repeat(jnp.arange(rows, dtype=F32), GRID_W)
    col = jnp.tile(jnp.arange(GRID_W, dtype=F32), rows)
    quarter = dim // 4
    freq = jnp.exp(-math.log(10000.0) * jnp.arange(quarter, dtype=F32) / quarter)
    ar = r[:, None] * freq
    ac = col[:, None] * freq
    return jnp.concatenate([jnp.sin(ar), jnp.cos(ar), jnp.sin(ac), jnp.cos(ac)], axis=-1)


def _reorder_w_in(w):
    o_ab = QKV_W
    o_z = o_ab + AB_COLS
    o_sg = o_z + GROUP_W
    o_pool = o_sg + 2 * GROUP_W
    o_cv = o_pool + GROUP_W
    pad = jnp.zeros((w.shape[0], LANES - AB_COLS), w.dtype)
    return jnp.concatenate([w[:, :QKV_W], w[:, o_z:o_sg], w[:, o_sg:o_pool], w[:, o_cv:o_cv + 2 * GROUP_W],
                            w[:, o_pool:o_cv], w[:, o_ab:o_z], pad], axis=1).astype(BF16)


def kernel(x, c, ctx, c_ctx, ada_w, ada_b, norm_mix_g, w_in, dn_conv_w, dn_a_log, dn_dt_bias, dn_norm_g,
           sg_ln_g, sg_ln_b, sg_w, sg_b, pool_w, pool_scale, cv_w, cv_b, cv_ln_g, cv_ln_b, w_out,
           norm_ffn_g, ffn_w1, ffn_w3, ffn_w2, router_w, router_b, moe_w1, moe_w3, moe_w2, final_norm_g):
    batch, seq, d = x.shape
    ctx_len = ctx.shape[1]
    depth = ada_w.shape[0]
    assert depth == 2 and d == 4 * GROUP_W and batch + 1 <= SUBLANES
    assert seq % EXPERT_ROWS == 0 and ctx_len % SEQ_TILE == 0 and (batch * ctx_len) % 512 == 0
    ctx_rows, x_rows = batch * ctx_len, batch * seq
    n_ctx_tiles, n_x_tiles = ctx_rows // SEQ_TILE, x_rows // SEQ_TILE
    seg = (n_ctx_tiles, ctx_len // SEQ_TILE, seq // SEQ_TILE)
    row_fn_for = lambda tile: functools.partial(_mod_row, tile=tile, ctx_rows=ctx_rows, seq=seq, batch=batch)
    row_fn = row_fn_for(SEQ_TILE)

    pos = _grid_pos_embed(seq // GRID_W, d)
    h = jnp.concatenate([ctx.reshape(ctx_rows, d), (x + pos[None]).reshape(x_rows, d)], axis=0)
    cc = jnp.zeros((SUBLANES, d), F32).at[:batch].set(c).at[batch].set(c_ctx)
    mod = _ada(cc, ada_w, ada_b).reshape(depth, SUBLANES, 1, 6 * d)

    def mixing(l, h, tile0, n_tiles):
        p = _proj(h, norm_mix_g[l], mod[l], _reorder_w_in(w_in[l]), row_fn)
        local = _dn_local(p, dn_conv_w[l], dn_a_log[l], dn_dt_bias[l], seg)
        o_f, o_b = _dn_scan(local, batch=batch, ctx_len=ctx_len, seq=seq)
        params = (dn_norm_g[l], sg_ln_g[l], sg_ln_b[l], sg_w[l], sg_b[l], pool_w[l], pool_scale[l],
                  cv_w[l], cv_b[l], cv_ln_g[l], cv_ln_b[l])
        return _mix(h, p, o_f, o_b, mod[l], params, w_out[l].astype(BF16),
                    seg=seg, row_fn=row_fn, tile0=tile0, n_tiles=n_tiles)

    h = mixing(0, h, 0, n_ctx_tiles + n_x_tiles)
    h = _ffn(h, norm_ffn_g[0], mod[0], ffn_w1[0].astype(BF16), ffn_w3[0].astype(BF16), ffn_w2[0].astype(BF16),
             row_fn_for)

    hx = mixing(1, h, n_ctx_tiles, n_x_tiles)
    x_row_fn = lambda i: i * ROUTE_TILE // seq
    hn, top, gates = _router(hx, norm_ffn_g[1], mod[1], router_w[0], router_b[0], x_row_fn, 0)
    n_slots = 2 * x_rows + N_EXPERTS * EXPERT_ROWS
    slot, slot_token, super_expert, super_sub, tile_valid = _routing_tables(top[:, :2], n_slots)
    xs = _gather_rows(hn, slot_token, tile_valid)
    ys = _experts(xs, super_expert, super_sub, moe_w1[0], moe_w3[0], moe_w2[0])
    out = _combine(ys, slot, hx, gates, mod[1], final_norm_g, x_row_fn)
    return out.reshape(batch, seq, d)
```

```python
import functools
import math

import jax
import jax.numpy as jnp
from jax import lax
from jax.experimental import pallas as pl
from jax.experimental.pallas import tpu as pltpu

F32 = jnp.float32
BF16 = jnp.bfloat16
EPS = 1e-6

GRID_W = 64
N_HEADS = 4
HEAD_DIM = 128
GROUP_W = N_HEADS * HEAD_DIM
DN_CHUNK = 64
SEQ_TILE = 256
CHUNKS_PER_TILE = SEQ_TILE // DN_CHUNK
SG_CHUNK = 128
POOL_WINDOWS = (2, 4, 8, 16)
SHORT_CONV = 5
CONF_CONV = 31
N_EXPERTS = 8
LANES = 128
SUBLANES = 8
VMEM_LIMIT = 56 * 1024 * 1024

QKV_OFF, QKV_W = 0, 3 * GROUP_W
Z_OFF = QKV_OFF + QKV_W
SG_OFF = Z_OFF + GROUP_W
CV_OFF = SG_OFF + 2 * GROUP_W
POOL_OFF = CV_OFF + 2 * GROUP_W
AB_OFF = POOL_OFF + GROUP_W
P_WIDTH = AB_OFF + LANES
AB_COLS = 4 * N_HEADS

SHM, SCM, GM, SHF, SCF, GF = range(6)


def _dot(a, b):
    return jnp.dot(a, b, preferred_element_type=F32)


def _dot_nt(a, b):
    return lax.dot_general(a, b, (((1,), (1,)), ((), ())), preferred_element_type=F32)


def _dot_tn(a, b):
    return lax.dot_general(a, b, (((0,), (0,)), ((), ())), preferred_element_type=F32)


def _sigmoid(x):
    return 1.0 / (1.0 + jnp.exp(-x))


def _silu(x):
    return x * _sigmoid(x)


def _cparams(*sem):
    return pltpu.CompilerParams(dimension_semantics=sem, vmem_limit_bytes=VMEM_LIMIT)


def _mod_row(i, *, tile, ctx_rows, seq, batch):
    start = i * tile
    return jnp.where(start < ctx_rows, batch, (start - ctx_rows) // seq)


def _mod_spec(d, chunk, row_fn):
    return pl.BlockSpec((None, 1, d), lambda i, *_: (row_fn(i), 0, chunk))


def _norm_modulate(x, g, scale, shift):
    y = x * lax.rsqrt(jnp.mean(x * x, axis=-1, keepdims=True) + EPS) * g
    return y * (1.0 + scale) + shift


def _ada_kernel(c_ref, w_ref, b_ref, o_ref):
    s = _silu(c_ref[...]).astype(BF16)
    o_ref[...] = _dot(s, w_ref[...].astype(BF16)) + b_ref[...]


def _ada(cc, ada_w, ada_b):
    nl, d, n = ada_w.shape
    tn = 1024
    return pl.pallas_call(
        _ada_kernel,
        grid=(nl, n // tn),
        in_specs=[pl.BlockSpec((SUBLANES, d), lambda l, j: (0, 0)),
                  pl.BlockSpec((None, d, tn), lambda l, j: (l, 0, j)),
                  pl.BlockSpec((None, 1, tn), lambda l, j: (l, 0, j))],
        out_specs=pl.BlockSpec((None, SUBLANES, tn), lambda l, j: (l, 0, j)),
        out_shape=jax.ShapeDtypeStruct((nl, SUBLANES, n), F32),
        compiler_params=_cparams("parallel", "parallel"),
        name="ada_modulation",
    )(cc, ada_w, ada_b.reshape(nl, 1, n))


def _proj_kernel(h_ref, g_ref, sh_ref, sc_ref, w_ref, o_ref):
    y = _norm_modulate(h_ref[...], g_ref[...], sc_ref[...], sh_ref[...])
    o_ref[...] = _dot(y.astype(BF16), w_ref[...])


def _proj(h, g, mod, w, row_fn):
    m, d = h.shape
    n = w.shape[1]
    tm = SEQ_TILE
    return pl.pallas_call(
        _proj_kernel,
        grid=(m // tm,),
        in_specs=[pl.BlockSpec((tm, d), lambda i: (i, 0)),
                  pl.BlockSpec((1, d), lambda i: (0, 0)),
                  _mod_spec(d, SHM, row_fn),
                  _mod_spec(d, SCM, row_fn),
                  pl.BlockSpec((d, n), lambda i: (0, 0), pipeline_mode=pl.Buffered(1))],
        out_specs=pl.BlockSpec((tm, n), lambda i: (i, 0)),
        out_shape=jax.ShapeDtypeStruct((m, n), F32),
        compiler_params=_cparams("parallel"),
        name="in_proj",
    )(h, g.reshape(1, d), mod, mod, w)


def _segment_pos(i, n_ctx_tiles, ctx_seg_tiles, x_seg_tiles):
    in_ctx = i < n_ctx_tiles
    tpos = jnp.where(in_ctx, i % ctx_seg_tiles, (i - n_ctx_tiles) % x_seg_tiles)
    tlen = jnp.where(in_ctx, ctx_seg_tiles, x_seg_tiles)
    return tpos, tlen


def _fill_ext(ext_ref, prev, cur, nxt, first, last, halo):
    ext_ref[0:halo, :] = jnp.where(first, 0.0, prev)
    ext_ref[halo:halo + SEQ_TILE, :] = cur
    ext_ref[halo + SEQ_TILE:2 * halo + SEQ_TILE, :] = jnp.where(last, 0.0, nxt)


PK_W, PK_QG, PK_KD, PK_QK = 0, GROUP_W, 2 * GROUP_W, 3 * GROUP_W
PK_WIDTH = 4 * GROUP_W


def _dn_local_kernel(qkv_ref, prev_ref, next_ref, ab_ref, cw_ref, alog_ref, dtb_ref,
                     uf_ref, pkf_ref, ub_ref, pkb_ref, gam_ref,
                     ext_ref, t_ref, l_ref, rhs_ref, *, seg):
    tpos, tlen = _segment_pos(pl.program_id(0), *seg)
    halo = SUBLANES
    _fill_ext(ext_ref, prev_ref[...], qkv_ref[...], next_ref[...], tpos == 0, tpos == tlen - 1, halo)

    pad = SHORT_CONV // 2
    acc = cw_ref[0:1, :] * ext_ref[halo - pad:halo - pad + SEQ_TILE, :]
    for j in range(1, SHORT_CONV):
        acc = acc + cw_ref[j:j + 1, :] * ext_ref[halo - pad + j:halo - pad + j + SEQ_TILE, :]
    s = _silu(acc)

    ab = ab_ref[...]
    lane = lax.broadcasted_iota(jnp.int32, (SEQ_TILE, LANES), 1)
    xs = ab + dtb_ref[...]
    softplus = jnp.maximum(xs, 0.0) + jnp.log1p(jnp.exp(-jnp.abs(xs)))
    la = jnp.where(lane < 2 * N_HEADS, -jnp.exp(alog_ref[...]) * softplus, 0.0)
    beta = _sigmoid(ab)

    r = lax.broadcasted_iota(jnp.int32, (SEQ_TILE, SEQ_TILE), 0)
    c = lax.broadcasted_iota(jnp.int32, (SEQ_TILE, SEQ_TILE), 1)
    same = (r // DN_CHUNK) == (c // DN_CHUNK)
    incl = (same & (c <= r), same & (c >= r))
    strict = (same & (c < r), same & (c > r))

    def off_block(s, d):
        hi, lo = (r, c) if d == 0 else (c, r)
        return ((hi // s) % 2 == 1) & ((lo // s) == (hi // s) - 1)

    la_hi = la.astype(BF16)
    rem = la - la_hi.astype(F32)
    la_mid = rem.astype(BF16)
    la_lo = (rem - la_mid.astype(F32)).astype(BF16)

    def chunk_sum(mask):
        m = jnp.where(mask, 1.0, 0.0).astype(BF16)
        return _dot(m, la_hi) + _dot(m, la_mid) + _dot(m, la_lo)

    g_fwd = chunk_sum(incl[0])
    g_bwd = chunk_sum(incl[1])
    g = jnp.where(lane < N_HEADS, g_fwd, g_bwd)
    g_tot = g_fwd + g_bwd - la
    g_t = g.T

    outs = ((uf_ref, pkf_ref), (ub_ref, pkb_ref))
    eye = jnp.where(r == c, 1.0, 0.0)
    for h in range(N_HEADS):
        sl = slice(h * HEAD_DIM, (h + 1) * HEAD_DIM)
        qh = s[:, h * HEAD_DIM:(h + 1) * HEAD_DIM]
        kh = s[:, GROUP_W + h * HEAD_DIM:GROUP_W + (h + 1) * HEAD_DIM]
        vh = s[:, 2 * GROUP_W + h * HEAD_DIM:2 * GROUP_W + (h + 1) * HEAD_DIM]
        qh = qh * lax.rsqrt(jnp.sum(qh * qh, axis=-1, keepdims=True) + EPS) * (HEAD_DIM ** -0.5)
        kh = kh * lax.rsqrt(jnp.sum(kh * kh, axis=-1, keepdims=True) + EPS)
        kh16 = kh.astype(BF16)
        qq = _dot_nt(qh.astype(BF16), kh16)
        kk = _dot_nt(kh16, kh16)
        for d in range(2):
            gi = d * N_HEADS + h
            u_ref, pk_ref = outs[d]
            gcol = g[:, gi:gi + 1]
            grow = g_t[gi:gi + 1, :]
            decay = jnp.where(incl[d], jnp.exp(jnp.where(incl[d], gcol - grow, 0.0)), 0.0)
            bcol = beta[:, 2 * N_HEADS + gi:2 * N_HEADS + gi + 1]
            eg = jnp.exp(gcol)
            kb = kh * bcol
            rhs_ref[gi, :, :HEAD_DIM] = vh * bcol
            rhs_ref[gi, :, HEAD_DIM:] = kb * eg
            lmat = jnp.where(strict[d], bcol * kk * decay, 0.0)
            l_ref[gi] = lmat
            t_ref[gi] = eye - jnp.where(off_block(1, d), lmat, 0.0)
            qkm = jnp.where(incl[d], qq * decay, 0.0)
            qk = qkm[:, 0:DN_CHUNK]
            for cb in range(1, CHUNKS_PER_TILE):
                qk = qk + qkm[:, cb * DN_CHUNK:(cb + 1) * DN_CHUNK]
            pk_ref[:, PK_QK + h * HEAD_DIM:PK_QK + h * HEAD_DIM + DN_CHUNK] = qk.astype(BF16)
            pk_ref[:, PK_QK + h * HEAD_DIM + DN_CHUNK:PK_QK + (h + 1) * HEAD_DIM] = jnp.zeros(
                (SEQ_TILE, HEAD_DIM - DN_CHUNK), BF16)
            gl = g_tot[:, gi:gi + 1]
            pk_ref[:, PK_QG + h * HEAD_DIM:PK_QG + (h + 1) * HEAD_DIM] = (qh * eg).astype(BF16)
            pk_ref[:, PK_KD + h * HEAD_DIM:PK_KD + (h + 1) * HEAD_DIM] = (kh * jnp.exp(gl - gcol)).astype(BF16)
            for cb in range(CHUNKS_PER_TILE):
                gam_ref[cb, :, gi * LANES:(gi + 1) * LANES] = jnp.broadcast_to(
                    jnp.exp(gl[cb * DN_CHUNK:cb * DN_CHUNK + 1, :]), (1, LANES))

    n_chain = 2 * N_HEADS
    s_blk = 2
    while s_blk < DN_CHUNK:
        lt = []
        for gi in range(n_chain):
            l_s = jnp.where(off_block(s_blk, gi // N_HEADS), l_ref[gi], 0.0).astype(BF16)
            lt.append(_dot(l_s, t_ref[gi].astype(BF16)).astype(BF16))
        for gi in range(n_chain):
            t_inv = t_ref[gi]
            t_ref[gi] = t_inv - _dot(t_inv.astype(BF16), lt[gi])
        s_blk *= 2
    for gi in range(n_chain):
        d, h = divmod(gi, N_HEADS)
        u_ref, pk_ref = outs[d]
        rhs = rhs_ref[gi]
        sol = rhs + _dot(jnp.where(strict[d], t_ref[gi], 0.0).astype(BF16), rhs.astype(BF16))
        u_ref[:, h * HEAD_DIM:(h + 1) * HEAD_DIM] = sol[:, :HEAD_DIM]
        pk_ref[:, PK_W + h * HEAD_DIM:PK_W + (h + 1) * HEAD_DIM] = sol[:, HEAD_DIM:].astype(BF16)


def _batch_major_tile(i, n_ctx_tiles, ctx_seg_tiles, x_seg_tiles):
    per_batch = ctx_seg_tiles + x_seg_tiles
    j = i - n_ctx_tiles
    return jnp.where(i < n_ctx_tiles,
                     (i // ctx_seg_tiles) * per_batch + i % ctx_seg_tiles,
                     (j // x_seg_tiles) * per_batch + ctx_seg_tiles + j % x_seg_tiles)


def _dn_local(p, conv_w, a_log, dt_bias, seg):
    rows = p.shape[0]
    nt = rows // SEQ_TILE
    hb = SEQ_TILE // SUBLANES
    last_hb = rows // SUBLANES - 1
    cw = jnp.zeros((SUBLANES, QKV_W), F32).at[:SHORT_CONV].set(conv_w)
    pad_row = lambda v: jnp.zeros((1, LANES), F32).at[0, :2 * N_HEADS].set(v.reshape(-1))
    u_out = jax.ShapeDtypeStruct((rows, GROUP_W), F32)
    pk_out = jax.ShapeDtypeStruct((rows, PK_WIDTH), BF16)
    gam_out = jax.ShapeDtypeStruct((rows // DN_CHUNK, 1, 2 * N_HEADS * LANES), F32)
    dst = lambda i: _batch_major_tile(i, *seg)
    u_spec = pl.BlockSpec((SEQ_TILE, GROUP_W), lambda i: (dst(i), 0))
    pk_spec = pl.BlockSpec((SEQ_TILE, PK_WIDTH), lambda i: (dst(i), 0))
    full = lambda shape: pl.BlockSpec(shape, lambda i: (0,) * len(shape))
    n_chain = 2 * N_HEADS
    return pl.pallas_call(
        functools.partial(_dn_local_kernel, seg=seg),
        grid=(nt,),
        in_specs=[pl.BlockSpec((SEQ_TILE, QKV_W), lambda i: (i, QKV_OFF // QKV_W)),
                  pl.BlockSpec((SUBLANES, QKV_W), lambda i: (jnp.maximum(i * hb - 1, 0), 0)),
                  pl.BlockSpec((SUBLANES, QKV_W), lambda i: (jnp.minimum((i + 1) * hb, last_hb), 0)),
                  pl.BlockSpec((SEQ_TILE, LANES), lambda i: (i, AB_OFF // LANES)),
                  full((SUBLANES, QKV_W)), full((1, LANES)), full((1, LANES))],
        out_specs=[u_spec, pk_spec, u_spec, pk_spec,
                   pl.BlockSpec((CHUNKS_PER_TILE, 1, 2 * N_HEADS * LANES), lambda i: (dst(i), 0, 0))],
        out_shape=[u_out, pk_out, u_out, pk_out, gam_out],
        scratch_shapes=[pltpu.VMEM((SEQ_TILE + 2 * SUBLANES, QKV_W), F32),
                        pltpu.VMEM((n_chain, SEQ_TILE, SEQ_TILE), F32),
                        pltpu.VMEM((n_chain, SEQ_TILE, SEQ_TILE), F32),
                        pltpu.VMEM((n_chain, SEQ_TILE, 2 * HEAD_DIM), F32)],
        compiler_params=_cparams("parallel"),
        name="dn_local",
    )(p, p, p, p, cw, pad_row(a_log), pad_row(dt_bias))


def _dn_scan_kernel(uf, pkf, gamf, ub, pkb, gamb, of_ref, ob_ref, s_ref, *, batch):
    @pl.when(pl.program_id(0) == 0)
    def _():
        s_ref[...] = jnp.zeros_like(s_ref)

    dirs = ((uf, pkf, gamf, of_ref), (ub, pkb, gamb, ob_ref))
    chains = [(b, d, h) for b in range(batch) for d in range(2) for h in range(N_HEADS)]
    head = lambda off, h: slice(off + h * HEAD_DIM, off + (h + 1) * HEAD_DIM)

    ws = []
    for ci, (b, d, h) in enumerate(chains):
        pk = dirs[d][1]
        lhs = jnp.concatenate([pk[b, :, head(PK_W, h)], pk[b, :, head(PK_QG, h)]], axis=0)
        ws.append(_dot(lhs, s_ref[ci].astype(BF16)))
    v16 = []
    for ci, (b, d, h) in enumerate(chains):
        v16.append((dirs[d][0][b, :, head(0, h)] - ws[ci][:DN_CHUNK]).astype(BF16))
    for ci, (b, d, h) in enumerate(chains):
        pk, o_ref = dirs[d][1], dirs[d][3]
        qk = pk[b, :, PK_QK + h * HEAD_DIM:PK_QK + h * HEAD_DIM + DN_CHUNK]
        o_ref[b, :, head(0, h)] = ws[ci][DN_CHUNK:] + _dot(qk, v16[ci])
    for ci, (b, d, h) in enumerate(chains):
        pk, gam = dirs[d][1], dirs[d][2]
        gi = d * N_HEADS + h
        s_ref[ci] = s_ref[ci] * gam[b, :, gi * LANES:(gi + 1) * LANES] + _dot_tn(pk[b, :, head(PK_KD, h)], v16[ci])


def _dn_scan(local, *, batch, ctx_len, seq):
    uf, pkf, ub, pkb, gam = local
    cc, xc = ctx_len // DN_CHUNK, seq // DN_CHUNK
    nc = cc + xc
    bwd = lambda s: jnp.where(s < cc, cc - 1 - s, cc + (xc - 1 - (s - cc)))
    by_chunk = lambda a: a.reshape(batch, nc, *a.shape[1:]) if a.ndim == 3 else a.reshape(batch, nc, DN_CHUNK, a.shape[1])

    def specs(idx):
        return [pl.BlockSpec((batch, None, DN_CHUNK, GROUP_W), lambda s: (0, idx(s), 0, 0)),
                pl.BlockSpec((batch, None, DN_CHUNK, PK_WIDTH), lambda s: (0, idx(s), 0, 0)),
                pl.BlockSpec((batch, None, 1, 2 * N_HEADS * LANES), lambda s: (0, idx(s), 0, 0))]

    out = jax.ShapeDtypeStruct((batch, nc, DN_CHUNK, GROUP_W), F32)
    fwd = lambda s: s
    o_f, o_b = pl.pallas_call(
        functools.partial(_dn_scan_kernel, batch=batch),
        grid=(nc,),
        in_specs=specs(fwd) + specs(bwd),
        out_specs=[pl.BlockSpec((batch, None, DN_CHUNK, GROUP_W), lambda s: (0, s, 0, 0)),
                   pl.BlockSpec((batch, None, DN_CHUNK, GROUP_W), lambda s: (0, bwd(s), 0, 0))],
        out_shape=[out, out],
        scratch_shapes=[pltpu.VMEM((batch * 2 * N_HEADS, HEAD_DIM, HEAD_DIM), F32)],
        compiler_params=_cparams("arbitrary"),
        name="dn_scan",
    )(by_chunk(uf), by_chunk(pkf), by_chunk(gam), by_chunk(ub), by_chunk(pkb), by_chunk(gam))
    return o_f.reshape(-1, GROUP_W), o_b.reshape(-1, GROUP_W)


def _layernorm(x, g, b):
    mu = jnp.mean(x, axis=-1, keepdims=True)
    xc = x - mu
    var = jnp.mean(xc * xc, axis=-1, keepdims=True)
    return xc * lax.rsqrt(var + EPS) * g + b


def _mix_kernel(h_ref, of_ref, ob_ref, z_ref, sg_ref, cv_ref, cvp_ref, cvn_ref, pool_ref, poolp_ref, pooln_ref,
                gm_ref, dng_ref, sglg_ref, sglb_ref, sgw_ref, sgbt_ref, pw_ref, ps_ref,
                cvw_ref, cvb_ref, cvlg_ref, cvlb_ref, wout_ref,
                o_ref, y_ref, pext_ref, cext_ref, cacc_ref, *, seg, tile0):
    tpos, tlen = _segment_pos(pl.program_id(0) + tile0, *seg)
    first, last = tpos == 0, tpos == tlen - 1

    o = of_ref[...] + ob_ref[...]
    z = z_ref[...]
    for h in range(N_HEADS):
        sl = slice(h * HEAD_DIM, (h + 1) * HEAD_DIM)
        oh = o[:, sl]
        yh = oh * lax.rsqrt(jnp.mean(oh * oh, axis=-1, keepdims=True) + EPS) * dng_ref[...] * _silu(z[:, sl])
        y_ref[:, sl] = yh.astype(BF16)

    psg = sg_ref[...]
    psg = 0.5 * psg * (1.0 + jnp.tanh(math.sqrt(2.0 / math.pi) * (psg + 0.044715 * (psg * psg * psg))))
    u = psg[:, :GROUP_W]
    v = _layernorm(psg[:, GROUP_W:], sglg_ref[...], sglb_ref[...]).astype(BF16)
    for n in range(SEQ_TILE // SG_CHUNK):
        rs = slice(n * SG_CHUNK, (n + 1) * SG_CHUNK)
        for h in range(N_HEADS):
            sl = slice(h * HEAD_DIM, (h + 1) * HEAD_DIM)
            sv = _dot(sgw_ref[h], v[rs, sl]) + sgbt_ref[:, h:h + 1]
            y_ref[rs, GROUP_W + h * HEAD_DIM:GROUP_W + (h + 1) * HEAD_DIM] = (u[rs, sl] * sv).astype(BF16)

    halo = SUBLANES
    _fill_ext(pext_ref, poolp_ref[...], pool_ref[...], pooln_ref[...], first, last, halo)
    t = tpos * SEQ_TILE + lax.broadcasted_iota(jnp.int32, (SEQ_TILE, 1), 0)
    seg_len = tlen * SEQ_TILE
    for gi, win in enumerate(POOL_WINDOWS):
        sl = slice(gi * LANES, (gi + 1) * LANES)
        tot = pext_ref[halo - win // 2:halo - win // 2 + SEQ_TILE, sl]
        for m in range(1 - win // 2, win // 2):
            tot = tot + pext_ref[halo + m:halo + m + SEQ_TILE, sl]
        cnt = jnp.clip(t + win // 2, 0, seg_len) - jnp.clip(t - win // 2, 0, seg_len)
        yg = tot / cnt.astype(F32) - pool_ref[:, sl]
        yg = _dot(yg.astype(BF16), pw_ref[gi]) * ps_ref[:, sl]
        y_ref[:, 2 * GROUP_W + gi * LANES:2 * GROUP_W + (gi + 1) * LANES] = yg.astype(BF16)

    halo = 2 * SUBLANES
    glu = lambda p: p[:, :GROUP_W] * _sigmoid(p[:, GROUP_W:])
    _fill_ext(cext_ref, glu(cvp_ref[...]), glu(cv_ref[...]), glu(cvn_ref[...]), first, last, halo)
    pad = CONF_CONV // 2
    span = SEQ_TILE + SUBLANES
    acc = None
    for rho in range(SUBLANES):
        part = None
        for j in range(CONF_CONV):
            off = halo - pad + j
            if off % SUBLANES == rho:
                term = cvw_ref[j:j + 1, :] * cext_ref[off - rho:off - rho + span, :]
                part = term if part is None else part + term
        if part is not None:
            cacc_ref[...] = part
            shifted = cacc_ref[rho:rho + SEQ_TILE, :]
            acc = shifted if acc is None else acc + shifted
    yc = _silu(_layernorm(acc + cvb_ref[...], cvlg_ref[...], cvlb_ref[...]))
    y_ref[:, 3 * GROUP_W:] = yc.astype(BF16)

    o_ref[...] = h_ref[...] + gm_ref[...] * _dot(y_ref[...], wout_ref[...])


def _mix(h, p, o_f, o_b, mod, params, wout, *, seg, row_fn, tile0, n_tiles):
    d = h.shape[1]
    (dng, sglg, sglb, sgw, sgb, pw, ps, cvw, cvb, cvlg, cvlb) = params
    rows = p.shape[0]
    row = lambda v: v.reshape(1, -1)
    cvw_p = jnp.zeros((4 * SUBLANES, GROUP_W), F32).at[:CONF_CONV].set(cvw)
    sgbt = jnp.zeros((SG_CHUNK, LANES), F32).at[:, :N_HEADS].set(sgb.T)

    def halo_specs(width, col, halo):
        hb = SEQ_TILE // halo
        last_hb = rows // halo - 1
        return [pl.BlockSpec((SEQ_TILE, width), lambda i: (i + tile0, col)),
                pl.BlockSpec((halo, width), lambda i: (jnp.maximum((i + tile0) * hb - 1, 0), col)),
                pl.BlockSpec((halo, width), lambda i: (jnp.minimum((i + tile0 + 1) * hb, last_hb), col))]

    full = lambda a: pl.BlockSpec(a.shape, lambda i: (0,) * a.ndim)
    consts = [row(dng), row(sglg), row(sglb), sgw.astype(BF16), sgbt, pw.astype(BF16), row(ps),
              cvw_p, row(cvb), row(cvlg), row(cvlb)]
    return pl.pallas_call(
        functools.partial(_mix_kernel, seg=seg, tile0=tile0),
        grid=(n_tiles,),
        in_specs=[pl.BlockSpec((SEQ_TILE, d), lambda i: (i + tile0, 0)),
                  pl.BlockSpec((SEQ_TILE, GROUP_W), lambda i: (_batch_major_tile(i + tile0, *seg), 0)),
                  pl.BlockSpec((SEQ_TILE, GROUP_W), lambda i: (_batch_major_tile(i + tile0, *seg), 0)),
                  pl.BlockSpec((SEQ_TILE, GROUP_W), lambda i: (i + tile0, Z_OFF // GROUP_W)),
                  pl.BlockSpec((SEQ_TILE, 2 * GROUP_W), lambda i: (i + tile0, SG_OFF // (2 * GROUP_W)))]
                 + halo_specs(2 * GROUP_W, CV_OFF // (2 * GROUP_W), 2 * SUBLANES)
                 + halo_specs(GROUP_W, POOL_OFF // GROUP_W, SUBLANES)
                 + [_mod_spec(d, GM, lambda i: row_fn(i + tile0))]
                 + [full(a) for a in consts]
                 + [pl.BlockSpec(wout.shape, lambda i: (0, 0), pipeline_mode=pl.Buffered(1))],
        out_specs=pl.BlockSpec((SEQ_TILE, d), lambda i: (i, 0)),
        out_shape=jax.ShapeDtypeStruct((n_tiles * SEQ_TILE, d), F32),
        scratch_shapes=[pltpu.VMEM((SEQ_TILE, d), BF16),
                        pltpu.VMEM((SEQ_TILE + 2 * SUBLANES, GROUP_W), F32),
                        pltpu.VMEM((SEQ_TILE + 4 * SUBLANES, GROUP_W), F32),
                        pltpu.VMEM((SEQ_TILE + SUBLANES, GROUP_W), F32)],
        compiler_params=_cparams("parallel"),
        name="mixers_out_proj",
    )(h, o_f, o_b, p, p, p, p, p, p, p, p, mod, *consts, wout)


FFN_CHUNK = 512


def _ffn_kernel(h_ref, g_ref, sh_ref, sc_ref, gate_ref, w1_ref, w3_ref, w2_ref, o_ref, xn_ref):
    j = pl.program_id(1)

    chunks = [slice(r, r + FFN_CHUNK) for r in range(0, o_ref.shape[0], FFN_CHUNK)]

    @pl.when(j == 0)
    def _():
        for rows in chunks:
            xn_ref[rows, :] = _norm_modulate(h_ref[rows, :], g_ref[...], sc_ref[...], sh_ref[...]).astype(BF16)

    for rows in chunks:
        xn = xn_ref[rows, :]
        mid = _silu(_dot(xn, w1_ref[...])) * _dot(xn, w3_ref[...])
        part = _dot(mid.astype(BF16), w2_ref[...])

        @pl.when(j == 0)
        def _():
            o_ref[rows, :] = part

        @pl.when(j > 0)
        def _():
            o_ref[rows, :] += part

    @pl.when(j == pl.num_programs(1) - 1)
    def _():
        for rows in chunks:
            o_ref[rows, :] = h_ref[rows, :] + gate_ref[...] * o_ref[rows, :]


def _ffn(h, g, mod, w1, w3, w2, row_fn_for, tm):
    m, d = h.shape
    f = w1.shape[1]
    tf = 512
    row_fn = row_fn_for(tm)
    return pl.pallas_call(
        _ffn_kernel,
        grid=(m // tm, f // tf),
        in_specs=[pl.BlockSpec((tm, d), lambda i, j: (i, 0), pipeline_mode=pl.Buffered(1)),
                  pl.BlockSpec((1, d), lambda i, j: (0, 0)),
                  _mod_spec(d, SHF, row_fn), _mod_spec(d, SCF, row_fn), _mod_spec(d, GF, row_fn),
                  pl.BlockSpec((d, tf), lambda i, j: (0, j)),
                  pl.BlockSpec((d, tf), lambda i, j: (0, j)),
                  pl.BlockSpec((tf, d), lambda i, j: (j, 0))],
        out_specs=pl.BlockSpec((tm, d), lambda i, j: (i, 0)),
        out_shape=jax.ShapeDtypeStruct((m, d), F32),
        scratch_shapes=[pltpu.VMEM((tm, d), BF16)],
        compiler_params=_cparams("parallel", "arbitrary"),
        name="dense_swiglu",
    )(h, g.reshape(1, d), mod, mod, mod, w1, w3, w2)


ROUTE_TILE = 256
EXPERT_ROWS = 2048


def _router_kernel(h_ref, g_ref, sh_ref, sc_ref, rw_ref, rb_ref, hn_ref, idx_ref, gate_ref):
    hn = _norm_modulate(h_ref[...], g_ref[...], sc_ref[...], sh_ref[...])
    hn_ref[...] = hn
    logits = jnp.dot(hn, rw_ref[...], preferred_element_type=F32, precision=lax.Precision.HIGHEST) + rb_ref[...]
    lane = lax.broadcasted_iota(jnp.int32, logits.shape, 1)
    lane_f = lane.astype(F32)
    neg = -jnp.inf
    lg = jnp.where(lane < N_EXPERTS, logits, neg)
    m1 = jnp.max(lg, axis=-1, keepdims=True)
    i1 = jnp.min(jnp.where(lg == m1, lane_f, float(LANES)), axis=-1, keepdims=True)
    lg2 = jnp.where(lane_f == i1, neg, lg)
    m2 = jnp.max(lg2, axis=-1, keepdims=True)
    i2 = jnp.min(jnp.where(lg2 == m2, lane_f, float(LANES)), axis=-1, keepdims=True)
    e2 = jnp.exp(m2 - m1)
    g1 = 1.0 / (1.0 + e2)
    g2 = e2 / (1.0 + e2)
    idx_ref[...] = jnp.where(lane == 0, i1, jnp.where(lane == 1, i2, 0.0)).astype(jnp.int32)
    gate_ref[...] = jnp.where(lane == 0, g1, jnp.where(lane == 1, g2, 0.0))


def _router(h, g, mod, rw, rb, row_fn, tile0):
    d = h.shape[1]
    tm = ROUTE_TILE
    n_tiles = h.shape[0] // tm - tile0
    m = n_tiles * tm
    rw_p = jnp.zeros((d, LANES), F32).at[:, :N_EXPERTS].set(rw)
    rb_p = jnp.zeros((1, LANES), F32).at[0, :N_EXPERTS].set(rb)
    shifted = lambda i: row_fn(i + tile0)
    return pl.pallas_call(
        _router_kernel,
        grid=(n_tiles,),
        in_specs=[pl.BlockSpec((tm, d), lambda i: (i + tile0, 0)),
                  pl.BlockSpec((1, d), lambda i: (0, 0)),
                  _mod_spec(d, SHF, shifted), _mod_spec(d, SCF, shifted),
                  pl.BlockSpec((d, LANES), lambda i: (0, 0)),
                  pl.BlockSpec((1, LANES), lambda i: (0, 0))],
        out_specs=[pl.BlockSpec((tm, d), lambda i: (i, 0)),
                   pl.BlockSpec((tm, LANES), lambda i: (i, 0)),
                   pl.BlockSpec((tm, LANES), lambda i: (i, 0))],
        out_shape=[jax.ShapeDtypeStruct((m, d), F32),
                   jax.ShapeDtypeStruct((m, LANES), jnp.int32),
                   jax.ShapeDtypeStruct((m, LANES), F32)],
        compiler_params=_cparams("parallel"),
        name="moe_router",
    )(h, g.reshape(1, d), mod, mod, rw_p, rb_p)


def _routing_tables(top_idx, n_slots):
    n_tok = top_idx.shape[0]
    e_flat = top_idx.reshape(-1)
    onehot = (e_flat[:, None] == jnp.arange(N_EXPERTS)[None, :]).astype(jnp.int32)
    rank = jnp.take_along_axis(jnp.cumsum(onehot, axis=0), e_flat[:, None], axis=1)[:, 0] - 1
    counts = jnp.sum(onehot, axis=0)
    padded = (counts + EXPERT_ROWS - 1) // EXPERT_ROWS * EXPERT_ROWS
    ends = jnp.cumsum(padded)
    starts = ends - padded
    slot = starts[e_flat] + rank
    slot_token = jnp.zeros((n_slots,), jnp.int32).at[slot].set(jnp.arange(2 * n_tok, dtype=jnp.int32) // 2)
    n_super = n_slots // EXPERT_ROWS
    super_start = jnp.arange(n_super, dtype=jnp.int32) * EXPERT_ROWS
    super_expert = jnp.minimum(jnp.sum(super_start[:, None] >= ends[None, :], axis=1), N_EXPERTS - 1).astype(jnp.int32)
    valid_rows = jnp.clip(counts[super_expert] - (super_start - starts[super_expert]), 0, EXPERT_ROWS)
    valid_rows = jnp.where(super_start < ends[-1], valid_rows, 0)
    super_sub = ((valid_rows + ROUTE_TILE - 1) // ROUTE_TILE).astype(jnp.int32)
    tile_valid = (jnp.arange(n_slots // ROUTE_TILE, dtype=jnp.int32) % (EXPERT_ROWS // ROUTE_TILE)
                  < jnp.repeat(super_sub, EXPERT_ROWS // ROUTE_TILE)).astype(jnp.int32)
    return slot.astype(jnp.int32), slot_token, super_expert, super_sub, tile_valid


def _row_copy(src_hbm, dst_vmem, src_row, dst_row, sem):
    return pltpu.make_async_copy(src_hbm.at[pl.ds(src_row, 1), :], dst_vmem.at[pl.ds(dst_row, 1), :], sem)


def _gather_kernel(tok_ref, valid_ref, hn_hbm, o_ref, buf_ref, sem):
    i = pl.program_id(0)
    n = pl.num_programs(0)

    def request(tile):
        slot = tile % 2

        def start(r, carry):
            _row_copy(hn_hbm, buf_ref.at[slot], tok_ref[tile * ROUTE_TILE + r], r, sem.at[slot]).start()
            return carry

        lax.fori_loop(0, ROUTE_TILE, start, 0)

    @pl.when((i == 0) & (valid_ref[0] != 0))
    def _():
        request(0)

    nxt = jnp.minimum(i + 1, n - 1)

    @pl.when((i + 1 < n) & (valid_ref[nxt] != 0))
    def _():
        request(nxt)

    @pl.when(valid_ref[i] == 0)
    def _():
        o_ref[...] = jnp.zeros_like(o_ref)

    @pl.when(valid_ref[i] != 0)
    def _():
        slot = i % 2

        def wait(r, carry):
            _row_copy(hn_hbm, buf_ref.at[slot], 0, r, sem.at[slot]).wait()
            return carry

        lax.fori_loop(0, ROUTE_TILE, wait, 0)
        o_ref[...] = buf_ref[slot].astype(BF16)


def _gather_rows(hn, slot_token, tile_valid):
    d = hn.shape[1]
    n_slots = slot_token.shape[0]
    return pl.pallas_call(
        _gather_kernel,
        grid_spec=pltpu.PrefetchScalarGridSpec(
            num_scalar_prefetch=2,
            grid=(n_slots // ROUTE_TILE,),
            in_specs=[pl.BlockSpec(memory_space=pl.ANY)],
            out_specs=pl.BlockSpec((ROUTE_TILE, d), lambda i, *_: (i, 0)),
            scratch_shapes=[pltpu.VMEM((2, ROUTE_TILE, d), F32), pltpu.SemaphoreType.DMA((2,))]),
        out_shape=jax.ShapeDtypeStruct((n_slots, d), BF16),
        compiler_params=_cparams("arbitrary"),
        name="moe_gather",
    )(slot_token, tile_valid, hn)


WEIGHT_SPLIT = 4


def _expert_kernel(exp_ref, sub_ref, x_ref, *refs):
    w1_parts = refs[:WEIGHT_SPLIT]
    w3_parts = refs[WEIGHT_SPLIT:2 * WEIGHT_SPLIT]
    w2_parts = refs[2 * WEIGHT_SPLIT:3 * WEIGHT_SPLIT]
    o_ref, w1b_ref, w3b_ref, w2b_ref = refs[3 * WEIGHT_SPLIT:]
    s, j = pl.program_id(0), pl.program_id(1)

    @pl.when(j == 0)
    def _():
        o_ref[...] = jnp.zeros_like(o_ref)

    @pl.when(sub_ref[s] > 0)
    def _():
        for parts, dst in ((w1_parts, w1b_ref), (w3_parts, w3b_ref), (w2_parts, w2b_ref)):
            rows = dst.shape[0] // WEIGHT_SPLIT
            for k, part in enumerate(parts):
                dst[k * rows:(k + 1) * rows, :] = part[...].astype(BF16)

        def sub_tile(r, carry):
            rows = pl.ds(pl.multiple_of(r * ROUTE_TILE, ROUTE_TILE), ROUTE_TILE)
            xr = x_ref[rows, :]
            mid = _silu(_dot(xr, w1b_ref[...])) * _dot(xr, w3b_ref[...])
            o_ref[rows, :] += _dot(mid.astype(BF16), w2b_ref[...])
            return carry

        lax.fori_loop(0, sub_ref[s], sub_tile, 0)


def _experts(xs, super_expert, super_sub, w1, w3, w2):
    n_slots, d = xs.shape
    f = w1.shape[2]
    tf = 256
    nf = f // tf
    n_super = n_slots // EXPERT_ROWS
    ns = WEIGHT_SPLIT

    def f_eff(s, j, sub):
        return jnp.where(sub[s] > 0, j, nf - 1)

    up_specs = [pl.BlockSpec((None, d // ns, tf), lambda s, j, e, sub, k=k: (e[s], k, f_eff(s, j, sub)))
                for k in range(ns)]
    down_specs = [pl.BlockSpec((None, tf // ns, d), lambda s, j, e, sub, k=k: (e[s], f_eff(s, j, sub) * ns + k, 0))
                  for k in range(ns)]
    row_spec = pl.BlockSpec((EXPERT_ROWS, d), lambda s, j, e, sub: (s, 0), pipeline_mode=pl.Buffered(1))
    return pl.pallas_call(
        _expert_kernel,
        grid_spec=pltpu.PrefetchScalarGridSpec(
            num_scalar_prefetch=2,
            grid=(n_super, nf),
            in_specs=[row_spec] + up_specs + up_specs + down_specs,
            out_specs=row_spec,
            scratch_shapes=[pltpu.VMEM((d, tf), BF16), pltpu.VMEM((d, tf), BF16), pltpu.VMEM((tf, d), BF16)]),
        out_shape=jax.ShapeDtypeStruct((n_slots, d), F32),
        compiler_params=_cparams("arbitrary", "arbitrary"),
        name="moe_experts",
    )(super_expert, super_sub, xs, *([w1] * ns), *([w3] * ns), *([w2] * ns))


def _combine_kernel(slot_ref, ys_hbm, h_ref, gate_ref, gf_ref, fg_ref, o_ref, buf_ref, sem):
    i = pl.program_id(0)
    n = pl.num_programs(0)

    def request(tile):
        slot = tile % 2

        def start(r, carry):
            for k in range(2):
                src = slot_ref[2 * (tile * ROUTE_TILE + r) + k]
                _row_copy(ys_hbm, buf_ref.at[slot, k], src, r, sem.at[slot]).start()
            return carry

        lax.fori_loop(0, ROUTE_TILE, start, 0)

    @pl.when(i == 0)
    def _():
        request(0)

    @pl.when(i + 1 < n)
    def _():
        request(jnp.minimum(i + 1, n - 1))

    slot = i % 2

    def wait(r, carry):
        for k in range(2):
            _row_copy(ys_hbm, buf_ref.at[slot, k], 0, r, sem.at[slot]).wait()
        return carry

    lax.fori_loop(0, ROUTE_TILE, wait, 0)
    gates = gate_ref[...]
    y = gates[:, 0:1] * buf_ref[slot, 0] + gates[:, 1:2] * buf_ref[slot, 1]
    x = h_ref[...] + gf_ref[...] * y
    o_ref[...] = x * lax.rsqrt(jnp.mean(x * x, axis=-1, keepdims=True) + EPS) * fg_ref[...]


def _combine(ys, slot, h, gates, mod, final_g, row_fn):
    m, d = h.shape
    return pl.pallas_call(
        _combine_kernel,
        grid_spec=pltpu.PrefetchScalarGridSpec(
            num_scalar_prefetch=1,
            grid=(m // ROUTE_TILE,),
            in_specs=[pl.BlockSpec(memory_space=pl.ANY),
                      pl.BlockSpec((ROUTE_TILE, d), lambda i, *_: (i, 0)),
                      pl.BlockSpec((ROUTE_TILE, LANES), lambda i, *_: (i, 0)),
                      _mod_spec(d, GF, row_fn),
                      pl.BlockSpec((1, d), lambda i, *_: (0, 0))],
            out_specs=pl.BlockSpec((ROUTE_TILE, d), lambda i, *_: (i, 0)),
            scratch_shapes=[pltpu.VMEM((2, 2, ROUTE_TILE, d), F32), pltpu.SemaphoreType.DMA((2,))]),
        out_shape=jax.ShapeDtypeStruct((m, d), F32),
        compiler_params=_cparams("arbitrary"),
        name="moe_combine_final_norm",
    )(slot, ys, h, gates, mod, final_g.reshape(1, d))


def _grid_pos_embed(rows, dim):
    r = jnp.repeat(jnp.arange(rows, dtype=F32), GRID_W)
    col = jnp.tile(jnp.arange(GRID_W, dtype=F32), rows)
    quarter = dim // 4
    freq = jnp.exp(-math.log(10000.0) * jnp.arange(quarter, dtype=F32) / quarter)
    ar = r[:, None] * freq
    ac = col[:, None] * freq
    return jnp.concatenate([jnp.sin(ar), jnp.cos(ar), jnp.sin(ac), jnp.cos(ac)], axis=-1)


def _reorder_w_in(w):
    o_ab = QKV_W
    o_z = o_ab + AB_COLS
    o_sg = o_z + GROUP_W
    o_pool = o_sg + 2 * GROUP_W
    o_cv = o_pool + GROUP_W
    pad = jnp.zeros((w.shape[0], LANES - AB_COLS), w.dtype)
    return jnp.concatenate([w[:, :QKV_W], w[:, o_z:o_sg], w[:, o_sg:o_pool], w[:, o_cv:o_cv + 2 * GROUP_W],
                            w[:, o_pool:o_cv], w[:, o_ab:o_z], pad], axis=1).astype(BF16)


def kernel(x, c, ctx, c_ctx, ada_w, ada_b, norm_mix_g, w_in, dn_conv_w, dn_a_log, dn_dt_bias, dn_norm_g,
           sg_ln_g, sg_ln_b, sg_w, sg_b, pool_w, pool_scale, cv_w, cv_b, cv_ln_g, cv_ln_b, w_out,
           norm_ffn_g, ffn_w1, ffn_w3, ffn_w2, router_w, router_b, moe_w1, moe_w3, moe_w2, final_norm_g):
    batch, seq, d = x.shape
    ctx_len = ctx.shape[1]
    depth = ada_w.shape[0]
    assert depth == 2 and d == 4 * GROUP_W and batch + 1 <= SUBLANES
    assert (2 * batch * seq) % EXPERT_ROWS == 0 and ctx_len % SEQ_TILE == 0 and seq % SEQ_TILE == 0
    ctx_rows, x_rows = batch * ctx_len, batch * seq
    n_ctx_tiles, n_x_tiles = ctx_rows // SEQ_TILE, x_rows // SEQ_TILE
    seg = (n_ctx_tiles, ctx_len // SEQ_TILE, seq // SEQ_TILE)
    row_fn_for = lambda tile: functools.partial(_mod_row, tile=tile, ctx_rows=ctx_rows, seq=seq, batch=batch)
    row_fn = row_fn_for(SEQ_TILE)

    pos = _grid_pos_embed(seq // GRID_W, d)
    h = jnp.concatenate([ctx.reshape(ctx_rows, d), (x + pos[None]).reshape(x_rows, d)], axis=0)
    cc = jnp.zeros((SUBLANES, d), F32).at[:batch].set(c).at[batch].set(c_ctx)
    mod = _ada(cc, ada_w, ada_b).reshape(depth, SUBLANES, 1, 6 * d)

    def mixing(l, h, tile0, n_tiles):
        p = _proj(h, norm_mix_g[l], mod[l], _reorder_w_in(w_in[l]), row_fn)
        local = _dn_local(p, dn_conv_w[l], dn_a_log[l], dn_dt_bias[l], seg)
        o_f, o_b = _dn_scan(local, batch=batch, ctx_len=ctx_len, seq=seq)
        params = (dn_norm_g[l], sg_ln_g[l], sg_ln_b[l], sg_w[l], sg_b[l], pool_w[l], pool_scale[l],
                  cv_w[l], cv_b[l], cv_ln_g[l], cv_ln_b[l])
        return _mix(h, p, o_f, o_b, mod[l], params, w_out[l].astype(BF16),
                    seg=seg, row_fn=row_fn, tile0=tile0, n_tiles=n_tiles)

    h = mixing(0, h, 0, n_ctx_tiles + n_x_tiles)
    h = _ffn(h, norm_ffn_g[0], mod[0], ffn_w1[0].astype(BF16), ffn_w3[0].astype(BF16), ffn_w2[0].astype(BF16),
             row_fn_for, math.gcd(1024, ctx_rows, seq))

    hx = mixing(1, h, n_ctx_tiles, n_x_tiles)
    x_row_fn = lambda i: i * ROUTE_TILE // seq
    hn, top, gates = _router(hx, norm_ffn_g[1], mod[1], router_w[0], router_b[0], x_row_fn, 0)
    n_slots = 2 * x_rows + N_EXPERTS * EXPERT_ROWS
    slot, slot_token, super_expert, super_sub, tile_valid = _routing_tables(top[:, :2], n_slots)
    xs = _gather_rows(hn, slot_token, tile_valid)
    ys = _experts(xs, super_expert, super_sub, moe_w1[0], moe_w3[0], moe_w2[0])
    out = _combine(ys, slot, hx, gates, mod[1], final_norm_g, x_row_fn)
    return out.reshape(batch, seq, d)
```

```python
import functools
import math

import jax
import jax.numpy as jnp
import numpy as np
from jax import lax
from jax.experimental import pallas as pl
from jax.experimental.pallas import tpu as pltpu

F32 = jnp.float32
BF16 = jnp.bfloat16
EPS = 1e-6

GRID_W = 64
N_HEADS = 4
HEAD_DIM = 128
GROUP_W = N_HEADS * HEAD_DIM
DN_CHUNK = 64
SEQ_TILE = 256
CHUNKS_PER_TILE = SEQ_TILE // DN_CHUNK
SG_CHUNK = 128
POOL_WINDOWS = (2, 4, 8, 16)
SHORT_CONV = 5
CONF_CONV = 31
N_EXPERTS = 8
LANES = 128
SUBLANES = 8
VMEM_LIMIT = 56 * 1024 * 1024

QKV_OFF, QKV_W = 0, 3 * GROUP_W
Z_OFF = QKV_OFF + QKV_W
SG_OFF = Z_OFF + GROUP_W
CV_OFF = SG_OFF + 2 * GROUP_W
POOL_OFF = CV_OFF + 2 * GROUP_W
AB_OFF = POOL_OFF + GROUP_W
P_WIDTH = AB_OFF + LANES
AB_COLS = 4 * N_HEADS

SHM, SCM, GM, SHF, SCF, GF = range(6)


def _dot(a, b):
    return jnp.dot(a, b, preferred_element_type=F32)


def _dot_nt(a, b):
    return lax.dot_general(a, b, (((1,), (1,)), ((), ())), preferred_element_type=F32)


def _dot_tn(a, b):
    return lax.dot_general(a, b, (((0,), (0,)), ((), ())), preferred_element_type=F32)


def _sigmoid(x):
    return 1.0 / (1.0 + jnp.exp(-x))


def _silu(x):
    return x * _sigmoid(x)


def _cparams(*sem):
    return pltpu.CompilerParams(dimension_semantics=sem, vmem_limit_bytes=VMEM_LIMIT)


def _mod_row(i, *, tile, ctx_rows, seq, batch):
    start = i * tile
    return jnp.where(start < ctx_rows, batch, (start - ctx_rows) // seq)


def _mod_spec(d, chunk, row_fn):
    return pl.BlockSpec((None, 1, d), lambda i, *_: (row_fn(i), 0, chunk))


def _norm_modulate(x, g, scale, shift):
    y = x * lax.rsqrt(jnp.mean(x * x, axis=-1, keepdims=True) + EPS) * g
    return y * (1.0 + scale) + shift


def _ada_kernel(c_ref, w_ref, b_ref, o_ref):
    s = _silu(c_ref[...]).astype(BF16)
    o_ref[...] = _dot(s, w_ref[...].astype(BF16)) + b_ref[...]


def _ada(cc, ada_w, ada_b):
    nl, d, n = ada_w.shape
    tn = 1024
    return pl.pallas_call(
        _ada_kernel,
        grid=(nl, n // tn),
        in_specs=[pl.BlockSpec((SUBLANES, d), lambda l, j: (0, 0)),
                  pl.BlockSpec((None, d, tn), lambda l, j: (l, 0, j)),
                  pl.BlockSpec((None, 1, tn), lambda l, j: (l, 0, j))],
        out_specs=pl.BlockSpec((None, SUBLANES, tn), lambda l, j: (l, 0, j)),
        out_shape=jax.ShapeDtypeStruct((nl, SUBLANES, n), F32),
        compiler_params=_cparams("parallel", "parallel"),
        name="ada_modulation",
    )(cc, ada_w, ada_b.reshape(nl, 1, n))


def _proj_kernel(h_ref, g_ref, sh_ref, sc_ref, w_ref, o_ref):
    y = _norm_modulate(h_ref[...], g_ref[...], sc_ref[...], sh_ref[...])
    o_ref[...] = _dot(y.astype(BF16), w_ref[...])


def _proj(h, g, mod, w, row_fn):
    m, d = h.shape
    n = w.shape[1]
    tm = SEQ_TILE
    return pl.pallas_call(
        _proj_kernel,
        grid=(m // tm,),
        in_specs=[pl.BlockSpec((tm, d), lambda i: (i, 0)),
                  pl.BlockSpec((1, d), lambda i: (0, 0)),
                  _mod_spec(d, SHM, row_fn),
                  _mod_spec(d, SCM, row_fn),
                  pl.BlockSpec((d, n), lambda i: (0, 0), pipeline_mode=pl.Buffered(1))],
        out_specs=pl.BlockSpec((tm, n), lambda i: (i, 0)),
        out_shape=jax.ShapeDtypeStruct((m, n), F32),
        compiler_params=_cparams("parallel"),
        name="in_proj",
    )(h, g.reshape(1, d), mod, mod, w)


def _segment_pos(i, n_ctx_tiles, ctx_seg_tiles, x_seg_tiles):
    in_ctx = i < n_ctx_tiles
    tpos = jnp.where(in_ctx, i % ctx_seg_tiles, (i - n_ctx_tiles) % x_seg_tiles)
    tlen = jnp.where(in_ctx, ctx_seg_tiles, x_seg_tiles)
    return tpos, tlen


def _fill_ext(ext_ref, prev, cur, nxt, first, last, halo):
    ext_ref[0:halo, :] = jnp.where(first, 0.0, prev)
    ext_ref[halo:halo + SEQ_TILE, :] = cur
    ext_ref[halo + SEQ_TILE:2 * halo + SEQ_TILE, :] = jnp.where(last, 0.0, nxt)


PK_W, PK_QG, PK_KD, PK_QK = 0, GROUP_W, 2 * GROUP_W, 3 * GROUP_W
PK_WIDTH = 4 * GROUP_W


def _dn_local_kernel(qkv_ref, prev_ref, next_ref, ab_ref, cw_ref, alog_ref, dtb_ref,
                     uf_ref, pkf_ref, ub_ref, pkb_ref, gam_ref,
                     ext_ref, t_ref, l_ref, rhs_ref, *, seg):
    tpos, tlen = _segment_pos(pl.program_id(0), *seg)
    halo = SUBLANES
    _fill_ext(ext_ref, prev_ref[...], qkv_ref[...], next_ref[...], tpos == 0, tpos == tlen - 1, halo)

    pad = SHORT_CONV // 2
    acc = cw_ref[0:1, :] * ext_ref[halo - pad:halo - pad + SEQ_TILE, :]
    for j in range(1, SHORT_CONV):
        acc = acc + cw_ref[j:j + 1, :] * ext_ref[halo - pad + j:halo - pad + j + SEQ_TILE, :]
    s = _silu(acc)

    ab = ab_ref[...]
    lane = lax.broadcasted_iota(jnp.int32, (SEQ_TILE, LANES), 1)
    xs = ab + dtb_ref[...]
    softplus = jnp.maximum(xs, 0.0) + jnp.log1p(jnp.exp(-jnp.abs(xs)))
    la = jnp.where(lane < 2 * N_HEADS, -jnp.exp(alog_ref[...]) * softplus, 0.0)
    beta = _sigmoid(ab)

    r = lax.broadcasted_iota(jnp.int32, (SEQ_TILE, SEQ_TILE), 0)
    c = lax.broadcasted_iota(jnp.int32, (SEQ_TILE, SEQ_TILE), 1)
    same = (r // DN_CHUNK) == (c // DN_CHUNK)
    incl = (same & (c <= r), same & (c >= r))
    strict = (same & (c < r), same & (c > r))

    def off_block(s, d):
        hi, lo = (r, c) if d == 0 else (c, r)
        return ((hi // s) % 2 == 1) & ((lo // s) == (hi // s) - 1)

    la_hi = la.astype(BF16)
    rem = la - la_hi.astype(F32)
    la_mid = rem.astype(BF16)
    la_lo = (rem - la_mid.astype(F32)).astype(BF16)

    def chunk_sum(mask):
        m = jnp.where(mask, 1.0, 0.0).astype(BF16)
        return _dot(m, la_hi) + _dot(m, la_mid) + _dot(m, la_lo)

    g_fwd = chunk_sum(incl[0])
    g_bwd = chunk_sum(incl[1])
    g = jnp.where(lane < N_HEADS, g_fwd, g_bwd)
    g_tot = g_fwd + g_bwd - la
    g_t = g.T

    outs = ((uf_ref, pkf_ref), (ub_ref, pkb_ref))
    eye = jnp.where(r == c, 1.0, 0.0)
    for h in range(N_HEADS):
        sl = slice(h * HEAD_DIM, (h + 1) * HEAD_DIM)
        qh = s[:, h * HEAD_DIM:(h + 1) * HEAD_DIM]
        kh = s[:, GROUP_W + h * HEAD_DIM:GROUP_W + (h + 1) * HEAD_DIM]
        vh = s[:, 2 * GROUP_W + h * HEAD_DIM:2 * GROUP_W + (h + 1) * HEAD_DIM]
        qh = qh * lax.rsqrt(jnp.sum(qh * qh, axis=-1, keepdims=True) + EPS) * (HEAD_DIM ** -0.5)
        kh = kh * lax.rsqrt(jnp.sum(kh * kh, axis=-1, keepdims=True) + EPS)
        kh16 = kh.astype(BF16)
        qq = _dot_nt(qh.astype(BF16), kh16)
        kk = _dot_nt(kh16, kh16)
        for d in range(2):
            gi = d * N_HEADS + h
            u_ref, pk_ref = outs[d]
            gcol = g[:, gi:gi + 1]
            grow = g_t[gi:gi + 1, :]
            decay = jnp.where(incl[d], jnp.exp(jnp.where(incl[d], gcol - grow, 0.0)), 0.0)
            bcol = beta[:, 2 * N_HEADS + gi:2 * N_HEADS + gi + 1]
            eg = jnp.exp(gcol)
            kb = kh * bcol
            rhs_ref[gi, :, :HEAD_DIM] = vh * bcol
            rhs_ref[gi, :, HEAD_DIM:] = kb * eg
            lmat = jnp.where(strict[d], bcol * kk * decay, 0.0)
            l_ref[gi] = lmat
            t_ref[gi] = eye - jnp.where(off_block(1, d), lmat, 0.0)
            qkm = jnp.where(incl[d], qq * decay, 0.0)
            qk = qkm[:, 0:DN_CHUNK]
            for cb in range(1, CHUNKS_PER_TILE):
                qk = qk + qkm[:, cb * DN_CHUNK:(cb + 1) * DN_CHUNK]
            pk_ref[:, PK_QK + h * HEAD_DIM:PK_QK + h * HEAD_DIM + DN_CHUNK] = qk.astype(BF16)
            pk_ref[:, PK_QK + h * HEAD_DIM + DN_CHUNK:PK_QK + (h + 1) * HEAD_DIM] = jnp.zeros(
                (SEQ_TILE, HEAD_DIM - DN_CHUNK), BF16)
            gl = g_tot[:, gi:gi + 1]
            pk_ref[:, PK_QG + h * HEAD_DIM:PK_QG + (h + 1) * HEAD_DIM] = (qh * eg).astype(BF16)
            pk_ref[:, PK_KD + h * HEAD_DIM:PK_KD + (h + 1) * HEAD_DIM] = (kh * jnp.exp(gl - gcol)).astype(BF16)
            for cb in range(CHUNKS_PER_TILE):
                gam_ref[cb, :, gi * LANES:(gi + 1) * LANES] = jnp.broadcast_to(
                    jnp.exp(gl[cb * DN_CHUNK:cb * DN_CHUNK + 1, :]), (1, LANES))

    n_chain = 2 * N_HEADS
    s_blk = 2
    while s_blk < DN_CHUNK:
        lt = []
        for gi in range(n_chain):
            l_s = jnp.where(off_block(s_blk, gi // N_HEADS), l_ref[gi], 0.0).astype(BF16)
            lt.append(_dot(l_s, t_ref[gi].astype(BF16)).astype(BF16))
        for gi in range(n_chain):
            t_inv = t_ref[gi]
            t_ref[gi] = t_inv - _dot(t_inv.astype(BF16), lt[gi])
        s_blk *= 2
    for gi in range(n_chain):
        d, h = divmod(gi, N_HEADS)
        u_ref, pk_ref = outs[d]
        rhs = rhs_ref[gi]
        sol = rhs + _dot(jnp.where(strict[d], t_ref[gi], 0.0).astype(BF16), rhs.astype(BF16))
        u_ref[:, h * HEAD_DIM:(h + 1) * HEAD_DIM] = sol[:, :HEAD_DIM]
        pk_ref[:, PK_W + h * HEAD_DIM:PK_W + (h + 1) * HEAD_DIM] = sol[:, HEAD_DIM:].astype(BF16)


def _batch_major_tile(i, n_ctx_tiles, ctx_seg_tiles, x_seg_tiles):
    per_batch = ctx_seg_tiles + x_seg_tiles
    j = i - n_ctx_tiles
    return jnp.where(i < n_ctx_tiles,
                     (i // ctx_seg_tiles) * per_batch + i % ctx_seg_tiles,
                     (j // x_seg_tiles) * per_batch + ctx_seg_tiles + j % x_seg_tiles)


def _dn_local(p, conv_w, a_log, dt_bias, seg):
    rows = p.shape[0]
    nt = rows // SEQ_TILE
    hb = SEQ_TILE // SUBLANES
    last_hb = rows // SUBLANES - 1
    cw = jnp.zeros((SUBLANES, QKV_W), F32).at[:SHORT_CONV].set(conv_w)
    pad_row = lambda v: jnp.zeros((1, LANES), F32).at[0, :2 * N_HEADS].set(v.reshape(-1))
    u_out = jax.ShapeDtypeStruct((rows, GROUP_W), F32)
    pk_out = jax.ShapeDtypeStruct((rows, PK_WIDTH), BF16)
    gam_out = jax.ShapeDtypeStruct((rows // DN_CHUNK, 1, 2 * N_HEADS * LANES), F32)
    dst = lambda i: _batch_major_tile(i, *seg)
    u_spec = pl.BlockSpec((SEQ_TILE, GROUP_W), lambda i: (dst(i), 0))
    pk_spec = pl.BlockSpec((SEQ_TILE, PK_WIDTH), lambda i: (dst(i), 0))
    full = lambda shape: pl.BlockSpec(shape, lambda i: (0,) * len(shape))
    n_chain = 2 * N_HEADS
    return pl.pallas_call(
        functools.partial(_dn_local_kernel, seg=seg),
        grid=(nt,),
        in_specs=[pl.BlockSpec((SEQ_TILE, QKV_W), lambda i: (i, QKV_OFF // QKV_W)),
                  pl.BlockSpec((SUBLANES, QKV_W), lambda i: (jnp.maximum(i * hb - 1, 0), 0)),
                  pl.BlockSpec((SUBLANES, QKV_W), lambda i: (jnp.minimum((i + 1) * hb, last_hb), 0)),
                  pl.BlockSpec((SEQ_TILE, LANES), lambda i: (i, AB_OFF // LANES)),
                  full((SUBLANES, QKV_W)), full((1, LANES)), full((1, LANES))],
        out_specs=[u_spec, pk_spec, u_spec, pk_spec,
                   pl.BlockSpec((CHUNKS_PER_TILE, 1, 2 * N_HEADS * LANES), lambda i: (dst(i), 0, 0))],
        out_shape=[u_out, pk_out, u_out, pk_out, gam_out],
        scratch_shapes=[pltpu.VMEM((SEQ_TILE + 2 * SUBLANES, QKV_W), F32),
                        pltpu.VMEM((n_chain, SEQ_TILE, SEQ_TILE), F32),
                        pltpu.VMEM((n_chain, SEQ_TILE, SEQ_TILE), F32),
                        pltpu.VMEM((n_chain, SEQ_TILE, 2 * HEAD_DIM), F32)],
        compiler_params=_cparams("parallel"),
        name="dn_local",
    )(p, p, p, p, cw, pad_row(a_log), pad_row(dt_bias))


def _dn_scan_kernel(uf, pkf, gamf, ub, pkb, gamb, of_ref, ob_ref, s_ref, *, batch):
    @pl.when(pl.program_id(0) == 0)
    def _():
        s_ref[...] = jnp.zeros_like(s_ref)

    dirs = ((uf, pkf, gamf, of_ref), (ub, pkb, gamb, ob_ref))
    chains = [(b, d, h) for b in range(batch) for d in range(2) for h in range(N_HEADS)]
    head = lambda off, h: slice(off + h * HEAD_DIM, off + (h + 1) * HEAD_DIM)

    ws = []
    for ci, (b, d, h) in enumerate(chains):
        pk = dirs[d][1]
        lhs = jnp.concatenate([pk[b, :, head(PK_W, h)], pk[b, :, head(PK_QG, h)]], axis=0)
        ws.append(_dot(lhs, s_ref[ci].astype(BF16)))
    v16 = []
    for ci, (b, d, h) in enumerate(chains):
        v16.append((dirs[d][0][b, :, head(0, h)] - ws[ci][:DN_CHUNK]).astype(BF16))
    for ci, (b, d, h) in enumerate(chains):
        pk, o_ref = dirs[d][1], dirs[d][3]
        qk = pk[b, :, PK_QK + h * HEAD_DIM:PK_QK + h * HEAD_DIM + DN_CHUNK]
        o_ref[b, :, head(0, h)] = ws[ci][DN_CHUNK:] + _dot(qk, v16[ci])
    for ci, (b, d, h) in enumerate(chains):
        pk, gam = dirs[d][1], dirs[d][2]
        gi = d * N_HEADS + h
        s_ref[ci] = s_ref[ci] * gam[b, :, gi * LANES:(gi + 1) * LANES] + _dot_tn(pk[b, :, head(PK_KD, h)], v16[ci])


def _dn_scan(local, *, batch, ctx_len, seq):
    uf, pkf, ub, pkb, gam = local
    cc, xc = ctx_len // DN_CHUNK, seq // DN_CHUNK
    nc = cc + xc
    bwd = lambda s: jnp.where(s < cc, cc - 1 - s, cc + (xc - 1 - (s - cc)))
    by_chunk = lambda a: a.reshape(batch, nc, *a.shape[1:]) if a.ndim == 3 else a.reshape(batch, nc, DN_CHUNK, a.shape[1])

    def specs(idx):
        return [pl.BlockSpec((batch, None, DN_CHUNK, GROUP_W), lambda s: (0, idx(s), 0, 0)),
                pl.BlockSpec((batch, None, DN_CHUNK, PK_WIDTH), lambda s: (0, idx(s), 0, 0)),
                pl.BlockSpec((batch, None, 1, 2 * N_HEADS * LANES), lambda s: (0, idx(s), 0, 0))]

    out = jax.ShapeDtypeStruct((batch, nc, DN_CHUNK, GROUP_W), F32)
    fwd = lambda s: s
    o_f, o_b = pl.pallas_call(
        functools.partial(_dn_scan_kernel, batch=batch),
        grid=(nc,),
        in_specs=specs(fwd) + specs(bwd),
        out_specs=[pl.BlockSpec((batch, None, DN_CHUNK, GROUP_W), lambda s: (0, s, 0, 0)),
                   pl.BlockSpec((batch, None, DN_CHUNK, GROUP_W), lambda s: (0, bwd(s), 0, 0))],
        out_shape=[out, out],
        scratch_shapes=[pltpu.VMEM((batch * 2 * N_HEADS, HEAD_DIM, HEAD_DIM), F32)],
        compiler_params=_cparams("arbitrary"),
        name="dn_scan",
    )(by_chunk(uf), by_chunk(pkf), by_chunk(gam), by_chunk(ub), by_chunk(pkb), by_chunk(gam))
    return o_f.reshape(-1, GROUP_W), o_b.reshape(-1, GROUP_W)


def _layernorm(x, g, b):
    mu = jnp.mean(x, axis=-1, keepdims=True)
    xc = x - mu
    var = jnp.mean(xc * xc, axis=-1, keepdims=True)
    return xc * lax.rsqrt(var + EPS) * g + b


def _mix_kernel(h_ref, of_ref, ob_ref, z_ref, sg_ref, cv_ref, cvp_ref, cvn_ref, pool_ref, poolp_ref, pooln_ref,
                gm_ref, dng_ref, sglg_ref, sglb_ref, sgw_ref, sgbt_ref, pw_ref, ps_ref,
                cvw_ref, cvb_ref, cvlg_ref, cvlb_ref, wout_ref,
                o_ref, y_ref, pext_ref, cext_ref, cacc_ref, *, seg, tile0):
    tpos, tlen = _segment_pos(pl.program_id(0) + tile0, *seg)
    first, last = tpos == 0, tpos == tlen - 1

    o = of_ref[...] + ob_ref[...]
    z = z_ref[...]
    for h in range(N_HEADS):
        sl = slice(h * HEAD_DIM, (h + 1) * HEAD_DIM)
        oh = o[:, sl]
        yh = oh * lax.rsqrt(jnp.mean(oh * oh, axis=-1, keepdims=True) + EPS) * dng_ref[...] * _silu(z[:, sl])
        y_ref[:, sl] = yh.astype(BF16)

    psg = sg_ref[...]
    psg = 0.5 * psg * (1.0 + jnp.tanh(math.sqrt(2.0 / math.pi) * (psg + 0.044715 * (psg * psg * psg))))
    u = psg[:, :GROUP_W]
    v = _layernorm(psg[:, GROUP_W:], sglg_ref[...], sglb_ref[...]).astype(BF16)
    for n in range(SEQ_TILE // SG_CHUNK):
        rs = slice(n * SG_CHUNK, (n + 1) * SG_CHUNK)
        for h in range(N_HEADS):
            sl = slice(h * HEAD_DIM, (h + 1) * HEAD_DIM)
            sv = _dot(sgw_ref[h], v[rs, sl]) + sgbt_ref[:, h:h + 1]
            y_ref[rs, GROUP_W + h * HEAD_DIM:GROUP_W + (h + 1) * HEAD_DIM] = (u[rs, sl] * sv).astype(BF16)

    halo = SUBLANES
    _fill_ext(pext_ref, poolp_ref[...], pool_ref[...], pooln_ref[...], first, last, halo)
    t = tpos * SEQ_TILE + lax.broadcasted_iota(jnp.int32, (SEQ_TILE, 1), 0)
    seg_len = tlen * SEQ_TILE
    for gi, win in enumerate(POOL_WINDOWS):
        sl = slice(gi * LANES, (gi + 1) * LANES)
        tot = pext_ref[halo - win // 2:halo - win // 2 + SEQ_TILE, sl]
        for m in range(1 - win // 2, win // 2):
            tot = tot + pext_ref[halo + m:halo + m + SEQ_TILE, sl]
        cnt = jnp.clip(t + win // 2, 0, seg_len) - jnp.clip(t - win // 2, 0, seg_len)
        yg = tot / cnt.astype(F32) - pool_ref[:, sl]
        yg = _dot(yg.astype(BF16), pw_ref[gi]) * ps_ref[:, sl]
        y_ref[:, 2 * GROUP_W + gi * LANES:2 * GROUP_W + (gi + 1) * LANES] = yg.astype(BF16)

    halo = 2 * SUBLANES
    glu = lambda p: p[:, :GROUP_W] * _sigmoid(p[:, GROUP_W:])
    _fill_ext(cext_ref, glu(cvp_ref[...]), glu(cv_ref[...]), glu(cvn_ref[...]), first, last, halo)
    pad = CONF_CONV // 2
    span = SEQ_TILE + SUBLANES
    acc = None
    for rho in range(SUBLANES):
        part = None
        for j in range(CONF_CONV):
            off = halo - pad + j
            if off % SUBLANES == rho:
                term = cvw_ref[j:j + 1, :] * cext_ref[off - rho:off - rho + span, :]
                part = term if part is None else part + term
        if part is not None:
            cacc_ref[...] = part
            shifted = cacc_ref[rho:rho + SEQ_TILE, :]
            acc = shifted if acc is None else acc + shifted
    yc = _silu(_layernorm(acc + cvb_ref[...], cvlg_ref[...], cvlb_ref[...]))
    y_ref[:, 3 * GROUP_W:] = yc.astype(BF16)

    o_ref[...] = h_ref[...] + gm_ref[...] * _dot(y_ref[...], wout_ref[...])


def _mix(h, p, o_f, o_b, mod, params, wout, *, seg, row_fn, tile0, n_tiles):
    d = h.shape[1]
    (dng, sglg, sglb, sgw, sgb, pw, ps, cvw, cvb, cvlg, cvlb) = params
    rows = p.shape[0]
    row = lambda v: v.reshape(1, -1)
    cvw_p = jnp.zeros((4 * SUBLANES, GROUP_W), F32).at[:CONF_CONV].set(cvw)
    sgbt = jnp.zeros((SG_CHUNK, LANES), F32).at[:, :N_HEADS].set(sgb.T)

    def halo_specs(width, col, halo):
        hb = SEQ_TILE // halo
        last_hb = rows // halo - 1
        return [pl.BlockSpec((SEQ_TILE, width), lambda i: (i + tile0, col)),
                pl.BlockSpec((halo, width), lambda i: (jnp.maximum((i + tile0) * hb - 1, 0), col)),
                pl.BlockSpec((halo, width), lambda i: (jnp.minimum((i + tile0 + 1) * hb, last_hb), col))]

    full = lambda a: pl.BlockSpec(a.shape, lambda i: (0,) * a.ndim)
    consts = [row(dng), row(sglg), row(sglb), sgw.astype(BF16), sgbt, pw.astype(BF16), row(ps),
              cvw_p, row(cvb), row(cvlg), row(cvlb)]
    return pl.pallas_call(
        functools.partial(_mix_kernel, seg=seg, tile0=tile0),
        grid=(n_tiles,),
        in_specs=[pl.BlockSpec((SEQ_TILE, d), lambda i: (i + tile0, 0)),
                  pl.BlockSpec((SEQ_TILE, GROUP_W), lambda i: (_batch_major_tile(i + tile0, *seg), 0)),
                  pl.BlockSpec((SEQ_TILE, GROUP_W), lambda i: (_batch_major_tile(i + tile0, *seg), 0)),
                  pl.BlockSpec((SEQ_TILE, GROUP_W), lambda i: (i + tile0, Z_OFF // GROUP_W)),
                  pl.BlockSpec((SEQ_TILE, 2 * GROUP_W), lambda i: (i + tile0, SG_OFF // (2 * GROUP_W)))]
                 + halo_specs(2 * GROUP_W, CV_OFF // (2 * GROUP_W), 2 * SUBLANES)
                 + halo_specs(GROUP_W, POOL_OFF // GROUP_W, SUBLANES)
                 + [_mod_spec(d, GM, lambda i: row_fn(i + tile0))]
                 + [full(a) for a in consts]
                 + [pl.BlockSpec(wout.shape, lambda i: (0, 0), pipeline_mode=pl.Buffered(1))],
        out_specs=pl.BlockSpec((SEQ_TILE, d), lambda i: (i, 0)),
        out_shape=jax.ShapeDtypeStruct((n_tiles * SEQ_TILE, d), F32),
        scratch_shapes=[pltpu.VMEM((SEQ_TILE, d), BF16),
                        pltpu.VMEM((SEQ_TILE + 2 * SUBLANES, GROUP_W), F32),
                        pltpu.VMEM((SEQ_TILE + 4 * SUBLANES, GROUP_W), F32),
                        pltpu.VMEM((SEQ_TILE + SUBLANES, GROUP_W), F32)],
        compiler_params=_cparams("parallel"),
        name="mixers_out_proj",
    )(h, o_f, o_b, p, p, p, p, p, p, p, p, mod, *consts, wout)


def _ffn_kernel(h_ref, g_ref, sh_ref, sc_ref, gate_ref, w1_ref, w3_ref, w2_ref, o_ref, xn_ref, acc_ref):
    j = pl.program_id(1)

    @pl.when(j == 0)
    def _():
        xn_ref[...] = _norm_modulate(h_ref[...], g_ref[...], sc_ref[...], sh_ref[...]).astype(BF16)
        acc_ref[...] = jnp.zeros_like(acc_ref)

    xn = xn_ref[...]
    mid = _silu(_dot(xn, w1_ref[...])) * _dot(xn, w3_ref[...])
    acc_ref[...] += _dot(mid.astype(BF16), w2_ref[...])

    @pl.when(j == pl.num_programs(1) - 1)
    def _():
        o_ref[...] = h_ref[...] + gate_ref[...] * acc_ref[...]


def _ffn(h, g, mod, w1, w3, w2, row_fn_for, tm):
    m, d = h.shape
    f = w1.shape[1]
    tf = 512
    row_fn = row_fn_for(tm)
    return pl.pallas_call(
        _ffn_kernel,
        grid=(m // tm, f // tf),
        in_specs=[pl.BlockSpec((tm, d), lambda i, j: (i, 0)),
                  pl.BlockSpec((1, d), lambda i, j: (0, 0)),
                  _mod_spec(d, SHF, row_fn), _mod_spec(d, SCF, row_fn), _mod_spec(d, GF, row_fn),
                  pl.BlockSpec((d, tf), lambda i, j: (0, j)),
                  pl.BlockSpec((d, tf), lambda i, j: (0, j)),
                  pl.BlockSpec((tf, d), lambda i, j: (j, 0))],
        out_specs=pl.BlockSpec((tm, d), lambda i, j: (i, 0)),
        out_shape=jax.ShapeDtypeStruct((m, d), F32),
        scratch_shapes=[pltpu.VMEM((tm, d), BF16), pltpu.VMEM((tm, d), F32)],
        compiler_params=_cparams("parallel", "arbitrary"),
        name="dense_swiglu",
    )(h, g.reshape(1, d), mod, mod, mod, w1, w3, w2)


ROUTE_TILE = 256
EXPERT_ROWS = 2048


def _router_kernel(h_ref, g_ref, sh_ref, sc_ref, rw_ref, rb_ref, hn_ref, idx_ref, gate_ref, total_ref, count_ref):
    hn = _norm_modulate(h_ref[...], g_ref[...], sc_ref[...], sh_ref[...])
    hn_ref[...] = hn
    logits = jnp.dot(hn, rw_ref[...], preferred_element_type=F32, precision=lax.Precision.HIGHEST) + rb_ref[...]
    lane = lax.broadcasted_iota(jnp.int32, logits.shape, 1)
    lane_f = lane.astype(F32)
    neg = -jnp.inf
    lg = jnp.where(lane < N_EXPERTS, logits, neg)
    m1 = jnp.max(lg, axis=-1, keepdims=True)
    i1 = jnp.min(jnp.where(lg == m1, lane_f, float(LANES)), axis=-1, keepdims=True)
    lg2 = jnp.where(lane_f == i1, neg, lg)
    m2 = jnp.max(lg2, axis=-1, keepdims=True)
    i2 = jnp.min(jnp.where(lg2 == m2, lane_f, float(LANES)), axis=-1, keepdims=True)
    e2 = jnp.exp(m2 - m1)
    g1 = 1.0 / (1.0 + e2)
    g2 = e2 / (1.0 + e2)
    gate_ref[...] = jnp.where(lane == 0, g1, jnp.where(lane == 1, g2, 0.0))

    @pl.when(pl.program_id(0) == 0)
    def _():
        count_ref[...] = jnp.zeros_like(count_ref)

    pick1 = jnp.where(lane_f == i1, 1.0, 0.0)
    pick2 = jnp.where(lane_f == i2, 1.0, 0.0)
    picked = pick1 + pick2
    r = lax.broadcasted_iota(jnp.int32, (ROUTE_TILE, ROUTE_TILE), 0)
    c = lax.broadcasted_iota(jnp.int32, (ROUTE_TILE, ROUTE_TILE), 1)
    earlier = jnp.where(c < r, 1.0, 0.0).astype(BF16)
    before = count_ref[0:1, :] + _dot(earlier, picked.astype(BF16))
    rank1 = jnp.sum(pick1 * before, axis=-1, keepdims=True)
    rank2 = jnp.sum(pick2 * before, axis=-1, keepdims=True)
    count_ref[...] = count_ref[...] + jnp.sum(picked, axis=0, keepdims=True)
    total_ref[...] = count_ref[...]
    idx_ref[...] = jnp.where(lane == 0, i1, jnp.where(lane == 1, i2, jnp.where(
        lane == 2, rank1, jnp.where(lane == 3, rank2, 0.0)))).astype(jnp.int32)


def _router(h, g, mod, rw, rb, row_fn, tile0):
    d = h.shape[1]
    tm = ROUTE_TILE
    n_tiles = h.shape[0] // tm - tile0
    m = n_tiles * tm
    rw_p = jnp.zeros((d, LANES), F32).at[:, :N_EXPERTS].set(rw)
    rb_p = jnp.zeros((1, LANES), F32).at[0, :N_EXPERTS].set(rb)
    shifted = lambda i: row_fn(i + tile0)
    return pl.pallas_call(
        _router_kernel,
        grid=(n_tiles,),
        in_specs=[pl.BlockSpec((tm, d), lambda i: (i + tile0, 0)),
                  pl.BlockSpec((1, d), lambda i: (0, 0)),
                  _mod_spec(d, SHF, shifted), _mod_spec(d, SCF, shifted),
                  pl.BlockSpec((d, LANES), lambda i: (0, 0)),
                  pl.BlockSpec((1, LANES), lambda i: (0, 0))],
        out_specs=[pl.BlockSpec((tm, d), lambda i: (i, 0)),
                   pl.BlockSpec((tm, LANES), lambda i: (i, 0)),
                   pl.BlockSpec((tm, LANES), lambda i: (i, 0)),
                   pl.BlockSpec((SUBLANES, LANES), lambda i: (0, 0))],
        out_shape=[jax.ShapeDtypeStruct((m, d), F32),
                   jax.ShapeDtypeStruct((m, LANES), jnp.int32),
                   jax.ShapeDtypeStruct((m, LANES), F32),
                   jax.ShapeDtypeStruct((SUBLANES, LANES), F32)],
        scratch_shapes=[pltpu.VMEM((SUBLANES, LANES), F32)],
        compiler_params=_cparams("arbitrary"),
        name="moe_router",
    )(h, g.reshape(1, d), mod, mod, rw_p, rb_p)


def _routing_tables(route, totals, n_slots):
    n_tok = route.shape[0]
    e_flat = route[:, 0:2].reshape(-1)
    rank = route[:, 2:4].reshape(-1)
    counts = totals[0, :N_EXPERTS].astype(jnp.int32)
    padded = (counts + EXPERT_ROWS - 1) // EXPERT_ROWS * EXPERT_ROWS
    ends = jnp.cumsum(padded)
    starts = ends - padded
    onehot = (e_flat[:, None] == jnp.arange(N_EXPERTS)[None, :]).astype(jnp.int32)
    slot = jnp.sum(onehot * starts[None, :], axis=1) + rank
    slot_token = jnp.zeros((n_slots,), jnp.int32).at[slot].set(jnp.arange(2 * n_tok, dtype=jnp.int32) // 2)
    n_super = n_slots // EXPERT_ROWS
    super_start = jnp.arange(n_super, dtype=jnp.int32) * EXPERT_ROWS
    super_expert = jnp.minimum(jnp.sum(super_start[:, None] >= ends[None, :], axis=1), N_EXPERTS - 1).astype(jnp.int32)
    valid_rows = jnp.clip(counts[super_expert] - (super_start - starts[super_expert]), 0, EXPERT_ROWS)
    valid_rows = jnp.where(super_start < ends[-1], valid_rows, 0)
    super_sub = ((valid_rows + ROUTE_TILE - 1) // ROUTE_TILE).astype(jnp.int32)
    tile_valid = (jnp.arange(n_slots // ROUTE_TILE, dtype=jnp.int32) % (EXPERT_ROWS // ROUTE_TILE)
                  < jnp.repeat(super_sub, EXPERT_ROWS // ROUTE_TILE)).astype(jnp.int32)
    return slot.astype(jnp.int32), slot_token, super_expert, super_sub, tile_valid


def _row_copy(src_hbm, dst_vmem, src_row, dst_row, sem):
    return pltpu.make_async_copy(src_hbm.at[pl.ds(src_row, 1), :], dst_vmem.at[pl.ds(dst_row, 1), :], sem)


def _gather_kernel(tok_ref, valid_ref, hn_hbm, o_ref, buf_ref, sem):
    i = pl.program_id(0)
    n = pl.num_programs(0)

    def request(tile):
        slot = tile % 2

        def start(r, carry):
            _row_copy(hn_hbm, buf_ref.at[slot], tok_ref[tile * ROUTE_TILE + r], r, sem.at[slot]).start()
            return carry

        lax.fori_loop(0, ROUTE_TILE, start, 0, unroll=8)

    @pl.when((i == 0) & (valid_ref[0] != 0))
    def _():
        request(0)

    nxt = jnp.minimum(i + 1, n - 1)

    @pl.when((i + 1 < n) & (valid_ref[nxt] != 0))
    def _():
        request(nxt)

    @pl.when(valid_ref[i] == 0)
    def _():
        o_ref[...] = jnp.zeros_like(o_ref)

    @pl.when(valid_ref[i] != 0)
    def _():
        slot = i % 2

        pltpu.make_async_copy(hn_hbm.at[pl.ds(0, ROUTE_TILE), :], buf_ref.at[slot], sem.at[slot]).wait()
        o_ref[...] = buf_ref[slot].astype(BF16)


def _gather_rows(hn, slot_token, tile_valid):
    d = hn.shape[1]
    n_slots = slot_token.shape[0]
    return pl.pallas_call(
        _gather_kernel,
        grid_spec=pltpu.PrefetchScalarGridSpec(
            num_scalar_prefetch=2,
            grid=(n_slots // ROUTE_TILE,),
            in_specs=[pl.BlockSpec(memory_space=pl.ANY)],
            out_specs=pl.BlockSpec((ROUTE_TILE, d), lambda i, *_: (i, 0)),
            scratch_shapes=[pltpu.VMEM((2, ROUTE_TILE, d), F32), pltpu.SemaphoreType.DMA((2,))]),
        out_shape=jax.ShapeDtypeStruct((n_slots, d), BF16),
        compiler_params=_cparams("arbitrary"),
        name="moe_gather",
    )(slot_token, tile_valid, hn)


WEIGHT_SPLIT = 4


def _expert_kernel(exp_ref, sub_ref, x_ref, *refs):
    w1_parts = refs[:WEIGHT_SPLIT]
    w3_parts = refs[WEIGHT_SPLIT:2 * WEIGHT_SPLIT]
    w2_parts = refs[2 * WEIGHT_SPLIT:3 * WEIGHT_SPLIT]
    o_ref, w1b_ref, w3b_ref, w2b_ref = refs[3 * WEIGHT_SPLIT:]
    s, j = pl.program_id(0), pl.program_id(1)

    @pl.when(j == 0)
    def _():
        o_ref[...] = jnp.zeros_like(o_ref)

    @pl.when(sub_ref[s] > 0)
    def _():
        for parts, dst in ((w1_parts, w1b_ref), (w3_parts, w3b_ref), (w2_parts, w2b_ref)):
            rows = dst.shape[0] // WEIGHT_SPLIT
            for k, part in enumerate(parts):
                dst[k * rows:(k + 1) * rows, :] = part[...].astype(BF16)

        def swiglu_rows(start, size):
            rows = pl.ds(pl.multiple_of(start, ROUTE_TILE), size)
            xr = x_ref[rows, :]
            mid = _silu(_dot(xr, w1b_ref[...])) * _dot(xr, w3b_ref[...])
            o_ref[rows, :] += _dot(mid.astype(BF16), w2b_ref[...])

        n_sub = sub_ref[s]
        n_pair = n_sub // 2

        def pair(r, carry):
            swiglu_rows(r * (2 * ROUTE_TILE), 2 * ROUTE_TILE)
            return carry

        lax.fori_loop(0, n_pair, pair, 0)

        @pl.when(n_sub % 2 == 1)
        def _():
            swiglu_rows(n_pair * (2 * ROUTE_TILE), ROUTE_TILE)


def _experts(xs, super_expert, super_sub, w1, w3, w2):
    n_slots, d = xs.shape
    f = w1.shape[2]
    tf = 256
    nf = f // tf
    n_super = n_slots // EXPERT_ROWS
    ns = WEIGHT_SPLIT

    def f_eff(s, j, sub):
        return jnp.where(sub[s] > 0, j, nf - 1)

    up_specs = [pl.BlockSpec((None, d // ns, tf), lambda s, j, e, sub, k=k: (e[s], k, f_eff(s, j, sub)))
                for k in range(ns)]
    down_specs = [pl.BlockSpec((None, tf // ns, d), lambda s, j, e, sub, k=k: (e[s], f_eff(s, j, sub) * ns + k, 0))
                  for k in range(ns)]
    row_spec = pl.BlockSpec((EXPERT_ROWS, d), lambda s, j, e, sub: (s, 0), pipeline_mode=pl.Buffered(1))
    return pl.pallas_call(
        _expert_kernel,
        grid_spec=pltpu.PrefetchScalarGridSpec(
            num_scalar_prefetch=2,
            grid=(n_super, nf),
            in_specs=[row_spec] + up_specs + up_specs + down_specs,
            out_specs=row_spec,
            scratch_shapes=[pltpu.VMEM((d, tf), BF16), pltpu.VMEM((d, tf), BF16), pltpu.VMEM((tf, d), BF16)]),
        out_shape=jax.ShapeDtypeStruct((n_slots, d), F32),
        compiler_params=_cparams("arbitrary", "arbitrary"),
        name="moe_experts",
    )(super_expert, super_sub, xs, *([w1] * ns), *([w3] * ns), *([w2] * ns))


def _combine_kernel(slot_ref, ys_hbm, h_ref, gate_ref, gf_ref, fg_ref, o_ref, buf_ref, sem):
    i = pl.program_id(0)
    n = pl.num_programs(0)

    def request(tile):
        slot = tile % 2

        def start(r, carry):
            for k in range(2):
                src = slot_ref[2 * (tile * ROUTE_TILE + r) + k]
                _row_copy(ys_hbm, buf_ref.at[slot, k], src, r, sem.at[slot]).start()
            return carry

        lax.fori_loop(0, ROUTE_TILE, start, 0, unroll=8)

    @pl.when(i == 0)
    def _():
        request(0)

    @pl.when(i + 1 < n)
    def _():
        request(jnp.minimum(i + 1, n - 1))

    slot = i % 2

    for k in range(2):
        pltpu.make_async_copy(ys_hbm.at[pl.ds(0, ROUTE_TILE), :], buf_ref.at[slot, k], sem.at[slot]).wait()
    gates = gate_ref[...]
    y = gates[:, 0:1] * buf_ref[slot, 0] + gates[:, 1:2] * buf_ref[slot, 1]
    x = h_ref[...] + gf_ref[...] * y
    o_ref[...] = x * lax.rsqrt(jnp.mean(x * x, axis=-1, keepdims=True) + EPS) * fg_ref[...]


def _combine(ys, slot, h, gates, mod, final_g, row_fn):
    m, d = h.shape
    return pl.pallas_call(
        _combine_kernel,
        grid_spec=pltpu.PrefetchScalarGridSpec(
            num_scalar_prefetch=1,
            grid=(m // ROUTE_TILE,),
            in_specs=[pl.BlockSpec(memory_space=pl.ANY),
                      pl.BlockSpec((ROUTE_TILE, d), lambda i, *_: (i, 0)),
                      pl.BlockSpec((ROUTE_TILE, LANES), lambda i, *_: (i, 0)),
                      _mod_spec(d, GF, row_fn),
                      pl.BlockSpec((1, d), lambda i, *_: (0, 0))],
            out_specs=pl.BlockSpec((ROUTE_TILE, d), lambda i, *_: (i, 0)),
            scratch_shapes=[pltpu.VMEM((2, 2, ROUTE_TILE, d), F32), pltpu.SemaphoreType.DMA((2,))]),
        out_shape=jax.ShapeDtypeStruct((m, d), F32),
        compiler_params=_cparams("arbitrary"),
        name="moe_combine_final_norm",
    )(slot, ys, h, gates, mod, final_g.reshape(1, d))


def _grid_pos_embed(rows, dim):
    r = np.repeat(np.arange(rows, dtype=np.float32), GRID_W)
    col = np.tile(np.arange(GRID_W, dtype=np.float32), rows)
    quarter = dim // 4
    freq = np.exp(np.float32(-math.log(10000.0)) * np.arange(quarter, dtype=np.float32) / np.float32(quarter))
    ar = r[:, None] * freq
    ac = col[:, None] * freq
    return np.concatenate([np.sin(ar), np.cos(ar), np.sin(ac), np.cos(ac)], axis=-1).astype(np.float32)


def _reorder_w_in(w):
    o_ab = QKV_W
    o_z = o_ab + AB_COLS
    o_sg = o_z + GROUP_W
    o_pool = o_sg + 2 * GROUP_W
    o_cv = o_pool + GROUP_W
    pad = jnp.zeros((w.shape[0], LANES - AB_COLS), w.dtype)
    return jnp.concatenate([w[:, :QKV_W], w[:, o_z:o_sg], w[:, o_sg:o_pool], w[:, o_cv:o_cv + 2 * GROUP_W],
                            w[:, o_pool:o_cv], w[:, o_ab:o_z], pad], axis=1).astype(BF16)


def kernel(x, c, ctx, c_ctx, ada_w, ada_b, norm_mix_g, w_in, dn_conv_w, dn_a_log, dn_dt_bias, dn_norm_g,
           sg_ln_g, sg_ln_b, sg_w, sg_b, pool_w, pool_scale, cv_w, cv_b, cv_ln_g, cv_ln_b, w_out,
           norm_ffn_g, ffn_w1, ffn_w3, ffn_w2, router_w, router_b, moe_w1, moe_w3, moe_w2, final_norm_g):
    batch, seq, d = x.shape
    ctx_len = ctx.shape[1]
    depth = ada_w.shape[0]
    assert depth == 2 and d == 4 * GROUP_W and batch + 1 <= SUBLANES
    assert (2 * batch * seq) % EXPERT_ROWS == 0 and ctx_len % SEQ_TILE == 0 and seq % SEQ_TILE == 0
    ctx_rows, x_rows = batch * ctx_len, batch * seq
    n_ctx_tiles, n_x_tiles = ctx_rows // SEQ_TILE, x_rows // SEQ_TILE
    seg = (n_ctx_tiles, ctx_len // SEQ_TILE, seq // SEQ_TILE)
    row_fn_for = lambda tile: functools.partial(_mod_row, tile=tile, ctx_rows=ctx_rows, seq=seq, batch=batch)
    row_fn = row_fn_for(SEQ_TILE)

    pos = _grid_pos_embed(seq // GRID_W, d)
    h = jnp.concatenate([ctx.reshape(ctx_rows, d), (x + pos[None]).reshape(x_rows, d)], axis=0)
    cc = jnp.zeros((SUBLANES, d), F32).at[:batch].set(c).at[batch].set(c_ctx)
    mod = _ada(cc, ada_w, ada_b).reshape(depth, SUBLANES, 1, 6 * d)

    def mixing(l, h, tile0, n_tiles):
        p = _proj(h, norm_mix_g[l], mod[l], _reorder_w_in(w_in[l]), row_fn)
        local = _dn_local(p, dn_conv_w[l], dn_a_log[l], dn_dt_bias[l], seg)
        o_f, o_b = _dn_scan(local, batch=batch, ctx_len=ctx_len, seq=seq)
        params = (dn_norm_g[l], sg_ln_g[l], sg_ln_b[l], sg_w[l], sg_b[l], pool_w[l], pool_scale[l],
                  cv_w[l], cv_b[l], cv_ln_g[l], cv_ln_b[l])
        return _mix(h, p, o_f, o_b, mod[l], params, w_out[l].astype(BF16),
                    seg=seg, row_fn=row_fn, tile0=tile0, n_tiles=n_tiles)

    h = mixing(0, h, 0, n_ctx_tiles + n_x_tiles)
    h = _ffn(h, norm_ffn_g[0], mod[0], ffn_w1[0].astype(BF16), ffn_w3[0].astype(BF16), ffn_w2[0].astype(BF16),
             row_fn_for, math.gcd(512, ctx_rows, seq))

    hx = mixing(1, h, n_ctx_tiles, n_x_tiles)
    x_row_fn = lambda i: i * ROUTE_TILE // seq
    hn, route, gates, totals = _router(hx, norm_ffn_g[1], mod[1], router_w[0], router_b[0], x_row_fn, 0)
    n_slots = 2 * x_rows + N_EXPERTS * EXPERT_ROWS
    slot, slot_token, super_expert, super_sub, tile_valid = _routing_tables(route, totals, n_slots)
    xs = _gather_rows(hn, slot_token, tile_valid)
    ys = _experts(xs, super_expert, super_sub, moe_w1[0], moe_w3[0], moe_w2[0])
    out = _combine(ys, slot, hx, gates, mod[1], final_norm_g, x_row_fn)
    return out.reshape(batch, seq, d)
```

```python
import functools
import math

import jax
import jax.numpy as jnp
import numpy as np
from jax import lax
from jax.experimental import pallas as pl
from jax.experimental.pallas import tpu as pltpu

F32 = jnp.float32
BF16 = jnp.bfloat16
EPS = 1e-6

GRID_W = 64
N_HEADS = 4
HEAD_DIM = 128
GROUP_W = N_HEADS * HEAD_DIM
DN_CHUNK = 64
SEQ_TILE = 256
CHUNKS_PER_TILE = SEQ_TILE // DN_CHUNK
DN_BLOCK = 128
BLOCKS_PER_TILE = SEQ_TILE // DN_BLOCK
SG_CHUNK = 128
POOL_WINDOWS = (2, 4, 8, 16)
SHORT_CONV = 5
CONF_CONV = 31
N_EXPERTS = 8
LANES = 128
SUBLANES = 8
VMEM_LIMIT = 56 * 1024 * 1024

QKV_OFF, QKV_W = 0, 3 * GROUP_W
Z_OFF = QKV_OFF + QKV_W
SG_OFF = Z_OFF + GROUP_W
CV_OFF = SG_OFF + 2 * GROUP_W
POOL_OFF = CV_OFF + 2 * GROUP_W
AB_OFF = POOL_OFF + GROUP_W
P_WIDTH = AB_OFF + LANES
AB_COLS = 4 * N_HEADS

SHM, SCM, GM, SHF, SCF, GF = range(6)


def _dot(a, b):
    return jnp.dot(a, b, preferred_element_type=F32)


def _dot_nt(a, b):
    return lax.dot_general(a, b, (((1,), (1,)), ((), ())), preferred_element_type=F32)


def _dot_tn(a, b):
    return lax.dot_general(a, b, (((0,), (0,)), ((), ())), preferred_element_type=F32)


def _sigmoid(x):
    return 1.0 / (1.0 + jnp.exp(-x))


def _silu(x):
    return x * _sigmoid(x)


def _cparams(*sem):
    return pltpu.CompilerParams(dimension_semantics=sem, vmem_limit_bytes=VMEM_LIMIT)


def _mod_row(i, *, tile, ctx_rows, seq, batch):
    start = i * tile
    return jnp.where(start < ctx_rows, batch, (start - ctx_rows) // seq)


def _mod_spec(d, chunk, row_fn):
    return pl.BlockSpec((None, 1, d), lambda i, *_: (row_fn(i), 0, chunk))


def _norm_modulate(x, g, scale, shift):
    y = x * lax.rsqrt(jnp.mean(x * x, axis=-1, keepdims=True) + EPS) * g
    return y * (1.0 + scale) + shift


def _ada_kernel(c_ref, w_ref, b_ref, o_ref):
    s = _silu(c_ref[...]).astype(BF16)
    o_ref[...] = _dot(s, w_ref[...].astype(BF16)) + b_ref[...]


def _ada(cc, ada_w, ada_b):
    nl, d, n = ada_w.shape
    tn = 1024
    return pl.pallas_call(
        _ada_kernel,
        grid=(nl, n // tn),
        in_specs=[pl.BlockSpec((SUBLANES, d), lambda l, j: (0, 0)),
                  pl.BlockSpec((None, d, tn), lambda l, j: (l, 0, j)),
                  pl.BlockSpec((None, 1, tn), lambda l, j: (l, 0, j))],
        out_specs=pl.BlockSpec((None, SUBLANES, tn), lambda l, j: (l, 0, j)),
        out_shape=jax.ShapeDtypeStruct((nl, SUBLANES, n), F32),
        compiler_params=_cparams("parallel", "parallel"),
        name="ada_modulation",
    )(cc, ada_w, ada_b.reshape(nl, 1, n))


def _proj_kernel(h_ref, g_ref, sh_ref, sc_ref, w_ref, o_ref):
    y = _norm_modulate(h_ref[...], g_ref[...], sc_ref[...], sh_ref[...])
    o_ref[...] = _dot(y.astype(BF16), w_ref[...])


def _proj(h, g, mod, w, row_fn):
    m, d = h.shape
    n = w.shape[1]
    tm = SEQ_TILE
    return pl.pallas_call(
        _proj_kernel,
        grid=(m // tm,),
        in_specs=[pl.BlockSpec((tm, d), lambda i: (i, 0)),
                  pl.BlockSpec((1, d), lambda i: (0, 0)),
                  _mod_spec(d, SHM, row_fn),
                  _mod_spec(d, SCM, row_fn),
                  pl.BlockSpec((d, n), lambda i: (0, 0), pipeline_mode=pl.Buffered(1))],
        out_specs=pl.BlockSpec((tm, n), lambda i: (i, 0)),
        out_shape=jax.ShapeDtypeStruct((m, n), F32),
        compiler_params=_cparams("parallel"),
        name="in_proj",
    )(h, g.reshape(1, d), mod, mod, w)


def _segment_pos(i, n_ctx_tiles, ctx_seg_tiles, x_seg_tiles):
    in_ctx = i < n_ctx_tiles
    tpos = jnp.where(in_ctx, i % ctx_seg_tiles, (i - n_ctx_tiles) % x_seg_tiles)
    tlen = jnp.where(in_ctx, ctx_seg_tiles, x_seg_tiles)
    return tpos, tlen


def _fill_ext(ext_ref, prev, cur, nxt, first, last, halo):
    ext_ref[0:halo, :] = jnp.where(first, 0.0, prev)
    ext_ref[halo:halo + SEQ_TILE, :] = cur
    ext_ref[halo + SEQ_TILE:2 * halo + SEQ_TILE, :] = jnp.where(last, 0.0, nxt)


PK_W, PK_QG, PK_KD, PK_QK = 0, GROUP_W, 2 * GROUP_W, 3 * GROUP_W
PK_WIDTH = 4 * GROUP_W


def _dn_local_kernel(qkv_ref, prev_ref, next_ref, ab_ref, cw_ref, alog_ref, dtb_ref,
                     uf_ref, pkf_ref, ub_ref, pkb_ref, gam_ref,
                     ext_ref, t_ref, l_ref, rhs_ref, *, seg):
    tpos, tlen = _segment_pos(pl.program_id(0), *seg)
    halo = SUBLANES
    _fill_ext(ext_ref, prev_ref[...], qkv_ref[...], next_ref[...], tpos == 0, tpos == tlen - 1, halo)

    pad = SHORT_CONV // 2
    acc = cw_ref[0:1, :] * ext_ref[halo - pad:halo - pad + SEQ_TILE, :]
    for j in range(1, SHORT_CONV):
        acc = acc + cw_ref[j:j + 1, :] * ext_ref[halo - pad + j:halo - pad + j + SEQ_TILE, :]
    s = _silu(acc)

    ab = ab_ref[...]
    lane = lax.broadcasted_iota(jnp.int32, (SEQ_TILE, LANES), 1)
    xs = ab + dtb_ref[...]
    softplus = jnp.maximum(xs, 0.0) + jnp.log1p(jnp.exp(-jnp.abs(xs)))
    la = jnp.where(lane < 2 * N_HEADS, -jnp.exp(alog_ref[...]) * softplus, 0.0)
    beta = _sigmoid(ab)

    def chunk_masks(n):
        rr = lax.broadcasted_iota(jnp.int32, (n, n), 0)
        cc = lax.broadcasted_iota(jnp.int32, (n, n), 1)
        same = (rr // DN_CHUNK) == (cc // DN_CHUNK)
        return rr, cc, (same & (cc <= rr), same & (cc >= rr)), (same & (cc < rr), same & (cc > rr))

    _, _, incl_tile, _ = chunk_masks(SEQ_TILE)
    r, c, incl, strict = chunk_masks(DN_BLOCK)

    def off_block(s, d):
        hi, lo = (r, c) if d == 0 else (c, r)
        return ((hi // s) % 2 == 1) & ((lo // s) == (hi // s) - 1)

    la_hi = la.astype(BF16)
    rem = la - la_hi.astype(F32)
    la_mid = rem.astype(BF16)
    la_lo = (rem - la_mid.astype(F32)).astype(BF16)

    def chunk_sum(mask):
        m = jnp.where(mask, 1.0, 0.0).astype(BF16)
        return _dot(m, la_hi) + _dot(m, la_mid) + _dot(m, la_lo)

    g_fwd = chunk_sum(incl_tile[0])
    g_bwd = chunk_sum(incl_tile[1])
    g = jnp.where(lane < N_HEADS, g_fwd, g_bwd)
    g_tot = g_fwd + g_bwd - la
    g_t = g.T

    outs = ((uf_ref, pkf_ref), (ub_ref, pkb_ref))
    eye = jnp.where(r == c, 1.0, 0.0)
    for h in range(N_HEADS):
        sl = slice(h * HEAD_DIM, (h + 1) * HEAD_DIM)
        qh = s[:, h * HEAD_DIM:(h + 1) * HEAD_DIM]
        kh = s[:, GROUP_W + h * HEAD_DIM:GROUP_W + (h + 1) * HEAD_DIM]
        vh = s[:, 2 * GROUP_W + h * HEAD_DIM:2 * GROUP_W + (h + 1) * HEAD_DIM]
        qh = qh * lax.rsqrt(jnp.sum(qh * qh, axis=-1, keepdims=True) + EPS) * (HEAD_DIM ** -0.5)
        kh = kh * lax.rsqrt(jnp.sum(kh * kh, axis=-1, keepdims=True) + EPS)
        qh16 = qh.astype(BF16)
        kh16 = kh.astype(BF16)
        for d in range(2):
            gi = d * N_HEADS + h
            u_ref, pk_ref = outs[d]
            gcol = g[:, gi:gi + 1]
            bcol = beta[:, 2 * N_HEADS + gi:2 * N_HEADS + gi + 1]
            eg = jnp.exp(gcol)
            kb = kh * bcol
            rhs_ref[gi, :, :HEAD_DIM] = vh * bcol
            rhs_ref[gi, :, HEAD_DIM:] = kb * eg
            gl = g_tot[:, gi:gi + 1]
            pk_ref[:, PK_QG + h * HEAD_DIM:PK_QG + (h + 1) * HEAD_DIM] = (qh * eg).astype(BF16)
            pk_ref[:, PK_KD + h * HEAD_DIM:PK_KD + (h + 1) * HEAD_DIM] = (kh * jnp.exp(gl - gcol)).astype(BF16)
            pk_ref[:, PK_QK + h * HEAD_DIM + DN_CHUNK:PK_QK + (h + 1) * HEAD_DIM] = jnp.zeros(
                (SEQ_TILE, HEAD_DIM - DN_CHUNK), BF16)
            for cb in range(CHUNKS_PER_TILE):
                gam_ref[cb, :, gi * LANES:(gi + 1) * LANES] = jnp.broadcast_to(
                    jnp.exp(gl[cb * DN_CHUNK:cb * DN_CHUNK + 1, :]), (1, LANES))
            for blk in range(BLOCKS_PER_TILE):
                rows = slice(blk * DN_BLOCK, (blk + 1) * DN_BLOCK)
                unit = gi * BLOCKS_PER_TILE + blk
                grow = g_t[gi:gi + 1, rows]
                decay = jnp.where(incl[d], jnp.exp(jnp.where(incl[d], gcol[rows] - grow, 0.0)), 0.0)
                lmat = jnp.where(strict[d], bcol[rows] * _dot_nt(kh16[rows], kh16[rows]) * decay, 0.0)
                l_ref[unit] = lmat
                t_ref[unit] = eye - jnp.where(off_block(1, d), lmat, 0.0)
                qkm = jnp.where(incl[d], _dot_nt(qh16[rows], kh16[rows]) * decay, 0.0)
                qk = qkm[:, 0:DN_CHUNK]
                for cb in range(1, DN_BLOCK // DN_CHUNK):
                    qk = qk + qkm[:, cb * DN_CHUNK:(cb + 1) * DN_CHUNK]
                pk_ref[rows, PK_QK + h * HEAD_DIM:PK_QK + h * HEAD_DIM + DN_CHUNK] = qk.astype(BF16)

    n_chain = 2 * N_HEADS
    n_unit = n_chain * BLOCKS_PER_TILE
    unit_dir = lambda unit: unit // (N_HEADS * BLOCKS_PER_TILE)
    s_blk = 2
    while s_blk < DN_CHUNK:
        lt = []
        for unit in range(n_unit):
            l_s = jnp.where(off_block(s_blk, unit_dir(unit)), l_ref[unit], 0.0).astype(BF16)
            lt.append(_dot(l_s, t_ref[unit].astype(BF16)).astype(BF16))
        for unit in range(n_unit):
            t_inv = t_ref[unit]
            t_ref[unit] = t_inv - _dot(t_inv.astype(BF16), lt[unit])
        s_blk *= 2
    for unit in range(n_unit):
        gi, blk = divmod(unit, BLOCKS_PER_TILE)
        d, h = divmod(gi, N_HEADS)
        rows = slice(blk * DN_BLOCK, (blk + 1) * DN_BLOCK)
        u_ref, pk_ref = outs[d]
        rhs = rhs_ref[gi, rows, :]
        sol = rhs + _dot(jnp.where(strict[d], t_ref[unit], 0.0).astype(BF16), rhs.astype(BF16))
        u_ref[rows, h * HEAD_DIM:(h + 1) * HEAD_DIM] = sol[:, :HEAD_DIM]
        pk_ref[rows, PK_W + h * HEAD_DIM:PK_W + (h + 1) * HEAD_DIM] = sol[:, HEAD_DIM:].astype(BF16)


def _batch_major_tile(i, n_ctx_tiles, ctx_seg_tiles, x_seg_tiles):
    per_batch = ctx_seg_tiles + x_seg_tiles
    j = i - n_ctx_tiles
    return jnp.where(i < n_ctx_tiles,
                     (i // ctx_seg_tiles) * per_batch + i % ctx_seg_tiles,
                     (j // x_seg_tiles) * per_batch + ctx_seg_tiles + j % x_seg_tiles)


def _dn_local(p, conv_w, a_log, dt_bias, seg):
    rows = p.shape[0]
    nt = rows // SEQ_TILE
    hb = SEQ_TILE // SUBLANES
    last_hb = rows // SUBLANES - 1
    cw = jnp.zeros((SUBLANES, QKV_W), F32).at[:SHORT_CONV].set(conv_w)
    pad_row = lambda v: jnp.zeros((1, LANES), F32).at[0, :2 * N_HEADS].set(v.reshape(-1))
    u_out = jax.ShapeDtypeStruct((rows, GROUP_W), F32)
    pk_out = jax.ShapeDtypeStruct((rows, PK_WIDTH), BF16)
    gam_out = jax.ShapeDtypeStruct((rows // DN_CHUNK, 1, 2 * N_HEADS * LANES), F32)
    dst = lambda i: _batch_major_tile(i, *seg)
    u_spec = pl.BlockSpec((SEQ_TILE, GROUP_W), lambda i: (dst(i), 0))
    pk_spec = pl.BlockSpec((SEQ_TILE, PK_WIDTH), lambda i: (dst(i), 0))
    full = lambda shape: pl.BlockSpec(shape, lambda i: (0,) * len(shape))
    n_chain = 2 * N_HEADS
    return pl.pallas_call(
        functools.partial(_dn_local_kernel, seg=seg),
        grid=(nt,),
        in_specs=[pl.BlockSpec((SEQ_TILE, QKV_W), lambda i: (i, QKV_OFF // QKV_W)),
                  pl.BlockSpec((SUBLANES, QKV_W), lambda i: (jnp.maximum(i * hb - 1, 0), 0)),
                  pl.BlockSpec((SUBLANES, QKV_W), lambda i: (jnp.minimum((i + 1) * hb, last_hb), 0)),
                  pl.BlockSpec((SEQ_TILE, LANES), lambda i: (i, AB_OFF // LANES)),
                  full((SUBLANES, QKV_W)), full((1, LANES)), full((1, LANES))],
        out_specs=[u_spec, pk_spec, u_spec, pk_spec,
                   pl.BlockSpec((CHUNKS_PER_TILE, 1, 2 * N_HEADS * LANES), lambda i: (dst(i), 0, 0))],
        out_shape=[u_out, pk_out, u_out, pk_out, gam_out],
        scratch_shapes=[pltpu.VMEM((SEQ_TILE + 2 * SUBLANES, QKV_W), F32),
                        pltpu.VMEM((n_chain * BLOCKS_PER_TILE, DN_BLOCK, DN_BLOCK), F32),
                        pltpu.VMEM((n_chain * BLOCKS_PER_TILE, DN_BLOCK, DN_BLOCK), F32),
                        pltpu.VMEM((n_chain, SEQ_TILE, 2 * HEAD_DIM), F32)],
        compiler_params=_cparams("parallel"),
        name="dn_local",
    )(p, p, p, p, cw, pad_row(a_log), pad_row(dt_bias))


def _dn_scan_kernel(uf, pkf, gamf, ub, pkb, gamb, of_ref, ob_ref, s_ref, *, batch):
    @pl.when(pl.program_id(0) == 0)
    def _():
        s_ref[...] = jnp.zeros_like(s_ref)

    dirs = ((uf, pkf, gamf, of_ref), (ub, pkb, gamb, ob_ref))
    chains = [(b, d, h) for b in range(batch) for d in range(2) for h in range(N_HEADS)]
    head = lambda off, h: slice(off + h * HEAD_DIM, off + (h + 1) * HEAD_DIM)

    ws = []
    for ci, (b, d, h) in enumerate(chains):
        pk = dirs[d][1]
        lhs = jnp.concatenate([pk[b, :, head(PK_W, h)], pk[b, :, head(PK_QG, h)]], axis=0)
        ws.append(_dot(lhs, s_ref[ci].astype(BF16)))
    v16 = []
    for ci, (b, d, h) in enumerate(chains):
        v16.append((dirs[d][0][b, :, head(0, h)] - ws[ci][:DN_CHUNK]).astype(BF16))
    for ci, (b, d, h) in enumerate(chains):
        pk, o_ref = dirs[d][1], dirs[d][3]
        qk = pk[b, :, PK_QK + h * HEAD_DIM:PK_QK + h * HEAD_DIM + DN_CHUNK]
        o_ref[b, :, head(0, h)] = ws[ci][DN_CHUNK:] + _dot(qk, v16[ci])
    for ci, (b, d, h) in enumerate(chains):
        pk, gam = dirs[d][1], dirs[d][2]
        gi = d * N_HEADS + h
        s_ref[ci] = s_ref[ci] * gam[b, :, gi * LANES:(gi + 1) * LANES] + _dot_tn(pk[b, :, head(PK_KD, h)], v16[ci])


def _dn_scan(local, *, batch, ctx_len, seq):
    uf, pkf, ub, pkb, gam = local
    cc, xc = ctx_len // DN_CHUNK, seq // DN_CHUNK
    nc = cc + xc
    bwd = lambda s: jnp.where(s < cc, cc - 1 - s, cc + (xc - 1 - (s - cc)))
    by_chunk = lambda a: a.reshape(batch, nc, *a.shape[1:]) if a.ndim == 3 else a.reshape(batch, nc, DN_CHUNK, a.shape[1])

    def specs(idx):
        return [pl.BlockSpec((batch, None, DN_CHUNK, GROUP_W), lambda s: (0, idx(s), 0, 0)),
                pl.BlockSpec((batch, None, DN_CHUNK, PK_WIDTH), lambda s: (0, idx(s), 0, 0)),
                pl.BlockSpec((batch, None, 1, 2 * N_HEADS * LANES), lambda s: (0, idx(s), 0, 0))]

    out = jax.ShapeDtypeStruct((batch, nc, DN_CHUNK, GROUP_W), F32)
    fwd = lambda s: s
    o_f, o_b = pl.pallas_call(
        functools.partial(_dn_scan_kernel, batch=batch),
        grid=(nc,),
        in_specs=specs(fwd) + specs(bwd),
        out_specs=[pl.BlockSpec((batch, None, DN_CHUNK, GROUP_W), lambda s: (0, s, 0, 0)),
                   pl.BlockSpec((batch, None, DN_CHUNK, GROUP_W), lambda s: (0, bwd(s), 0, 0))],
        out_shape=[out, out],
        scratch_shapes=[pltpu.VMEM((batch * 2 * N_HEADS, HEAD_DIM, HEAD_DIM), F32)],
        compiler_params=_cparams("arbitrary"),
        name="dn_scan",
    )(by_chunk(uf), by_chunk(pkf), by_chunk(gam), by_chunk(ub), by_chunk(pkb), by_chunk(gam))
    return o_f.reshape(-1, GROUP_W), o_b.reshape(-1, GROUP_W)


def _layernorm(x, g, b):
    mu = jnp.mean(x, axis=-1, keepdims=True)
    xc = x - mu
    var = jnp.mean(xc * xc, axis=-1, keepdims=True)
    return xc * lax.rsqrt(var + EPS) * g + b


def _mix_kernel(h_ref, of_ref, ob_ref, z_ref, sg_ref, cv_ref, cvp_ref, cvn_ref, pool_ref, poolp_ref, pooln_ref,
                gm_ref, dng_ref, sglg_ref, sglb_ref, sgw_ref, sgbt_ref, pw_ref, ps_ref,
                cvw_ref, cvb_ref, cvlg_ref, cvlb_ref, wout_ref,
                o_ref, y_ref, pext_ref, cext_ref, cacc_ref, *, seg, tile0):
    tpos, tlen = _segment_pos(pl.program_id(0) + tile0, *seg)
    first, last = tpos == 0, tpos == tlen - 1

    o = of_ref[...] + ob_ref[...]
    z = z_ref[...]
    for h in range(N_HEADS):
        sl = slice(h * HEAD_DIM, (h + 1) * HEAD_DIM)
        oh = o[:, sl]
        yh = oh * lax.rsqrt(jnp.mean(oh * oh, axis=-1, keepdims=True) + EPS) * dng_ref[...] * _silu(z[:, sl])
        y_ref[:, sl] = yh.astype(BF16)

    psg = sg_ref[...]
    psg = 0.5 * psg * (1.0 + jnp.tanh(math.sqrt(2.0 / math.pi) * (psg + 0.044715 * (psg * psg * psg))))
    u = psg[:, :GROUP_W]
    v = _layernorm(psg[:, GROUP_W:], sglg_ref[...], sglb_ref[...]).astype(BF16)
    for n in range(SEQ_TILE // SG_CHUNK):
        rs = slice(n * SG_CHUNK, (n + 1) * SG_CHUNK)
        for h in range(N_HEADS):
            sl = slice(h * HEAD_DIM, (h + 1) * HEAD_DIM)
            sv = _dot(sgw_ref[h], v[rs, sl]) + sgbt_ref[:, h:h + 1]
            y_ref[rs, GROUP_W + h * HEAD_DIM:GROUP_W + (h + 1) * HEAD_DIM] = (u[rs, sl] * sv).astype(BF16)

    halo = SUBLANES
    _fill_ext(pext_ref, poolp_ref[...], pool_ref[...], pooln_ref[...], first, last, halo)
    t = tpos * SEQ_TILE + lax.broadcasted_iota(jnp.int32, (SEQ_TILE, 1), 0)
    seg_len = tlen * SEQ_TILE
    for gi, win in enumerate(POOL_WINDOWS):
        sl = slice(gi * LANES, (gi + 1) * LANES)
        tot = pext_ref[halo - win // 2:halo - win // 2 + SEQ_TILE, sl]
        for m in range(1 - win // 2, win // 2):
            tot = tot + pext_ref[halo + m:halo + m + SEQ_TILE, sl]
        cnt = jnp.clip(t + win // 2, 0, seg_len) - jnp.clip(t - win // 2, 0, seg_len)
        yg = tot / cnt.astype(F32) - pool_ref[:, sl]
        yg = _dot(yg.astype(BF16), pw_ref[gi]) * ps_ref[:, sl]
        y_ref[:, 2 * GROUP_W + gi * LANES:2 * GROUP_W + (gi + 1) * LANES] = yg.astype(BF16)

    halo = 2 * SUBLANES
    glu = lambda p: p[:, :GROUP_W] * _sigmoid(p[:, GROUP_W:])
    _fill_ext(cext_ref, glu(cvp_ref[...]), glu(cv_ref[...]), glu(cvn_ref[...]), first, last, halo)
    pad = CONF_CONV // 2
    span = SEQ_TILE + SUBLANES
    acc = None
    for rho in range(SUBLANES):
        part = None
        for j in range(CONF_CONV):
            off = halo - pad + j
            if off % SUBLANES == rho:
                term = cvw_ref[j:j + 1, :] * cext_ref[off - rho:off - rho + span, :]
                part = term if part is None else part + term
        if part is not None:
            cacc_ref[...] = part
            shifted = cacc_ref[rho:rho + SEQ_TILE, :]
            acc = shifted if acc is None else acc + shifted
    yc = _silu(_layernorm(acc + cvb_ref[...], cvlg_ref[...], cvlb_ref[...]))
    y_ref[:, 3 * GROUP_W:] = yc.astype(BF16)

    o_ref[...] = h_ref[...] + gm_ref[...] * _dot(y_ref[...], wout_ref[...])


def _mix(h, p, o_f, o_b, mod, params, wout, *, seg, row_fn, tile0, n_tiles):
    d = h.shape[1]
    (dng, sglg, sglb, sgw, sgb, pw, ps, cvw, cvb, cvlg, cvlb) = params
    rows = p.shape[0]
    row = lambda v: v.reshape(1, -1)
    cvw_p = jnp.zeros((4 * SUBLANES, GROUP_W), F32).at[:CONF_CONV].set(cvw)
    sgbt = jnp.zeros((SG_CHUNK, LANES), F32).at[:, :N_HEADS].set(sgb.T)

    def halo_specs(width, col, halo):
        hb = SEQ_TILE // halo
        last_hb = rows // halo - 1
        return [pl.BlockSpec((SEQ_TILE, width), lambda i: (i + tile0, col)),
                pl.BlockSpec((halo, width), lambda i: (jnp.maximum((i + tile0) * hb - 1, 0), col)),
                pl.BlockSpec((halo, width), lambda i: (jnp.minimum((i + tile0 + 1) * hb, last_hb), col))]

    full = lambda a: pl.BlockSpec(a.shape, lambda i: (0,) * a.ndim)
    consts = [row(dng), row(sglg), row(sglb), sgw.astype(BF16), sgbt, pw.astype(BF16), row(ps),
              cvw_p, row(cvb), row(cvlg), row(cvlb)]
    return pl.pallas_call(
        functools.partial(_mix_kernel, seg=seg, tile0=tile0),
        grid=(n_tiles,),
        in_specs=[pl.BlockSpec((SEQ_TILE, d), lambda i: (i + tile0, 0)),
                  pl.BlockSpec((SEQ_TILE, GROUP_W), lambda i: (_batch_major_tile(i + tile0, *seg), 0)),
                  pl.BlockSpec((SEQ_TILE, GROUP_W), lambda i: (_batch_major_tile(i + tile0, *seg), 0)),
                  pl.BlockSpec((SEQ_TILE, GROUP_W), lambda i: (i + tile0, Z_OFF // GROUP_W)),
                  pl.BlockSpec((SEQ_TILE, 2 * GROUP_W), lambda i: (i + tile0, SG_OFF // (2 * GROUP_W)))]
                 + halo_specs(2 * GROUP_W, CV_OFF // (2 * GROUP_W), 2 * SUBLANES)
                 + halo_specs(GROUP_W, POOL_OFF // GROUP_W, SUBLANES)
                 + [_mod_spec(d, GM, lambda i: row_fn(i + tile0))]
                 + [full(a) for a in consts]
                 + [pl.BlockSpec(wout.shape, lambda i: (0, 0), pipeline_mode=pl.Buffered(1))],
        out_specs=pl.BlockSpec((SEQ_TILE, d), lambda i: (i, 0)),
        out_shape=jax.ShapeDtypeStruct((n_tiles * SEQ_TILE, d), F32),
        scratch_shapes=[pltpu.VMEM((SEQ_TILE, d), BF16),
                        pltpu.VMEM((SEQ_TILE + 2 * SUBLANES, GROUP_W), F32),
                        pltpu.VMEM((SEQ_TILE + 4 * SUBLANES, GROUP_W), F32),
                        pltpu.VMEM((SEQ_TILE + SUBLANES, GROUP_W), F32)],
        compiler_params=_cparams("parallel"),
        name="mixers_out_proj",
    )(h, o_f, o_b, p, p, p, p, p, p, p, p, mod, *consts, wout)


def _ffn_kernel(h_ref, g_ref, sh_ref, sc_ref, gate_ref, w1_ref, w3_ref, w2_ref, o_ref, xn_ref, acc_ref):
    j = pl.program_id(1)

    @pl.when(j == 0)
    def _():
        xn_ref[...] = _norm_modulate(h_ref[...], g_ref[...], sc_ref[...], sh_ref[...]).astype(BF16)
        acc_ref[...] = jnp.zeros_like(acc_ref)

    xn = xn_ref[...]
    mid = _silu(_dot(xn, w1_ref[...])) * _dot(xn, w3_ref[...])
    acc_ref[...] += _dot(mid.astype(BF16), w2_ref[...])

    @pl.when(j == pl.num_programs(1) - 1)
    def _():
        o_ref[...] = h_ref[...] + gate_ref[...] * acc_ref[...]


def _ffn(h, g, mod, w1, w3, w2, row_fn_for, tm):
    m, d = h.shape
    f = w1.shape[1]
    tf = 512
    row_fn = row_fn_for(tm)
    return pl.pallas_call(
        _ffn_kernel,
        grid=(m // tm, f // tf),
        in_specs=[pl.BlockSpec((tm, d), lambda i, j: (i, 0)),
                  pl.BlockSpec((1, d), lambda i, j: (0, 0)),
                  _mod_spec(d, SHF, row_fn), _mod_spec(d, SCF, row_fn), _mod_spec(d, GF, row_fn),
                  pl.BlockSpec((d, tf), lambda i, j: (0, j)),
                  pl.BlockSpec((d, tf), lambda i, j: (0, j)),
                  pl.BlockSpec((tf, d), lambda i, j: (j, 0))],
        out_specs=pl.BlockSpec((tm, d), lambda i, j: (i, 0)),
        out_shape=jax.ShapeDtypeStruct((m, d), F32),
        scratch_shapes=[pltpu.VMEM((tm, d), BF16), pltpu.VMEM((tm, d), F32)],
        compiler_params=_cparams("parallel", "arbitrary"),
        name="dense_swiglu",
    )(h, g.reshape(1, d), mod, mod, mod, w1, w3, w2)


ROUTE_TILE = 256
EXPERT_ROWS = 2048


def _router_kernel(h_ref, g_ref, sh_ref, sc_ref, rw_ref, rb_ref, hn_ref, idx_ref, gate_ref, total_ref, count_ref):
    hn = _norm_modulate(h_ref[...], g_ref[...], sc_ref[...], sh_ref[...])
    hn_ref[...] = hn
    logits = jnp.dot(hn, rw_ref[...], preferred_element_type=F32, precision=lax.Precision.HIGHEST) + rb_ref[...]
    lane = lax.broadcasted_iota(jnp.int32, logits.shape, 1)
    lane_f = lane.astype(F32)
    neg = -jnp.inf
    lg = jnp.where(lane < N_EXPERTS, logits, neg)
    m1 = jnp.max(lg, axis=-1, keepdims=True)
    i1 = jnp.min(jnp.where(lg == m1, lane_f, float(LANES)), axis=-1, keepdims=True)
    lg2 = jnp.where(lane_f == i1, neg, lg)
    m2 = jnp.max(lg2, axis=-1, keepdims=True)
    i2 = jnp.min(jnp.where(lg2 == m2, lane_f, float(LANES)), axis=-1, keepdims=True)
    e2 = jnp.exp(m2 - m1)
    g1 = 1.0 / (1.0 + e2)
    g2 = e2 / (1.0 + e2)
    gate_ref[...] = jnp.where(lane == 0, g1, jnp.where(lane == 1, g2, 0.0))

    @pl.when(pl.program_id(0) == 0)
    def _():
        count_ref[...] = jnp.zeros_like(count_ref)

    pick1 = jnp.where(lane_f == i1, 1.0, 0.0)
    pick2 = jnp.where(lane_f == i2, 1.0, 0.0)
    picked = pick1 + pick2
    r = lax.broadcasted_iota(jnp.int32, (ROUTE_TILE, ROUTE_TILE), 0)
    c = lax.broadcasted_iota(jnp.int32, (ROUTE_TILE, ROUTE_TILE), 1)
    earlier = jnp.where(c < r, 1.0, 0.0).astype(BF16)
    before = count_ref[0:1, :] + _dot(earlier, picked.astype(BF16))
    rank1 = jnp.sum(pick1 * before, axis=-1, keepdims=True)
    rank2 = jnp.sum(pick2 * before, axis=-1, keepdims=True)
    count_ref[...] = count_ref[...] + jnp.sum(picked, axis=0, keepdims=True)
    total_ref[...] = count_ref[...]
    idx_ref[...] = jnp.where(lane == 0, i1, jnp.where(lane == 1, i2, jnp.where(
        lane == 2, rank1, jnp.where(lane == 3, rank2, 0.0)))).astype(jnp.int32)


def _router(h, g, mod, rw, rb, row_fn, tile0):
    d = h.shape[1]
    tm = ROUTE_TILE
    n_tiles = h.shape[0] // tm - tile0
    m = n_tiles * tm
    rw_p = jnp.zeros((d, LANES), F32).at[:, :N_EXPERTS].set(rw)
    rb_p = jnp.zeros((1, LANES), F32).at[0, :N_EXPERTS].set(rb)
    shifted = lambda i: row_fn(i + tile0)
    return pl.pallas_call(
        _router_kernel,
        grid=(n_tiles,),
        in_specs=[pl.BlockSpec((tm, d), lambda i: (i + tile0, 0)),
                  pl.BlockSpec((1, d), lambda i: (0, 0)),
                  _mod_spec(d, SHF, shifted), _mod_spec(d, SCF, shifted),
                  pl.BlockSpec((d, LANES), lambda i: (0, 0)),
                  pl.BlockSpec((1, LANES), lambda i: (0, 0))],
        out_specs=[pl.BlockSpec((tm, d), lambda i: (i, 0)),
                   pl.BlockSpec((tm, LANES), lambda i: (i, 0)),
                   pl.BlockSpec((tm, LANES), lambda i: (i, 0)),
                   pl.BlockSpec((SUBLANES, LANES), lambda i: (0, 0))],
        out_shape=[jax.ShapeDtypeStruct((m, d), F32),
                   jax.ShapeDtypeStruct((m, LANES), jnp.int32),
                   jax.ShapeDtypeStruct((m, LANES), F32),
                   jax.ShapeDtypeStruct((SUBLANES, LANES), F32)],
        scratch_shapes=[pltpu.VMEM((SUBLANES, LANES), F32)],
        compiler_params=_cparams("arbitrary"),
        name="moe_router",
    )(h, g.reshape(1, d), mod, mod, rw_p, rb_p)


def _routing_tables(route, totals, n_slots):
    e_flat = route[:, 0:2].reshape(-1)
    rank = route[:, 2:4].reshape(-1)
    counts = totals[0, :N_EXPERTS].astype(jnp.int32)
    padded = (counts + EXPERT_ROWS - 1) // EXPERT_ROWS * EXPERT_ROWS
    ends = jnp.cumsum(padded)
    starts = ends - padded
    onehot = (e_flat[:, None] == jnp.arange(N_EXPERTS)[None, :]).astype(jnp.int32)
    slot = jnp.sum(onehot * starts[None, :], axis=1) + rank
    n_super = n_slots // EXPERT_ROWS
    super_start = jnp.arange(n_super, dtype=jnp.int32) * EXPERT_ROWS
    super_expert = jnp.minimum(jnp.sum(super_start[:, None] >= ends[None, :], axis=1), N_EXPERTS - 1).astype(jnp.int32)
    valid_rows = jnp.clip(counts[super_expert] - (super_start - starts[super_expert]), 0, EXPERT_ROWS)
    valid_rows = jnp.where(super_start < ends[-1], valid_rows, 0)
    super_sub = ((valid_rows + ROUTE_TILE - 1) // ROUTE_TILE).astype(jnp.int32)
    tile_valid = (jnp.arange(n_slots // ROUTE_TILE, dtype=jnp.int32) % (EXPERT_ROWS // ROUTE_TILE)
                  < jnp.repeat(super_sub, EXPERT_ROWS // ROUTE_TILE)).astype(jnp.int32)
    return slot.astype(jnp.int32), super_expert, super_sub, tile_valid


def _row_copy(src_hbm, dst_vmem, src_row, dst_row, sem):
    return pltpu.make_async_copy(src_hbm.at[pl.ds(src_row, 1), :], dst_vmem.at[pl.ds(dst_row, 1), :], sem)


def _gather_kernel(slot_ref, valid_ref, hn_hbm, o_ref, buf_ref, tok_ref, sem):
    i = pl.program_id(0)
    n = pl.num_programs(0)

    @pl.when(i == 0)
    def _():
        def clear(p, carry):
            tok_ref[p] = 0
            return carry

        def put(a, carry):
            tok_ref[slot_ref[a]] = a // 2
            return carry

        lax.fori_loop(0, tok_ref.shape[0], clear, 0, unroll=8)
        lax.fori_loop(0, slot_ref.shape[0], put, 0, unroll=8)

    def request(tile):
        slot = tile % 2

        def start(r, carry):
            _row_copy(hn_hbm, buf_ref.at[slot], tok_ref[tile * ROUTE_TILE + r], r, sem.at[slot]).start()
            return carry

        lax.fori_loop(0, ROUTE_TILE, start, 0, unroll=8)

    @pl.when((i == 0) & (valid_ref[0] != 0))
    def _():
        request(0)

    nxt = jnp.minimum(i + 1, n - 1)

    @pl.when((i + 1 < n) & (valid_ref[nxt] != 0))
    def _():
        request(nxt)

    @pl.when(valid_ref[i] == 0)
    def _():
        o_ref[...] = jnp.zeros_like(o_ref)

    @pl.when(valid_ref[i] != 0)
    def _():
        slot = i % 2

        pltpu.make_async_copy(hn_hbm.at[pl.ds(0, ROUTE_TILE), :], buf_ref.at[slot], sem.at[slot]).wait()
        o_ref[...] = buf_ref[slot].astype(BF16)


def _gather_rows(hn, slot, tile_valid, n_slots):
    d = hn.shape[1]
    return pl.pallas_call(
        _gather_kernel,
        grid_spec=pltpu.PrefetchScalarGridSpec(
            num_scalar_prefetch=2,
            grid=(n_slots // ROUTE_TILE,),
            in_specs=[pl.BlockSpec(memory_space=pl.ANY)],
            out_specs=pl.BlockSpec((ROUTE_TILE, d), lambda i, *_: (i, 0)),
            scratch_shapes=[pltpu.VMEM((2, ROUTE_TILE, d), F32), pltpu.SMEM((n_slots,), jnp.int32),
                            pltpu.SemaphoreType.DMA((2,))]),
        out_shape=jax.ShapeDtypeStruct((n_slots, d), BF16),
        compiler_params=_cparams("arbitrary"),
        name="moe_gather",
    )(slot, tile_valid, hn)


WEIGHT_SPLIT = 4


def _expert_kernel(exp_ref, sub_ref, x_ref, *refs):
    w1_parts = refs[:WEIGHT_SPLIT]
    w3_parts = refs[WEIGHT_SPLIT:2 * WEIGHT_SPLIT]
    w2_parts = refs[2 * WEIGHT_SPLIT:3 * WEIGHT_SPLIT]
    o_ref, w1b_ref, w3b_ref, w2b_ref = refs[3 * WEIGHT_SPLIT:]
    s, j = pl.program_id(0), pl.program_id(1)

    @pl.when(j == 0)
    def _():
        o_ref[...] = jnp.zeros_like(o_ref)

    @pl.when(sub_ref[s] > 0)
    def _():
        for parts, dst in ((w1_parts, w1b_ref), (w3_parts, w3b_ref), (w2_parts, w2b_ref)):
            rows = dst.shape[0] // WEIGHT_SPLIT
            for k, part in enumerate(parts):
                dst[k * rows:(k + 1) * rows, :] = part[...].astype(BF16)

        def swiglu_rows(start, size):
            rows = pl.ds(pl.multiple_of(start, ROUTE_TILE), size)
            xr = x_ref[rows, :]
            mid = _silu(_dot(xr, w1b_ref[...])) * _dot(xr, w3b_ref[...])
            o_ref[rows, :] += _dot(mid.astype(BF16), w2b_ref[...])

        n_sub = sub_ref[s]
        n_pair = n_sub // 2

        def pair(r, carry):
            swiglu_rows(r * (2 * ROUTE_TILE), 2 * ROUTE_TILE)
            return carry

        lax.fori_loop(0, n_pair, pair, 0)

        @pl.when(n_sub % 2 == 1)
        def _():
            swiglu_rows(n_pair * (2 * ROUTE_TILE), ROUTE_TILE)


def _experts(xs, super_expert, super_sub, w1, w3, w2):
    n_slots, d = xs.shape
    f = w1.shape[2]
    tf = 256
    nf = f // tf
    n_super = n_slots // EXPERT_ROWS
    ns = WEIGHT_SPLIT

    def f_eff(s, j, sub):
        return jnp.where(sub[s] > 0, j, nf - 1)

    up_specs = [pl.BlockSpec((None, d // ns, tf), lambda s, j, e, sub, k=k: (e[s], k, f_eff(s, j, sub)))
                for k in range(ns)]
    down_specs = [pl.BlockSpec((None, tf // ns, d), lambda s, j, e, sub, k=k: (e[s], f_eff(s, j, sub) * ns + k, 0))
                  for k in range(ns)]
    row_spec = pl.BlockSpec((EXPERT_ROWS, d), lambda s, j, e, sub: (s, 0), pipeline_mode=pl.Buffered(1))
    return pl.pallas_call(
        _expert_kernel,
        grid_spec=pltpu.PrefetchScalarGridSpec(
            num_scalar_prefetch=2,
            grid=(n_super, nf),
            in_specs=[row_spec] + up_specs + up_specs + down_specs,
            out_specs=row_spec,
            scratch_shapes=[pltpu.VMEM((d, tf), BF16), pltpu.VMEM((d, tf), BF16), pltpu.VMEM((tf, d), BF16)]),
        out_shape=jax.ShapeDtypeStruct((n_slots, d), F32),
        compiler_params=_cparams("arbitrary", "arbitrary"),
        name="moe_experts",
    )(super_expert, super_sub, xs, *([w1] * ns), *([w3] * ns), *([w2] * ns))


def _combine_kernel(slot_ref, ys_hbm, h_ref, gate_ref, gf_ref, fg_ref, o_ref, buf_ref, sem):
    i = pl.program_id(0)
    n = pl.num_programs(0)

    def request(tile):
        slot = tile % 2

        def start(r, carry):
            for k in range(2):
                src = slot_ref[2 * (tile * ROUTE_TILE + r) + k]
                _row_copy(ys_hbm, buf_ref.at[slot, k], src, r, sem.at[slot]).start()
            return carry

        lax.fori_loop(0, ROUTE_TILE, start, 0, unroll=8)

    @pl.when(i == 0)
    def _():
        request(0)

    @pl.when(i + 1 < n)
    def _():
        request(jnp.minimum(i + 1, n - 1))

    slot = i % 2

    for k in range(2):
        pltpu.make_async_copy(ys_hbm.at[pl.ds(0, ROUTE_TILE), :], buf_ref.at[slot, k], sem.at[slot]).wait()
    gates = gate_ref[...]
    y = gates[:, 0:1] * buf_ref[slot, 0] + gates[:, 1:2] * buf_ref[slot, 1]
    x = h_ref[...] + gf_ref[...] * y
    o_ref[...] = x * lax.rsqrt(jnp.mean(x * x, axis=-1, keepdims=True) + EPS) * fg_ref[...]


def _combine(ys, slot, h, gates, mod, final_g, row_fn):
    m, d = h.shape
    return pl.pallas_call(
        _combine_kernel,
        grid_spec=pltpu.PrefetchScalarGridSpec(
            num_scalar_prefetch=1,
            grid=(m // ROUTE_TILE,),
            in_specs=[pl.BlockSpec(memory_space=pl.ANY),
                      pl.BlockSpec((ROUTE_TILE, d), lambda i, *_: (i, 0)),
                      pl.BlockSpec((ROUTE_TILE, LANES), lambda i, *_: (i, 0)),
                      _mod_spec(d, GF, row_fn),
                      pl.BlockSpec((1, d), lambda i, *_: (0, 0))],
            out_specs=pl.BlockSpec((ROUTE_TILE, d), lambda i, *_: (i, 0)),
            scratch_shapes=[pltpu.VMEM((2, 2, ROUTE_TILE, d), F32), pltpu.SemaphoreType.DMA((2,))]),
        out_shape=jax.ShapeDtypeStruct((m, d), F32),
        compiler_params=_cparams("arbitrary"),
        name="moe_combine_final_norm",
    )(slot, ys, h, gates, mod, final_g.reshape(1, d))


def _grid_pos_embed(rows, dim):
    r = np.repeat(np.arange(rows, dtype=np.float32), GRID_W)
    col = np.tile(np.arange(GRID_W, dtype=np.float32), rows)
    quarter = dim // 4
    freq = np.exp(np.float32(-math.log(10000.0)) * np.arange(quarter, dtype=np.float32) / np.float32(quarter))
    ar = r[:, None] * freq
    ac = col[:, None] * freq
    return np.concatenate([np.sin(ar), np.cos(ar), np.sin(ac), np.cos(ac)], axis=-1).astype(np.float32)


def _reorder_w_in(w):
    o_ab = QKV_W
    o_z = o_ab + AB_COLS
    o_sg = o_z + GROUP_W
    o_pool = o_sg + 2 * GROUP_W
    o_cv = o_pool + GROUP_W
    pad = jnp.zeros((w.shape[0], LANES - AB_COLS), w.dtype)
    return jnp.concatenate([w[:, :QKV_W], w[:, o_z:o_sg], w[:, o_sg:o_pool], w[:, o_cv:o_cv + 2 * GROUP_W],
                            w[:, o_pool:o_cv], w[:, o_ab:o_z], pad], axis=1).astype(BF16)


def kernel(x, c, ctx, c_ctx, ada_w, ada_b, norm_mix_g, w_in, dn_conv_w, dn_a_log, dn_dt_bias, dn_norm_g,
           sg_ln_g, sg_ln_b, sg_w, sg_b, pool_w, pool_scale, cv_w, cv_b, cv_ln_g, cv_ln_b, w_out,
           norm_ffn_g, ffn_w1, ffn_w3, ffn_w2, router_w, router_b, moe_w1, moe_w3, moe_w2, final_norm_g):
    batch, seq, d = x.shape
    ctx_len = ctx.shape[1]
    depth = ada_w.shape[0]
    assert depth == 2 and d == 4 * GROUP_W and batch + 1 <= SUBLANES
    assert (2 * batch * seq) % EXPERT_ROWS == 0 and ctx_len % SEQ_TILE == 0 and seq % SEQ_TILE == 0
    ctx_rows, x_rows = batch * ctx_len, batch * seq
    n_ctx_tiles, n_x_tiles = ctx_rows // SEQ_TILE, x_rows // SEQ_TILE
    seg = (n_ctx_tiles, ctx_len // SEQ_TILE, seq // SEQ_TILE)
    row_fn_for = lambda tile: functools.partial(_mod_row, tile=tile, ctx_rows=ctx_rows, seq=seq, batch=batch)
    row_fn = row_fn_for(SEQ_TILE)

    pos = _grid_pos_embed(seq // GRID_W, d)
    h = jnp.concatenate([ctx.reshape(ctx_rows, d), (x + pos[None]).reshape(x_rows, d)], axis=0)
    cc = jnp.zeros((SUBLANES, d), F32).at[:batch].set(c).at[batch].set(c_ctx)
    mod = _ada(cc, ada_w, ada_b).reshape(depth, SUBLANES, 1, 6 * d)

    def mixing(l, h, tile0, n_tiles):
        p = _proj(h, norm_mix_g[l], mod[l], _reorder_w_in(w_in[l]), row_fn)
        local = _dn_local(p, dn_conv_w[l], dn_a_log[l], dn_dt_bias[l], seg)
        o_f, o_b = _dn_scan(local, batch=batch, ctx_len=ctx_len, seq=seq)
        params = (dn_norm_g[l], sg_ln_g[l], sg_ln_b[l], sg_w[l], sg_b[l], pool_w[l], pool_scale[l],
                  cv_w[l], cv_b[l], cv_ln_g[l], cv_ln_b[l])
        return _mix(h, p, o_f, o_b, mod[l], params, w_out[l].astype(BF16),
                    seg=seg, row_fn=row_fn, tile0=tile0, n_tiles=n_tiles)

    h = mixing(0, h, 0, n_ctx_tiles + n_x_tiles)
    h = _ffn(h, norm_ffn_g[0], mod[0], ffn_w1[0].astype(BF16), ffn_w3[0].astype(BF16), ffn_w2[0].astype(BF16),
             row_fn_for, math.gcd(512, ctx_rows, seq))

    hx = mixing(1, h, n_ctx_tiles, n_x_tiles)
    x_row_fn = lambda i: i * ROUTE_TILE // seq
    hn, route, gates, totals = _router(hx, norm_ffn_g[1], mod[1], router_w[0], router_b[0], x_row_fn, 0)
    n_slots = 2 * x_rows + N_EXPERTS * EXPERT_ROWS
    slot, super_expert, super_sub, tile_valid = _routing_tables(route, totals, n_slots)
    xs = _gather_rows(hn, slot, tile_valid, n_slots)
    ys = _experts(xs, super_expert, super_sub, moe_w1[0], moe_w3[0], moe_w2[0])
    out = _combine(ys, slot, hx, gates, mod[1], final_norm_g, x_row_fn)
    return out.reshape(batch, seq, d)
```

```python
import functools
import math

import jax
import jax.numpy as jnp
import numpy as np
from jax import lax
from jax.experimental import pallas as pl
from jax.experimental.pallas import tpu as pltpu

F32 = jnp.float32
BF16 = jnp.bfloat16
EPS = 1e-6

GRID_W = 64
N_HEADS = 4
HEAD_DIM = 128
GROUP_W = N_HEADS * HEAD_DIM
DN_CHUNK = 64
SEQ_TILE = 256
CHUNKS_PER_TILE = SEQ_TILE // DN_CHUNK
DN_BLOCK = 128
BLOCKS_PER_TILE = SEQ_TILE // DN_BLOCK
SG_CHUNK = 128
POOL_WINDOWS = (2, 4, 8, 16)
SHORT_CONV = 5
CONF_CONV = 31
N_EXPERTS = 8
LANES = 128
SUBLANES = 8
VMEM_LIMIT = 56 * 1024 * 1024

QKV_OFF, QKV_W = 0, 3 * GROUP_W
Z_OFF = QKV_OFF + QKV_W
SG_OFF = Z_OFF + GROUP_W
CV_OFF = SG_OFF + 2 * GROUP_W
POOL_OFF = CV_OFF + 2 * GROUP_W
AB_OFF = POOL_OFF + GROUP_W
P_WIDTH = AB_OFF + LANES
AB_COLS = 4 * N_HEADS

SHM, SCM, GM, SHF, SCF, GF = range(6)


def _dot(a, b):
    return jnp.dot(a, b, preferred_element_type=F32)


def _dot_nt(a, b):
    return lax.dot_general(a, b, (((1,), (1,)), ((), ())), preferred_element_type=F32)


def _dot_tn(a, b):
    return lax.dot_general(a, b, (((0,), (0,)), ((), ())), preferred_element_type=F32)


def _sigmoid(x):
    return 1.0 / (1.0 + jnp.exp(-x))


def _silu(x):
    return x * _sigmoid(x)


def _cparams(*sem):
    return pltpu.CompilerParams(dimension_semantics=sem, vmem_limit_bytes=VMEM_LIMIT)


def _mod_row(i, *, tile, ctx_rows, seq, batch):
    start = i * tile
    return jnp.where(start < ctx_rows, batch, (start - ctx_rows) // seq)


def _mod_spec(d, chunk, row_fn):
    return pl.BlockSpec((None, 1, d), lambda i, *_: (row_fn(i), 0, chunk))


def _norm_modulate(x, g, scale, shift):
    y = x * lax.rsqrt(jnp.mean(x * x, axis=-1, keepdims=True) + EPS) * g
    return y * (1.0 + scale) + shift


def _ada_kernel(c_ref, w_ref, b_ref, o_ref):
    s = _silu(c_ref[...]).astype(BF16)
    o_ref[...] = _dot(s, w_ref[...].astype(BF16)) + b_ref[...]


def _ada(cc, ada_w, ada_b):
    nl, d, n = ada_w.shape
    tn = 1024
    return pl.pallas_call(
        _ada_kernel,
        grid=(nl, n // tn),
        in_specs=[pl.BlockSpec((SUBLANES, d), lambda l, j: (0, 0)),
                  pl.BlockSpec((None, d, tn), lambda l, j: (l, 0, j)),
                  pl.BlockSpec((None, 1, tn), lambda l, j: (l, 0, j))],
        out_specs=pl.BlockSpec((None, SUBLANES, tn), lambda l, j: (l, 0, j)),
        out_shape=jax.ShapeDtypeStruct((nl, SUBLANES, n), F32),
        compiler_params=_cparams("parallel", "parallel"),
        name="ada_modulation",
    )(cc, ada_w, ada_b.reshape(nl, 1, n))


def _proj_kernel(h_ref, g_ref, sh_ref, sc_ref, w_ref, o_ref):
    y = _norm_modulate(h_ref[...], g_ref[...], sc_ref[...], sh_ref[...])
    o_ref[...] = _dot(y.astype(BF16), w_ref[...])


def _proj(h, g, mod, w, row_fn):
    m, d = h.shape
    n = w.shape[1]
    tm = SEQ_TILE
    return pl.pallas_call(
        _proj_kernel,
        grid=(m // tm,),
        in_specs=[pl.BlockSpec((tm, d), lambda i: (i, 0)),
                  pl.BlockSpec((1, d), lambda i: (0, 0)),
                  _mod_spec(d, SHM, row_fn),
                  _mod_spec(d, SCM, row_fn),
                  pl.BlockSpec((d, n), lambda i: (0, 0), pipeline_mode=pl.Buffered(1))],
        out_specs=pl.BlockSpec((tm, n), lambda i: (i, 0)),
        out_shape=jax.ShapeDtypeStruct((m, n), F32),
        compiler_params=_cparams("parallel"),
        name="in_proj",
    )(h, g.reshape(1, d), mod, mod, w)


def _segment_pos(i, n_ctx_tiles, ctx_seg_tiles, x_seg_tiles):
    in_ctx = i < n_ctx_tiles
    tpos = jnp.where(in_ctx, i % ctx_seg_tiles, (i - n_ctx_tiles) % x_seg_tiles)
    tlen = jnp.where(in_ctx, ctx_seg_tiles, x_seg_tiles)
    return tpos, tlen


def _fill_ext(ext_ref, prev, cur, nxt, first, last, halo):
    ext_ref[0:halo, :] = jnp.where(first, 0.0, prev)
    ext_ref[halo:halo + SEQ_TILE, :] = cur
    ext_ref[halo + SEQ_TILE:2 * halo + SEQ_TILE, :] = jnp.where(last, 0.0, nxt)


PK_W, PK_QG, PK_KD, PK_QK = 0, GROUP_W, 2 * GROUP_W, 3 * GROUP_W
PK_WIDTH = 4 * GROUP_W


def _dn_local_kernel(qkv_ref, prev_ref, next_ref, ab_ref, cw_ref, alog_ref, dtb_ref,
                     uf_ref, pkf_ref, ub_ref, pkb_ref, gam_ref,
                     ext_ref, t_ref, l_ref, rhs_ref, *, seg):
    tpos, tlen = _segment_pos(pl.program_id(0), *seg)
    halo = SUBLANES
    _fill_ext(ext_ref, prev_ref[...], qkv_ref[...], next_ref[...], tpos == 0, tpos == tlen - 1, halo)

    pad = SHORT_CONV // 2
    acc = cw_ref[0:1, :] * ext_ref[halo - pad:halo - pad + SEQ_TILE, :]
    for j in range(1, SHORT_CONV):
        acc = acc + cw_ref[j:j + 1, :] * ext_ref[halo - pad + j:halo - pad + j + SEQ_TILE, :]
    s = _silu(acc)

    ab = ab_ref[...]
    lane = lax.broadcasted_iota(jnp.int32, (SEQ_TILE, LANES), 1)
    xs = ab + dtb_ref[...]
    softplus = jnp.maximum(xs, 0.0) + jnp.log1p(jnp.exp(-jnp.abs(xs)))
    la = jnp.where(lane < 2 * N_HEADS, -jnp.exp(alog_ref[...]) * softplus, 0.0)
    beta = _sigmoid(ab)

    def chunk_masks(n):
        rr = lax.broadcasted_iota(jnp.int32, (n, n), 0)
        cc = lax.broadcasted_iota(jnp.int32, (n, n), 1)
        same = (rr // DN_CHUNK) == (cc // DN_CHUNK)
        return rr, cc, (same & (cc <= rr), same & (cc >= rr)), (same & (cc < rr), same & (cc > rr))

    _, _, incl_tile, _ = chunk_masks(SEQ_TILE)
    r, c, incl, strict = chunk_masks(DN_BLOCK)

    def off_block(s, d):
        hi, lo = (r, c) if d == 0 else (c, r)
        return ((hi // s) % 2 == 1) & ((lo // s) == (hi // s) - 1)

    la_hi = la.astype(BF16)
    rem = la - la_hi.astype(F32)
    la_mid = rem.astype(BF16)
    la_lo = (rem - la_mid.astype(F32)).astype(BF16)

    def chunk_sum(mask):
        m = jnp.where(mask, 1.0, 0.0).astype(BF16)
        return _dot(m, la_hi) + _dot(m, la_mid) + _dot(m, la_lo)

    g_fwd = chunk_sum(incl_tile[0])
    g_bwd = chunk_sum(incl_tile[1])
    g = jnp.where(lane < N_HEADS, g_fwd, g_bwd)
    g_tot = g_fwd + g_bwd - la
    g_t = g.T

    outs = ((uf_ref, pkf_ref), (ub_ref, pkb_ref))
    eye = jnp.where(r == c, 1.0, 0.0)
    for h in range(N_HEADS):
        sl = slice(h * HEAD_DIM, (h + 1) * HEAD_DIM)
        qh = s[:, h * HEAD_DIM:(h + 1) * HEAD_DIM]
        kh = s[:, GROUP_W + h * HEAD_DIM:GROUP_W + (h + 1) * HEAD_DIM]
        vh = s[:, 2 * GROUP_W + h * HEAD_DIM:2 * GROUP_W + (h + 1) * HEAD_DIM]
        qh = qh * lax.rsqrt(jnp.sum(qh * qh, axis=-1, keepdims=True) + EPS) * (HEAD_DIM ** -0.5)
        kh = kh * lax.rsqrt(jnp.sum(kh * kh, axis=-1, keepdims=True) + EPS)
        qh16 = qh.astype(BF16)
        kh16 = kh.astype(BF16)
        for d in range(2):
            gi = d * N_HEADS + h
            u_ref, pk_ref = outs[d]
            gcol = g[:, gi:gi + 1]
            bcol = beta[:, 2 * N_HEADS + gi:2 * N_HEADS + gi + 1]
            eg = jnp.exp(gcol)
            kb = kh * bcol
            rhs_ref[gi, :, :HEAD_DIM] = vh * bcol
            rhs_ref[gi, :, HEAD_DIM:] = kb * eg
            gl = g_tot[:, gi:gi + 1]
            pk_ref[:, PK_QG + h * HEAD_DIM:PK_QG + (h + 1) * HEAD_DIM] = (qh * eg).astype(BF16)
            pk_ref[:, PK_KD + h * HEAD_DIM:PK_KD + (h + 1) * HEAD_DIM] = (kh * jnp.exp(gl - gcol)).astype(BF16)
            pk_ref[:, PK_QK + h * HEAD_DIM + DN_CHUNK:PK_QK + (h + 1) * HEAD_DIM] = jnp.zeros(
                (SEQ_TILE, HEAD_DIM - DN_CHUNK), BF16)
            for cb in range(CHUNKS_PER_TILE):
                gam_ref[cb, :, gi * LANES:(gi + 1) * LANES] = jnp.broadcast_to(
                    jnp.exp(gl[cb * DN_CHUNK:cb * DN_CHUNK + 1, :]), (1, LANES))
            for blk in range(BLOCKS_PER_TILE):
                rows = slice(blk * DN_BLOCK, (blk + 1) * DN_BLOCK)
                unit = gi * BLOCKS_PER_TILE + blk
                grow = g_t[gi:gi + 1, rows]
                decay = jnp.where(incl[d], jnp.exp(jnp.where(incl[d], gcol[rows] - grow, 0.0)), 0.0)
                lmat = jnp.where(strict[d], bcol[rows] * _dot_nt(kh16[rows], kh16[rows]) * decay, 0.0)
                l_ref[unit] = lmat
                t_ref[unit] = eye - jnp.where(off_block(1, d), lmat, 0.0)
                qkm = jnp.where(incl[d], _dot_nt(qh16[rows], kh16[rows]) * decay, 0.0)
                qk = qkm[:, 0:DN_CHUNK]
                for cb in range(1, DN_BLOCK // DN_CHUNK):
                    qk = qk + qkm[:, cb * DN_CHUNK:(cb + 1) * DN_CHUNK]
                pk_ref[rows, PK_QK + h * HEAD_DIM:PK_QK + h * HEAD_DIM + DN_CHUNK] = qk.astype(BF16)

    n_chain = 2 * N_HEADS
    n_unit = n_chain * BLOCKS_PER_TILE
    unit_dir = lambda unit: unit // (N_HEADS * BLOCKS_PER_TILE)
    s_blk = 2
    while s_blk < DN_CHUNK:
        lt = []
        for unit in range(n_unit):
            l_s = jnp.where(off_block(s_blk, unit_dir(unit)), l_ref[unit], 0.0).astype(BF16)
            lt.append(_dot(l_s, t_ref[unit].astype(BF16)).astype(BF16))
        for unit in range(n_unit):
            t_inv = t_ref[unit]
            t_ref[unit] = t_inv - _dot(t_inv.astype(BF16), lt[unit])
        s_blk *= 2
    for unit in range(n_unit):
        gi, blk = divmod(unit, BLOCKS_PER_TILE)
        d, h = divmod(gi, N_HEADS)
        rows = slice(blk * DN_BLOCK, (blk + 1) * DN_BLOCK)
        u_ref, pk_ref = outs[d]
        rhs = rhs_ref[gi, rows, :]
        sol = rhs + _dot(jnp.where(strict[d], t_ref[unit], 0.0).astype(BF16), rhs.astype(BF16))
        u_ref[rows, h * HEAD_DIM:(h + 1) * HEAD_DIM] = sol[:, :HEAD_DIM]
        pk_ref[rows, PK_W + h * HEAD_DIM:PK_W + (h + 1) * HEAD_DIM] = sol[:, HEAD_DIM:].astype(BF16)


def _batch_major_tile(i, n_ctx_tiles, ctx_seg_tiles, x_seg_tiles):
    per_batch = ctx_seg_tiles + x_seg_tiles
    j = i - n_ctx_tiles
    return jnp.where(i < n_ctx_tiles,
                     (i // ctx_seg_tiles) * per_batch + i % ctx_seg_tiles,
                     (j // x_seg_tiles) * per_batch + ctx_seg_tiles + j % x_seg_tiles)


def _dn_local(p, conv_w, a_log, dt_bias, seg):
    rows = p.shape[0]
    nt = rows // SEQ_TILE
    hb = SEQ_TILE // SUBLANES
    last_hb = rows // SUBLANES - 1
    cw = jnp.zeros((SUBLANES, QKV_W), F32).at[:SHORT_CONV].set(conv_w)
    pad_row = lambda v: jnp.zeros((1, LANES), F32).at[0, :2 * N_HEADS].set(v.reshape(-1))
    u_out = jax.ShapeDtypeStruct((rows, GROUP_W), F32)
    pk_out = jax.ShapeDtypeStruct((rows, PK_WIDTH), BF16)
    gam_out = jax.ShapeDtypeStruct((rows // DN_CHUNK, 1, 2 * N_HEADS * LANES), F32)
    dst = lambda i: _batch_major_tile(i, *seg)
    u_spec = pl.BlockSpec((SEQ_TILE, GROUP_W), lambda i: (dst(i), 0))
    pk_spec = pl.BlockSpec((SEQ_TILE, PK_WIDTH), lambda i: (dst(i), 0))
    full = lambda shape: pl.BlockSpec(shape, lambda i: (0,) * len(shape))
    n_chain = 2 * N_HEADS
    return pl.pallas_call(
        functools.partial(_dn_local_kernel, seg=seg),
        grid=(nt,),
        in_specs=[pl.BlockSpec((SEQ_TILE, QKV_W), lambda i: (i, QKV_OFF // QKV_W)),
                  pl.BlockSpec((SUBLANES, QKV_W), lambda i: (jnp.maximum(i * hb - 1, 0), 0)),
                  pl.BlockSpec((SUBLANES, QKV_W), lambda i: (jnp.minimum((i + 1) * hb, last_hb), 0)),
                  pl.BlockSpec((SEQ_TILE, LANES), lambda i: (i, AB_OFF // LANES)),
                  full((SUBLANES, QKV_W)), full((1, LANES)), full((1, LANES))],
        out_specs=[u_spec, pk_spec, u_spec, pk_spec,
                   pl.BlockSpec((CHUNKS_PER_TILE, 1, 2 * N_HEADS * LANES), lambda i: (dst(i), 0, 0))],
        out_shape=[u_out, pk_out, u_out, pk_out, gam_out],
        scratch_shapes=[pltpu.VMEM((SEQ_TILE + 2 * SUBLANES, QKV_W), F32),
                        pltpu.VMEM((n_chain * BLOCKS_PER_TILE, DN_BLOCK, DN_BLOCK), F32),
                        pltpu.VMEM((n_chain * BLOCKS_PER_TILE, DN_BLOCK, DN_BLOCK), F32),
                        pltpu.VMEM((n_chain, SEQ_TILE, 2 * HEAD_DIM), F32)],
        compiler_params=_cparams("parallel"),
        name="dn_local",
    )(p, p, p, p, cw, pad_row(a_log), pad_row(dt_bias))


def _dn_scan_kernel(uf, pkf, gamf, ub, pkb, gamb, of_ref, ob_ref, s_ref, *, batch):
    @pl.when(pl.program_id(0) == 0)
    def _():
        s_ref[...] = jnp.zeros_like(s_ref)

    dirs = ((uf, pkf, gamf, of_ref), (ub, pkb, gamb, ob_ref))
    chains = [(b, d, h) for b in range(batch) for d in range(2) for h in range(N_HEADS)]
    head = lambda off, h: slice(off + h * HEAD_DIM, off + (h + 1) * HEAD_DIM)

    ws = []
    for ci, (b, d, h) in enumerate(chains):
        pk = dirs[d][1]
        lhs = jnp.concatenate([pk[b, :, head(PK_W, h)], pk[b, :, head(PK_QG, h)]], axis=0)
        ws.append(_dot(lhs, s_ref[ci].astype(BF16)))
    v16 = []
    for ci, (b, d, h) in enumerate(chains):
        v16.append((dirs[d][0][b, :, head(0, h)] - ws[ci][:DN_CHUNK]).astype(BF16))
    for ci, (b, d, h) in enumerate(chains):
        pk, o_ref = dirs[d][1], dirs[d][3]
        qk = pk[b, :, PK_QK + h * HEAD_DIM:PK_QK + h * HEAD_DIM + DN_CHUNK]
        o_ref[b, :, head(0, h)] = ws[ci][DN_CHUNK:] + _dot(qk, v16[ci])
    for ci, (b, d, h) in enumerate(chains):
        pk, gam = dirs[d][1], dirs[d][2]
        gi = d * N_HEADS + h
        s_ref[ci] = s_ref[ci] * gam[b, :, gi * LANES:(gi + 1) * LANES] + _dot_tn(pk[b, :, head(PK_KD, h)], v16[ci])


def _dn_scan(local, *, batch, ctx_len, seq):
    uf, pkf, ub, pkb, gam = local
    cc, xc = ctx_len // DN_CHUNK, seq // DN_CHUNK
    nc = cc + xc
    bwd = lambda s: jnp.where(s < cc, cc - 1 - s, cc + (xc - 1 - (s - cc)))
    by_chunk = lambda a: a.reshape(batch, nc, *a.shape[1:]) if a.ndim == 3 else a.reshape(batch, nc, DN_CHUNK, a.shape[1])

    def specs(idx):
        return [pl.BlockSpec((batch, None, DN_CHUNK, GROUP_W), lambda s: (0, idx(s), 0, 0)),
                pl.BlockSpec((batch, None, DN_CHUNK, PK_WIDTH), lambda s: (0, idx(s), 0, 0)),
                pl.BlockSpec((batch, None, 1, 2 * N_HEADS * LANES), lambda s: (0, idx(s), 0, 0))]

    out = jax.ShapeDtypeStruct((batch, nc, DN_CHUNK, GROUP_W), F32)
    fwd = lambda s: s
    o_f, o_b = pl.pallas_call(
        functools.partial(_dn_scan_kernel, batch=batch),
        grid=(nc,),
        in_specs=specs(fwd) + specs(bwd),
        out_specs=[pl.BlockSpec((batch, None, DN_CHUNK, GROUP_W), lambda s: (0, s, 0, 0)),
                   pl.BlockSpec((batch, None, DN_CHUNK, GROUP_W), lambda s: (0, bwd(s), 0, 0))],
        out_shape=[out, out],
        scratch_shapes=[pltpu.VMEM((batch * 2 * N_HEADS, HEAD_DIM, HEAD_DIM), F32)],
        compiler_params=_cparams("arbitrary"),
        name="dn_scan",
    )(by_chunk(uf), by_chunk(pkf), by_chunk(gam), by_chunk(ub), by_chunk(pkb), by_chunk(gam))
    return o_f.reshape(-1, GROUP_W), o_b.reshape(-1, GROUP_W)


def _layernorm(x, g, b):
    mu = jnp.mean(x, axis=-1, keepdims=True)
    xc = x - mu
    var = jnp.mean(xc * xc, axis=-1, keepdims=True)
    return xc * lax.rsqrt(var + EPS) * g + b


def _mix_kernel(h_ref, of_ref, ob_ref, z_ref, sg_ref, cv_ref, cvp_ref, cvn_ref, pool_ref, poolp_ref, pooln_ref,
                gm_ref, dng_ref, sglg_ref, sglb_ref, sgw_ref, sgbt_ref, pw_ref, ps_ref,
                cvw_ref, cvb_ref, cvlg_ref, cvlb_ref, wout_ref,
                o_ref, y_ref, pext_ref, cext_ref, cacc_ref, *, seg, tile0):
    tpos, tlen = _segment_pos(pl.program_id(0) + tile0, *seg)
    first, last = tpos == 0, tpos == tlen - 1

    o = of_ref[...] + ob_ref[...]
    z = z_ref[...]
    for h in range(N_HEADS):
        sl = slice(h * HEAD_DIM, (h + 1) * HEAD_DIM)
        oh = o[:, sl]
        yh = oh * lax.rsqrt(jnp.mean(oh * oh, axis=-1, keepdims=True) + EPS) * dng_ref[...] * _silu(z[:, sl])
        y_ref[:, sl] = yh.astype(BF16)

    psg = sg_ref[...]
    psg = 0.5 * psg * (1.0 + jnp.tanh(math.sqrt(2.0 / math.pi) * (psg + 0.044715 * (psg * psg * psg))))
    u = psg[:, :GROUP_W]
    v = _layernorm(psg[:, GROUP_W:], sglg_ref[...], sglb_ref[...]).astype(BF16)
    for n in range(SEQ_TILE // SG_CHUNK):
        rs = slice(n * SG_CHUNK, (n + 1) * SG_CHUNK)
        for h in range(N_HEADS):
            sl = slice(h * HEAD_DIM, (h + 1) * HEAD_DIM)
            sv = _dot(sgw_ref[h], v[rs, sl]) + sgbt_ref[:, h:h + 1]
            y_ref[rs, GROUP_W + h * HEAD_DIM:GROUP_W + (h + 1) * HEAD_DIM] = (u[rs, sl] * sv).astype(BF16)

    halo = SUBLANES
    _fill_ext(pext_ref, poolp_ref[...], pool_ref[...], pooln_ref[...], first, last, halo)
    t = tpos * SEQ_TILE + lax.broadcasted_iota(jnp.int32, (SEQ_TILE, 1), 0)
    seg_len = tlen * SEQ_TILE
    for gi, win in enumerate(POOL_WINDOWS):
        sl = slice(gi * LANES, (gi + 1) * LANES)
        tot = pext_ref[halo - win // 2:halo - win // 2 + SEQ_TILE, sl]
        for m in range(1 - win // 2, win // 2):
            tot = tot + pext_ref[halo + m:halo + m + SEQ_TILE, sl]
        cnt = jnp.clip(t + win // 2, 0, seg_len) - jnp.clip(t - win // 2, 0, seg_len)
        yg = tot / cnt.astype(F32) - pool_ref[:, sl]
        yg = _dot(yg.astype(BF16), pw_ref[gi]) * ps_ref[:, sl]
        y_ref[:, 2 * GROUP_W + gi * LANES:2 * GROUP_W + (gi + 1) * LANES] = yg.astype(BF16)

    halo = 2 * SUBLANES
    glu = lambda p: p[:, :GROUP_W] * _sigmoid(p[:, GROUP_W:])
    _fill_ext(cext_ref, glu(cvp_ref[...]), glu(cv_ref[...]), glu(cvn_ref[...]), first, last, halo)
    pad = CONF_CONV // 2
    span = SEQ_TILE + SUBLANES
    acc = None
    for rho in range(SUBLANES):
        part = None
        for j in range(CONF_CONV):
            off = halo - pad + j
            if off % SUBLANES == rho:
                term = cvw_ref[j:j + 1, :] * cext_ref[off - rho:off - rho + span, :]
                part = term if part is None else part + term
        if part is not None:
            cacc_ref[...] = part
            shifted = cacc_ref[rho:rho + SEQ_TILE, :]
            acc = shifted if acc is None else acc + shifted
    yc = _silu(_layernorm(acc + cvb_ref[...], cvlg_ref[...], cvlb_ref[...]))
    y_ref[:, 3 * GROUP_W:] = yc.astype(BF16)

    o_ref[...] = h_ref[...] + gm_ref[...] * _dot(y_ref[...], wout_ref[...])


def _mix(h, p, o_f, o_b, mod, params, wout, *, seg, row_fn, tile0, n_tiles):
    d = h.shape[1]
    (dng, sglg, sglb, sgw, sgb, pw, ps, cvw, cvb, cvlg, cvlb) = params
    rows = p.shape[0]
    row = lambda v: v.reshape(1, -1)
    cvw_p = jnp.zeros((4 * SUBLANES, GROUP_W), F32).at[:CONF_CONV].set(cvw)
    sgbt = jnp.zeros((SG_CHUNK, LANES), F32).at[:, :N_HEADS].set(sgb.T)

    def halo_specs(width, col, halo):
        hb = SEQ_TILE // halo
        last_hb = rows // halo - 1
        return [pl.BlockSpec((SEQ_TILE, width), lambda i: (i + tile0, col)),
                pl.BlockSpec((halo, width), lambda i: (jnp.maximum((i + tile0) * hb - 1, 0), col)),
                pl.BlockSpec((halo, width), lambda i: (jnp.minimum((i + tile0 + 1) * hb, last_hb), col))]

    full = lambda a: pl.BlockSpec(a.shape, lambda i: (0,) * a.ndim)
    consts = [row(dng), row(sglg), row(sglb), sgw.astype(BF16), sgbt, pw.astype(BF16), row(ps),
              cvw_p, row(cvb), row(cvlg), row(cvlb)]
    return pl.pallas_call(
        functools.partial(_mix_kernel, seg=seg, tile0=tile0),
        grid=(n_tiles,),
        in_specs=[pl.BlockSpec((SEQ_TILE, d), lambda i: (i + tile0, 0)),
                  pl.BlockSpec((SEQ_TILE, GROUP_W), lambda i: (_batch_major_tile(i + tile0, *seg), 0)),
                  pl.BlockSpec((SEQ_TILE, GROUP_W), lambda i: (_batch_major_tile(i + tile0, *seg), 0)),
                  pl.BlockSpec((SEQ_TILE, GROUP_W), lambda i: (i + tile0, Z_OFF // GROUP_W)),
                  pl.BlockSpec((SEQ_TILE, 2 * GROUP_W), lambda i: (i + tile0, SG_OFF // (2 * GROUP_W)))]
                 + halo_specs(2 * GROUP_W, CV_OFF // (2 * GROUP_W), 2 * SUBLANES)
                 + halo_specs(GROUP_W, POOL_OFF // GROUP_W, SUBLANES)
                 + [_mod_spec(d, GM, lambda i: row_fn(i + tile0))]
                 + [full(a) for a in consts]
                 + [pl.BlockSpec(wout.shape, lambda i: (0, 0), pipeline_mode=pl.Buffered(1))],
        out_specs=pl.BlockSpec((SEQ_TILE, d), lambda i: (i, 0)),
        out_shape=jax.ShapeDtypeStruct((n_tiles * SEQ_TILE, d), F32),
        scratch_shapes=[pltpu.VMEM((SEQ_TILE, d), BF16),
                        pltpu.VMEM((SEQ_TILE + 2 * SUBLANES, GROUP_W), F32),
                        pltpu.VMEM((SEQ_TILE + 4 * SUBLANES, GROUP_W), F32),
                        pltpu.VMEM((SEQ_TILE + SUBLANES, GROUP_W), F32)],
        compiler_params=_cparams("parallel"),
        name="mixers_out_proj",
    )(h, o_f, o_b, p, p, p, p, p, p, p, p, mod, *consts, wout)


def _ffn_kernel(h_ref, g_ref, sh_ref, sc_ref, gate_ref, w1_ref, w3_ref, w2_ref, o_ref, xn_ref, acc_ref):
    j = pl.program_id(1)

    @pl.when(j == 0)
    def _():
        xn_ref[...] = _norm_modulate(h_ref[...], g_ref[...], sc_ref[...], sh_ref[...]).astype(BF16)
        acc_ref[...] = jnp.zeros_like(acc_ref)

    xn = xn_ref[...]
    mid = _silu(_dot(xn, w1_ref[...])) * _dot(xn, w3_ref[...])
    acc_ref[...] += _dot(mid.astype(BF16), w2_ref[...])

    @pl.when(j == pl.num_programs(1) - 1)
    def _():
        o_ref[...] = h_ref[...] + gate_ref[...] * acc_ref[...]


def _ffn(h, g, mod, w1, w3, w2, row_fn_for, tm):
    m, d = h.shape
    f = w1.shape[1]
    tf = 512
    row_fn = row_fn_for(tm)
    return pl.pallas_call(
        _ffn_kernel,
        grid=(m // tm, f // tf),
        in_specs=[pl.BlockSpec((tm, d), lambda i, j: (i, 0)),
                  pl.BlockSpec((1, d), lambda i, j: (0, 0)),
                  _mod_spec(d, SHF, row_fn), _mod_spec(d, SCF, row_fn), _mod_spec(d, GF, row_fn),
                  pl.BlockSpec((d, tf), lambda i, j: (0, j)),
                  pl.BlockSpec((d, tf), lambda i, j: (0, j)),
                  pl.BlockSpec((tf, d), lambda i, j: (j, 0))],
        out_specs=pl.BlockSpec((tm, d), lambda i, j: (i, 0)),
        out_shape=jax.ShapeDtypeStruct((m, d), F32),
        scratch_shapes=[pltpu.VMEM((tm, d), BF16), pltpu.VMEM((tm, d), F32)],
        compiler_params=_cparams("parallel", "arbitrary"),
        name="dense_swiglu",
    )(h, g.reshape(1, d), mod, mod, mod, w1, w3, w2)


ROUTE_TILE = 256
EXPERT_ROWS = 2048


def _router_kernel(h_ref, g_ref, sh_ref, sc_ref, rw_ref, rb_ref, hn_ref, idx_ref, gate_ref, total_ref, count_ref):
    hn = _norm_modulate(h_ref[...], g_ref[...], sc_ref[...], sh_ref[...])
    hn_ref[...] = hn
    hn_hi = hn.astype(BF16)
    hn_lo = (hn - hn_hi.astype(F32)).astype(BF16)
    rw = rw_ref[...]
    rw_hi = rw.astype(BF16)
    rw_lo = (rw - rw_hi.astype(F32)).astype(BF16)
    logits = _dot(hn_hi, rw_hi) + (_dot(hn_hi, rw_lo) + _dot(hn_lo, rw_hi)) + rb_ref[...]
    lane = lax.broadcasted_iota(jnp.int32, logits.shape, 1)
    lane_f = lane.astype(F32)
    neg = -jnp.inf
    lg = jnp.where(lane < N_EXPERTS, logits, neg)
    m1 = jnp.max(lg, axis=-1, keepdims=True)
    i1 = jnp.min(jnp.where(lg == m1, lane_f, float(LANES)), axis=-1, keepdims=True)
    lg2 = jnp.where(lane_f == i1, neg, lg)
    m2 = jnp.max(lg2, axis=-1, keepdims=True)
    i2 = jnp.min(jnp.where(lg2 == m2, lane_f, float(LANES)), axis=-1, keepdims=True)
    e2 = jnp.exp(m2 - m1)
    g1 = 1.0 / (1.0 + e2)
    g2 = e2 / (1.0 + e2)
    gate_ref[...] = jnp.where(lane == 0, g1, jnp.where(lane == 1, g2, 0.0))

    @pl.when(pl.program_id(0) == 0)
    def _():
        count_ref[...] = jnp.zeros_like(count_ref)

    pick1 = jnp.where(lane_f == i1, 1.0, 0.0)
    pick2 = jnp.where(lane_f == i2, 1.0, 0.0)
    picked = pick1 + pick2
    r = lax.broadcasted_iota(jnp.int32, (ROUTE_TILE, ROUTE_TILE), 0)
    c = lax.broadcasted_iota(jnp.int32, (ROUTE_TILE, ROUTE_TILE), 1)
    earlier = jnp.where(c < r, 1.0, 0.0).astype(BF16)
    before = count_ref[0:1, :] + _dot(earlier, picked.astype(BF16))
    rank1 = jnp.sum(pick1 * before, axis=-1, keepdims=True)
    rank2 = jnp.sum(pick2 * before, axis=-1, keepdims=True)
    count_ref[...] = count_ref[...] + jnp.sum(picked, axis=0, keepdims=True)
    total_ref[...] = count_ref[...]
    idx_ref[...] = jnp.where(lane == 0, i1, jnp.where(lane == 1, i2, jnp.where(
        lane == 2, rank1, jnp.where(lane == 3, rank2, 0.0)))).astype(jnp.int32)


def _router(h, g, mod, rw, rb, row_fn, tile0):
    d = h.shape[1]
    tm = ROUTE_TILE
    n_tiles = h.shape[0] // tm - tile0
    m = n_tiles * tm
    rw_p = jnp.zeros((d, LANES), F32).at[:, :N_EXPERTS].set(rw)
    rb_p = jnp.zeros((1, LANES), F32).at[0, :N_EXPERTS].set(rb)
    shifted = lambda i: row_fn(i + tile0)
    return pl.pallas_call(
        _router_kernel,
        grid=(n_tiles,),
        in_specs=[pl.BlockSpec((tm, d), lambda i: (i + tile0, 0)),
                  pl.BlockSpec((1, d), lambda i: (0, 0)),
                  _mod_spec(d, SHF, shifted), _mod_spec(d, SCF, shifted),
                  pl.BlockSpec((d, LANES), lambda i: (0, 0)),
                  pl.BlockSpec((1, LANES), lambda i: (0, 0))],
        out_specs=[pl.BlockSpec((tm, d), lambda i: (i, 0)),
                   pl.BlockSpec((tm, LANES), lambda i: (i, 0)),
                   pl.BlockSpec((tm, LANES), lambda i: (i, 0)),
                   pl.BlockSpec((SUBLANES, LANES), lambda i: (0, 0))],
        out_shape=[jax.ShapeDtypeStruct((m, d), F32),
                   jax.ShapeDtypeStruct((m, LANES), jnp.int32),
                   jax.ShapeDtypeStruct((m, LANES), F32),
                   jax.ShapeDtypeStruct((SUBLANES, LANES), F32)],
        scratch_shapes=[pltpu.VMEM((SUBLANES, LANES), F32)],
        compiler_params=_cparams("arbitrary"),
        name="moe_router",
    )(h, g.reshape(1, d), mod, mod, rw_p, rb_p)


def _routing_tables(route, totals, n_slots):
    e_flat = route[:, 0:2].reshape(-1)
    rank = route[:, 2:4].reshape(-1)
    counts = totals[0, :N_EXPERTS].astype(jnp.int32)
    passes = (counts + EXPERT_ROWS - 1) // EXPERT_ROWS
    sub_tiles = (counts + ROUTE_TILE - 1) // ROUTE_TILE
    pass_rows = jnp.maximum((sub_tiles + jnp.maximum(passes, 1) - 1) // jnp.maximum(passes, 1), 1) * ROUTE_TILE
    padded = passes * EXPERT_ROWS
    ends = jnp.cumsum(padded)
    starts = ends - padded
    onehot = (e_flat[:, None] == jnp.arange(N_EXPERTS)[None, :]).astype(jnp.int32)
    rank_rows = jnp.sum(onehot * pass_rows[None, :], axis=1)
    rank_pass = rank // rank_rows
    slot = jnp.sum(onehot * starts[None, :], axis=1) + rank_pass * EXPERT_ROWS + (rank - rank_pass * rank_rows)
    slot_token = jnp.zeros((n_slots,), jnp.int32).at[slot].set(jnp.arange(slot.shape[0], dtype=jnp.int32) // 2)
    n_super = n_slots // EXPERT_ROWS
    super_start = jnp.arange(n_super, dtype=jnp.int32) * EXPERT_ROWS
    super_expert = jnp.minimum(jnp.sum(super_start[:, None] >= ends[None, :], axis=1), N_EXPERTS - 1).astype(jnp.int32)
    super_pass = (super_start - starts[super_expert]) // EXPERT_ROWS
    valid_rows = jnp.clip(counts[super_expert] - super_pass * pass_rows[super_expert], 0, pass_rows[super_expert])
    valid_rows = jnp.where(super_start < ends[-1], valid_rows, 0)
    super_sub = ((valid_rows + ROUTE_TILE - 1) // ROUTE_TILE).astype(jnp.int32)
    tile_valid = (jnp.arange(n_slots // ROUTE_TILE, dtype=jnp.int32) % (EXPERT_ROWS // ROUTE_TILE)
                  < jnp.repeat(super_sub, EXPERT_ROWS // ROUTE_TILE)).astype(jnp.int32)
    return slot.astype(jnp.int32), slot_token, super_expert, super_sub, tile_valid


def _row_copy(src_hbm, dst_vmem, src_row, dst_row, sem):
    return pltpu.make_async_copy(src_hbm.at[pl.ds(src_row, 1), :], dst_vmem.at[pl.ds(dst_row, 1), :], sem)


def _gather_kernel(tok_ref, valid_ref, hn_hbm, o_ref, buf_ref, sem):
    i = pl.program_id(0)
    n = pl.num_programs(0)

    def request(tile):
        slot = tile % 2

        def start(r, carry):
            _row_copy(hn_hbm, buf_ref.at[slot], tok_ref[tile * ROUTE_TILE + r], r, sem.at[slot]).start()
            return carry

        lax.fori_loop(0, ROUTE_TILE, start, 0, unroll=8)

    @pl.when((i == 0) & (valid_ref[0] != 0))
    def _():
        request(0)

    nxt = jnp.minimum(i + 1, n - 1)

    @pl.when((i + 1 < n) & (valid_ref[nxt] != 0))
    def _():
        request(nxt)

    @pl.when(valid_ref[i] == 0)
    def _():
        o_ref[...] = jnp.zeros_like(o_ref)

    @pl.when(valid_ref[i] != 0)
    def _():
        slot = i % 2

        pltpu.make_async_copy(hn_hbm.at[pl.ds(0, ROUTE_TILE), :], buf_ref.at[slot], sem.at[slot]).wait()
        o_ref[...] = buf_ref[slot].astype(BF16)


def _gather_rows(hn, slot_token, tile_valid):
    d = hn.shape[1]
    n_slots = slot_token.shape[0]
    return pl.pallas_call(
        _gather_kernel,
        grid_spec=pltpu.PrefetchScalarGridSpec(
            num_scalar_prefetch=2,
            grid=(n_slots // ROUTE_TILE,),
            in_specs=[pl.BlockSpec(memory_space=pl.ANY)],
            out_specs=pl.BlockSpec((ROUTE_TILE, d), lambda i, *_: (i, 0)),
            scratch_shapes=[pltpu.VMEM((2, ROUTE_TILE, d), F32), pltpu.SemaphoreType.DMA((2,))]),
        out_shape=jax.ShapeDtypeStruct((n_slots, d), BF16),
        compiler_params=_cparams("arbitrary"),
        name="moe_gather",
    )(slot_token, tile_valid, hn)


WEIGHT_SPLIT = 4


def _expert_kernel(exp_ref, sub_ref, x_ref, *refs):
    w1_parts = refs[:WEIGHT_SPLIT]
    w3_parts = refs[WEIGHT_SPLIT:2 * WEIGHT_SPLIT]
    w2_parts = refs[2 * WEIGHT_SPLIT:3 * WEIGHT_SPLIT]
    o_ref, w1b_ref, w3b_ref, w2b_ref = refs[3 * WEIGHT_SPLIT:]
    s, j = pl.program_id(0), pl.program_id(1)

    @pl.when(j == 0)
    def _():
        o_ref[...] = jnp.zeros_like(o_ref)

    @pl.when(sub_ref[s] > 0)
    def _():
        for parts, dst in ((w1_parts, w1b_ref), (w3_parts, w3b_ref), (w2_parts, w2b_ref)):
            rows = dst.shape[0] // WEIGHT_SPLIT
            for k, part in enumerate(parts):
                dst[k * rows:(k + 1) * rows, :] = part[...].astype(BF16)

        def swiglu_rows(start, size):
            rows = pl.ds(pl.multiple_of(start, ROUTE_TILE), size)
            xr = x_ref[rows, :]
            mid = _silu(_dot(xr, w1b_ref[...])) * _dot(xr, w3b_ref[...])
            o_ref[rows, :] += _dot(mid.astype(BF16), w2b_ref[...])

        n_sub = sub_ref[s]
        n_pair = n_sub // 2

        def pair(r, carry):
            swiglu_rows(r * (2 * ROUTE_TILE), 2 * ROUTE_TILE)
            return carry

        lax.fori_loop(0, n_pair, pair, 0)

        @pl.when(n_sub % 2 == 1)
        def _():
            swiglu_rows(n_pair * (2 * ROUTE_TILE), ROUTE_TILE)


def _experts(xs, super_expert, super_sub, w1, w3, w2):
    n_slots, d = xs.shape
    f = w1.shape[2]
    tf = 256
    nf = f // tf
    n_super = n_slots // EXPERT_ROWS
    ns = WEIGHT_SPLIT

    def f_eff(s, j, sub):
        return jnp.where(sub[s] > 0, j, nf - 1)

    up_specs = [pl.BlockSpec((None, d // ns, tf), lambda s, j, e, sub, k=k: (e[s], k, f_eff(s, j, sub)))
                for k in range(ns)]
    down_specs = [pl.BlockSpec((None, tf // ns, d), lambda s, j, e, sub, k=k: (e[s], f_eff(s, j, sub) * ns + k, 0))
                  for k in range(ns)]
    row_spec = pl.BlockSpec((EXPERT_ROWS, d), lambda s, j, e, sub: (s, 0), pipeline_mode=pl.Buffered(1))
    return pl.pallas_call(
        _expert_kernel,
        grid_spec=pltpu.PrefetchScalarGridSpec(
            num_scalar_prefetch=2,
            grid=(n_super, nf),
            in_specs=[row_spec] + up_specs + up_specs + down_specs,
            out_specs=row_spec,
            scratch_shapes=[pltpu.VMEM((d, tf), BF16), pltpu.VMEM((d, tf), BF16), pltpu.VMEM((tf, d), BF16)]),
        out_shape=jax.ShapeDtypeStruct((n_slots, d), F32),
        compiler_params=_cparams("arbitrary", "arbitrary"),
        name="moe_experts",
    )(super_expert, super_sub, xs, *([w1] * ns), *([w3] * ns), *([w2] * ns))


def _combine_kernel(slot_ref, ys_hbm, h_ref, gate_ref, gf_ref, fg_ref, o_ref, buf_ref, sem):
    i = pl.program_id(0)
    n = pl.num_programs(0)

    def request(tile):
        slot = tile % 2

        def start(r, carry):
            for k in range(2):
                src = slot_ref[2 * (tile * ROUTE_TILE + r) + k]
                _row_copy(ys_hbm, buf_ref.at[slot, k], src, r, sem.at[slot]).start()
            return carry

        lax.fori_loop(0, ROUTE_TILE, start, 0, unroll=8)

    @pl.when(i == 0)
    def _():
        request(0)

    @pl.when(i + 1 < n)
    def _():
        request(jnp.minimum(i + 1, n - 1))

    slot = i % 2

    for k in range(2):
        pltpu.make_async_copy(ys_hbm.at[pl.ds(0, ROUTE_TILE), :], buf_ref.at[slot, k], sem.at[slot]).wait()
    gates = gate_ref[...]
    y = gates[:, 0:1] * buf_ref[slot, 0] + gates[:, 1:2] * buf_ref[slot, 1]
    x = h_ref[...] + gf_ref[...] * y
    o_ref[...] = x * lax.rsqrt(jnp.mean(x * x, axis=-1, keepdims=True) + EPS) * fg_ref[...]


def _combine(ys, slot, h, gates, mod, final_g, row_fn):
    m, d = h.shape
    return pl.pallas_call(
        _combine_kernel,
        grid_spec=pltpu.PrefetchScalarGridSpec(
            num_scalar_prefetch=1,
            grid=(m // ROUTE_TILE,),
            in_specs=[pl.BlockSpec(memory_space=pl.ANY),
                      pl.BlockSpec((ROUTE_TILE, d), lambda i, *_: (i, 0)),
                      pl.BlockSpec((ROUTE_TILE, LANES), lambda i, *_: (i, 0)),
                      _mod_spec(d, GF, row_fn),
                      pl.BlockSpec((1, d), lambda i, *_: (0, 0))],
            out_specs=pl.BlockSpec((ROUTE_TILE, d), lambda i, *_: (i, 0)),
            scratch_shapes=[pltpu.VMEM((2, 2, ROUTE_TILE, d), F32), pltpu.SemaphoreType.DMA((2,))]),
        out_shape=jax.ShapeDtypeStruct((m, d), F32),
        compiler_params=_cparams("arbitrary"),
        name="moe_combine_final_norm",
    )(slot, ys, h, gates, mod, final_g.reshape(1, d))


def _grid_pos_embed(rows, dim):
    r = np.repeat(np.arange(rows, dtype=np.float32), GRID_W)
    col = np.tile(np.arange(GRID_W, dtype=np.float32), rows)
    quarter = dim // 4
    freq = np.exp(np.float32(-math.log(10000.0)) * np.arange(quarter, dtype=np.float32) / np.float32(quarter))
    ar = r[:, None] * freq
    ac = col[:, None] * freq
    return np.concatenate([np.sin(ar), np.cos(ar), np.sin(ac), np.cos(ac)], axis=-1).astype(np.float32)


def _reorder_w_in(w):
    o_ab = QKV_W
    o_z = o_ab + AB_COLS
    o_sg = o_z + GROUP_W
    o_pool = o_sg + 2 * GROUP_W
    o_cv = o_pool + GROUP_W
    pad = jnp.zeros((w.shape[0], LANES - AB_COLS), w.dtype)
    return jnp.concatenate([w[:, :QKV_W], w[:, o_z:o_sg], w[:, o_sg:o_pool], w[:, o_cv:o_cv + 2 * GROUP_W],
                            w[:, o_pool:o_cv], w[:, o_ab:o_z], pad], axis=1).astype(BF16)


def kernel(x, c, ctx, c_ctx, ada_w, ada_b, norm_mix_g, w_in, dn_conv_w, dn_a_log, dn_dt_bias, dn_norm_g,
           sg_ln_g, sg_ln_b, sg_w, sg_b, pool_w, pool_scale, cv_w, cv_b, cv_ln_g, cv_ln_b, w_out,
           norm_ffn_g, ffn_w1, ffn_w3, ffn_w2, router_w, router_b, moe_w1, moe_w3, moe_w2, final_norm_g):
    batch, seq, d = x.shape
    ctx_len = ctx.shape[1]
    depth = ada_w.shape[0]
    assert depth == 2 and d == 4 * GROUP_W and batch + 1 <= SUBLANES
    assert (2 * batch * seq) % EXPERT_ROWS == 0 and ctx_len % SEQ_TILE == 0 and seq % SEQ_TILE == 0
    ctx_rows, x_rows = batch * ctx_len, batch * seq
    n_ctx_tiles, n_x_tiles = ctx_rows // SEQ_TILE, x_rows // SEQ_TILE
    seg = (n_ctx_tiles, ctx_len // SEQ_TILE, seq // SEQ_TILE)
    row_fn_for = lambda tile: functools.partial(_mod_row, tile=tile, ctx_rows=ctx_rows, seq=seq, batch=batch)
    row_fn = row_fn_for(SEQ_TILE)

    pos = _grid_pos_embed(seq // GRID_W, d)
    h = jnp.concatenate([ctx.reshape(ctx_rows, d), (x + pos[None]).reshape(x_rows, d)], axis=0)
    cc = jnp.zeros((SUBLANES, d), F32).at[:batch].set(c).at[batch].set(c_ctx)
    mod = _ada(cc, ada_w, ada_b).reshape(depth, SUBLANES, 1, 6 * d)

    def mixing(l, h, tile0, n_tiles):
        p = _proj(h, norm_mix_g[l], mod[l], _reorder_w_in(w_in[l]), row_fn)
        local = _dn_local(p, dn_conv_w[l], dn_a_log[l], dn_dt_bias[l], seg)
        o_f, o_b = _dn_scan(local, batch=batch, ctx_len=ctx_len, seq=seq)
        params = (dn_norm_g[l], sg_ln_g[l], sg_ln_b[l], sg_w[l], sg_b[l], pool_w[l], pool_scale[l],
                  cv_w[l], cv_b[l], cv_ln_g[l], cv_ln_b[l])
        return _mix(h, p, o_f, o_b, mod[l], params, w_out[l].astype(BF16),
                    seg=seg, row_fn=row_fn, tile0=tile0, n_tiles=n_tiles)

    h = mixing(0, h, 0, n_ctx_tiles + n_x_tiles)
    h = _ffn(h, norm_ffn_g[0], mod[0], ffn_w1[0].astype(BF16), ffn_w3[0].astype(BF16), ffn_w2[0].astype(BF16),
             row_fn_for, math.gcd(512, ctx_rows, seq))

    hx = mixing(1, h, n_ctx_tiles, n_x_tiles)
    x_row_fn = lambda i: i * ROUTE_TILE // seq
    hn, route, gates, totals = _router(hx, norm_ffn_g[1], mod[1], router_w[0], router_b[0], x_row_fn, 0)
    n_slots = 2 * x_rows + N_EXPERTS * EXPERT_ROWS
    slot, slot_token, super_expert, super_sub, tile_valid = _routing_tables(route, totals, n_slots)
    xs = _gather_rows(hn, slot_token, tile_valid)
    ys = _experts(xs, super_expert, super_sub, moe_w1[0], moe_w3[0], moe_w2[0])
    out = _combine(ys, slot, hx, gates, mod[1], final_norm_g, x_row_fn)
    return out.reshape(batch, seq, d)
```

```python
import functools
import math

import jax
import jax.numpy as jnp
import numpy as np
from jax import lax
from jax.experimental import pallas as pl
from jax.experimental.pallas import tpu as pltpu

F32 = jnp.float32
BF16 = jnp.bfloat16
EPS = 1e-6

GRID_W = 64
N_HEADS = 4
HEAD_DIM = 128
GROUP_W = N_HEADS * HEAD_DIM
DN_CHUNK = 64
SEQ_TILE = 256
CHUNKS_PER_TILE = SEQ_TILE // DN_CHUNK
DN_BLOCK = 128
BLOCKS_PER_TILE = SEQ_TILE // DN_BLOCK
SG_CHUNK = 128
POOL_WINDOWS = (2, 4, 8, 16)
SHORT_CONV = 5
CONF_CONV = 31
N_EXPERTS = 8
LANES = 128
SUBLANES = 8
VMEM_LIMIT = 56 * 1024 * 1024

QKV_OFF, QKV_W = 0, 3 * GROUP_W
Z_OFF = QKV_OFF + QKV_W
SG_OFF = Z_OFF + GROUP_W
CV_OFF = SG_OFF + 2 * GROUP_W
POOL_OFF = CV_OFF + 2 * GROUP_W
AB_OFF = POOL_OFF + GROUP_W
P_WIDTH = AB_OFF + LANES
AB_COLS = 4 * N_HEADS

SHM, SCM, GM, SHF, SCF, GF = range(6)


def _dot(a, b):
    return jnp.dot(a, b, preferred_element_type=F32)


def _dot_nt(a, b):
    return lax.dot_general(a, b, (((1,), (1,)), ((), ())), preferred_element_type=F32)


def _dot_tn(a, b):
    return lax.dot_general(a, b, (((0,), (0,)), ((), ())), preferred_element_type=F32)


def _sigmoid(x):
    return 1.0 / (1.0 + jnp.exp(-x))


def _silu(x):
    return x * _sigmoid(x)


def _cparams(*sem):
    return pltpu.CompilerParams(dimension_semantics=sem, vmem_limit_bytes=VMEM_LIMIT)


def _mod_row(i, *, tile, ctx_rows, seq, batch):
    start = i * tile
    return jnp.where(start < ctx_rows, batch, (start - ctx_rows) // seq)


def _mod_spec(d, chunk, row_fn):
    return pl.BlockSpec((None, 1, d), lambda i, *_: (row_fn(i), 0, chunk))


def _norm_modulate(x, g, scale, shift):
    y = x * lax.rsqrt(jnp.mean(x * x, axis=-1, keepdims=True) + EPS) * g
    return y * (1.0 + scale) + shift


def _assemble_kernel(ctx_ref, x_ref, pos_ref, o_ref, *, n_ctx_tiles):
    i = pl.program_id(0)

    @pl.when(i < n_ctx_tiles)
    def _():
        o_ref[...] = ctx_ref[...]

    @pl.when(i >= n_ctx_tiles)
    def _():
        o_ref[...] = x_ref[...] + pos_ref[...]


def _assemble(ctx2d, x2d, pos):
    ctx_rows, d = ctx2d.shape
    x_rows, seq = x2d.shape[0], pos.shape[0]
    tile = math.gcd(512, ctx_rows, seq)
    nct, seq_tiles = ctx_rows // tile, seq // tile
    x_tile = lambda i: jnp.maximum(i - nct, 0)
    return pl.pallas_call(
        functools.partial(_assemble_kernel, n_ctx_tiles=nct),
        grid=((ctx_rows + x_rows) // tile,),
        in_specs=[pl.BlockSpec((tile, d), lambda i: (jnp.minimum(i, nct - 1), 0)),
                  pl.BlockSpec((tile, d), lambda i: (x_tile(i), 0)),
                  pl.BlockSpec((tile, d), lambda i: (x_tile(i) % seq_tiles, 0))],
        out_specs=pl.BlockSpec((tile, d), lambda i: (i, 0)),
        out_shape=jax.ShapeDtypeStruct((ctx_rows + x_rows, d), F32),
        compiler_params=_cparams("parallel"),
        name="assemble_rows",
    )(ctx2d, x2d, pos)


def _ada_kernel(c_ref, w_ref, b_ref, o_ref):
    s = _silu(c_ref[...]).astype(BF16)
    o_ref[...] = _dot(s, w_ref[...].astype(BF16)) + b_ref[...]


def _ada(cc, ada_w, ada_b):
    nl, d, n = ada_w.shape
    tn = 1024
    return pl.pallas_call(
        _ada_kernel,
        grid=(nl, n // tn),
        in_specs=[pl.BlockSpec((SUBLANES, d), lambda l, j: (0, 0)),
                  pl.BlockSpec((None, d, tn), lambda l, j: (l, 0, j)),
                  pl.BlockSpec((None, 1, tn), lambda l, j: (l, 0, j))],
        out_specs=pl.BlockSpec((None, SUBLANES, tn), lambda l, j: (l, 0, j)),
        out_shape=jax.ShapeDtypeStruct((nl, SUBLANES, n), F32),
        compiler_params=_cparams("parallel", "parallel"),
        name="ada_modulation",
    )(cc, ada_w, ada_b.reshape(nl, 1, n))


def _proj_kernel(h_ref, g_ref, sh_ref, sc_ref, w_ref, o_ref):
    y = _norm_modulate(h_ref[...], g_ref[...], sc_ref[...], sh_ref[...])
    o_ref[...] = _dot(y.astype(BF16), w_ref[...])


def _proj(h, g, mod, w, row_fn):
    m, d = h.shape
    n = w.shape[1]
    tm = SEQ_TILE
    return pl.pallas_call(
        _proj_kernel,
        grid=(m // tm,),
        in_specs=[pl.BlockSpec((tm, d), lambda i: (i, 0)),
                  pl.BlockSpec((1, d), lambda i: (0, 0)),
                  _mod_spec(d, SHM, row_fn),
                  _mod_spec(d, SCM, row_fn),
                  pl.BlockSpec((d, n), lambda i: (0, 0), pipeline_mode=pl.Buffered(1))],
        out_specs=pl.BlockSpec((tm, n), lambda i: (i, 0)),
        out_shape=jax.ShapeDtypeStruct((m, n), F32),
        compiler_params=_cparams("parallel"),
        name="in_proj",
    )(h, g.reshape(1, d), mod, mod, w)


def _segment_pos(i, n_ctx_tiles, ctx_seg_tiles, x_seg_tiles):
    in_ctx = i < n_ctx_tiles
    tpos = jnp.where(in_ctx, i % ctx_seg_tiles, (i - n_ctx_tiles) % x_seg_tiles)
    tlen = jnp.where(in_ctx, ctx_seg_tiles, x_seg_tiles)
    return tpos, tlen


def _fill_ext(ext_ref, prev, cur, nxt, first, last, halo):
    ext_ref[0:halo, :] = jnp.where(first, 0.0, prev)
    ext_ref[halo:halo + SEQ_TILE, :] = cur
    ext_ref[halo + SEQ_TILE:2 * halo + SEQ_TILE, :] = jnp.where(last, 0.0, nxt)


PK_W, PK_QG, PK_KD, PK_QK = 0, GROUP_W, 2 * GROUP_W, 3 * GROUP_W
PK_WIDTH = 4 * GROUP_W


def _dn_local_kernel(qkv_ref, prev_ref, next_ref, ab_ref, cw_ref, alog_ref, dtb_ref,
                     uf_ref, pkf_ref, ub_ref, pkb_ref, gam_ref,
                     ext_ref, t_ref, l_ref, rhs_ref, *, seg):
    tpos, tlen = _segment_pos(pl.program_id(0), *seg)
    halo = SUBLANES
    _fill_ext(ext_ref, prev_ref[...], qkv_ref[...], next_ref[...], tpos == 0, tpos == tlen - 1, halo)

    pad = SHORT_CONV // 2
    acc = cw_ref[0:1, :] * ext_ref[halo - pad:halo - pad + SEQ_TILE, :]
    for j in range(1, SHORT_CONV):
        acc = acc + cw_ref[j:j + 1, :] * ext_ref[halo - pad + j:halo - pad + j + SEQ_TILE, :]
    s = _silu(acc)

    ab = ab_ref[...]
    lane = lax.broadcasted_iota(jnp.int32, (SEQ_TILE, LANES), 1)
    xs = ab + dtb_ref[...]
    softplus = jnp.maximum(xs, 0.0) + jnp.log1p(jnp.exp(-jnp.abs(xs)))
    la = jnp.where(lane < 2 * N_HEADS, -jnp.exp(alog_ref[...]) * softplus, 0.0)
    beta = _sigmoid(ab)

    def chunk_masks(n):
        rr = lax.broadcasted_iota(jnp.int32, (n, n), 0)
        cc = lax.broadcasted_iota(jnp.int32, (n, n), 1)
        same = (rr // DN_CHUNK) == (cc // DN_CHUNK)
        return rr, cc, (same & (cc <= rr), same & (cc >= rr)), (same & (cc < rr), same & (cc > rr))

    _, _, incl_tile, _ = chunk_masks(SEQ_TILE)
    r, c, incl, strict = chunk_masks(DN_BLOCK)

    def off_block(s, d):
        hi, lo = (r, c) if d == 0 else (c, r)
        return ((hi // s) % 2 == 1) & ((lo // s) == (hi // s) - 1)

    la_hi = la.astype(BF16)
    rem = la - la_hi.astype(F32)
    la_mid = rem.astype(BF16)
    la_lo = (rem - la_mid.astype(F32)).astype(BF16)

    def chunk_sum(mask):
        m = jnp.where(mask, 1.0, 0.0).astype(BF16)
        return _dot(m, la_hi) + _dot(m, la_mid) + _dot(m, la_lo)

    g_fwd = chunk_sum(incl_tile[0])
    g_bwd = chunk_sum(incl_tile[1])
    g = jnp.where(lane < N_HEADS, g_fwd, g_bwd)
    g_tot = g_fwd + g_bwd - la
    g_t = g.T

    outs = ((uf_ref, pkf_ref), (ub_ref, pkb_ref))
    eye = jnp.where(r == c, 1.0, 0.0)
    for h in range(N_HEADS):
        sl = slice(h * HEAD_DIM, (h + 1) * HEAD_DIM)
        qh = s[:, h * HEAD_DIM:(h + 1) * HEAD_DIM]
        kh = s[:, GROUP_W + h * HEAD_DIM:GROUP_W + (h + 1) * HEAD_DIM]
        vh = s[:, 2 * GROUP_W + h * HEAD_DIM:2 * GROUP_W + (h + 1) * HEAD_DIM]
        qh = qh * lax.rsqrt(jnp.sum(qh * qh, axis=-1, keepdims=True) + EPS) * (HEAD_DIM ** -0.5)
        kh = kh * lax.rsqrt(jnp.sum(kh * kh, axis=-1, keepdims=True) + EPS)
        qh16 = qh.astype(BF16)
        kh16 = kh.astype(BF16)
        for d in range(2):
            gi = d * N_HEADS + h
            u_ref, pk_ref = outs[d]
            gcol = g[:, gi:gi + 1]
            bcol = beta[:, 2 * N_HEADS + gi:2 * N_HEADS + gi + 1]
            eg = jnp.exp(gcol)
            kb = kh * bcol
            rhs_ref[gi, :, :HEAD_DIM] = vh * bcol
            rhs_ref[gi, :, HEAD_DIM:] = kb * eg
            gl = g_tot[:, gi:gi + 1]
            pk_ref[:, PK_QG + h * HEAD_DIM:PK_QG + (h + 1) * HEAD_DIM] = (qh * eg).astype(BF16)
            pk_ref[:, PK_KD + h * HEAD_DIM:PK_KD + (h + 1) * HEAD_DIM] = (kh * jnp.exp(gl - gcol)).astype(BF16)
            pk_ref[:, PK_QK + h * HEAD_DIM + DN_CHUNK:PK_QK + (h + 1) * HEAD_DIM] = jnp.zeros(
                (SEQ_TILE, HEAD_DIM - DN_CHUNK), BF16)
            for cb in range(CHUNKS_PER_TILE):
                gam_ref[cb, :, gi * LANES:(gi + 1) * LANES] = jnp.broadcast_to(
                    jnp.exp(gl[cb * DN_CHUNK:cb * DN_CHUNK + 1, :]), (1, LANES))
            for blk in range(BLOCKS_PER_TILE):
                rows = slice(blk * DN_BLOCK, (blk + 1) * DN_BLOCK)
                unit = gi * BLOCKS_PER_TILE + blk
                grow = g_t[gi:gi + 1, rows]
                decay = jnp.where(incl[d], jnp.exp(jnp.where(incl[d], gcol[rows] - grow, 0.0)), 0.0)
                lmat = jnp.where(strict[d], bcol[rows] * _dot_nt(kh16[rows], kh16[rows]) * decay, 0.0)
                l_ref[unit] = lmat
                t_ref[unit] = eye - jnp.where(off_block(1, d), lmat, 0.0)
                qkm = jnp.where(incl[d], _dot_nt(qh16[rows], kh16[rows]) * decay, 0.0)
                qk = qkm[:, 0:DN_CHUNK]
                for cb in range(1, DN_BLOCK // DN_CHUNK):
                    qk = qk + qkm[:, cb * DN_CHUNK:(cb + 1) * DN_CHUNK]
                pk_ref[rows, PK_QK + h * HEAD_DIM:PK_QK + h * HEAD_DIM + DN_CHUNK] = qk.astype(BF16)

    n_chain = 2 * N_HEADS
    n_unit = n_chain * BLOCKS_PER_TILE
    unit_dir = lambda unit: unit // (N_HEADS * BLOCKS_PER_TILE)
    s_blk = 2
    while s_blk < DN_CHUNK:
        lt = []
        for unit in range(n_unit):
            l_s = jnp.where(off_block(s_blk, unit_dir(unit)), l_ref[unit], 0.0).astype(BF16)
            lt.append(_dot(l_s, t_ref[unit].astype(BF16)).astype(BF16))
        for unit in range(n_unit):
            t_inv = t_ref[unit]
            t_ref[unit] = t_inv - _dot(t_inv.astype(BF16), lt[unit])
        s_blk *= 2
    for unit in range(n_unit):
        gi, blk = divmod(unit, BLOCKS_PER_TILE)
        d, h = divmod(gi, N_HEADS)
        rows = slice(blk * DN_BLOCK, (blk + 1) * DN_BLOCK)
        u_ref, pk_ref = outs[d]
        rhs = rhs_ref[gi, rows, :]
        sol = rhs + _dot(jnp.where(strict[d], t_ref[unit], 0.0).astype(BF16), rhs.astype(BF16))
        u_ref[rows, h * HEAD_DIM:(h + 1) * HEAD_DIM] = sol[:, :HEAD_DIM]
        pk_ref[rows, PK_W + h * HEAD_DIM:PK_W + (h + 1) * HEAD_DIM] = sol[:, HEAD_DIM:].astype(BF16)


def _batch_major_tile(i, n_ctx_tiles, ctx_seg_tiles, x_seg_tiles):
    per_batch = ctx_seg_tiles + x_seg_tiles
    j = i - n_ctx_tiles
    return jnp.where(i < n_ctx_tiles,
                     (i // ctx_seg_tiles) * per_batch + i % ctx_seg_tiles,
                     (j // x_seg_tiles) * per_batch + ctx_seg_tiles + j % x_seg_tiles)


def _dn_local(p, conv_w, a_log, dt_bias, seg):
    rows = p.shape[0]
    nt = rows // SEQ_TILE
    hb = SEQ_TILE // SUBLANES
    last_hb = rows // SUBLANES - 1
    cw = jnp.zeros((SUBLANES, QKV_W), F32).at[:SHORT_CONV].set(conv_w)
    pad_row = lambda v: jnp.zeros((1, LANES), F32).at[0, :2 * N_HEADS].set(v.reshape(-1))
    u_out = jax.ShapeDtypeStruct((rows, GROUP_W), F32)
    pk_out = jax.ShapeDtypeStruct((rows, PK_WIDTH), BF16)
    gam_out = jax.ShapeDtypeStruct((rows // DN_CHUNK, 1, 2 * N_HEADS * LANES), F32)
    dst = lambda i: _batch_major_tile(i, *seg)
    u_spec = pl.BlockSpec((SEQ_TILE, GROUP_W), lambda i: (dst(i), 0))
    pk_spec = pl.BlockSpec((SEQ_TILE, PK_WIDTH), lambda i: (dst(i), 0))
    full = lambda shape: pl.BlockSpec(shape, lambda i: (0,) * len(shape))
    n_chain = 2 * N_HEADS
    return pl.pallas_call(
        functools.partial(_dn_local_kernel, seg=seg),
        grid=(nt,),
        in_specs=[pl.BlockSpec((SEQ_TILE, QKV_W), lambda i: (i, QKV_OFF // QKV_W)),
                  pl.BlockSpec((SUBLANES, QKV_W), lambda i: (jnp.maximum(i * hb - 1, 0), 0)),
                  pl.BlockSpec((SUBLANES, QKV_W), lambda i: (jnp.minimum((i + 1) * hb, last_hb), 0)),
                  pl.BlockSpec((SEQ_TILE, LANES), lambda i: (i, AB_OFF // LANES)),
                  full((SUBLANES, QKV_W)), full((1, LANES)), full((1, LANES))],
        out_specs=[u_spec, pk_spec, u_spec, pk_spec,
                   pl.BlockSpec((CHUNKS_PER_TILE, 1, 2 * N_HEADS * LANES), lambda i: (dst(i), 0, 0))],
        out_shape=[u_out, pk_out, u_out, pk_out, gam_out],
        scratch_shapes=[pltpu.VMEM((SEQ_TILE + 2 * SUBLANES, QKV_W), F32),
                        pltpu.VMEM((n_chain * BLOCKS_PER_TILE, DN_BLOCK, DN_BLOCK), F32),
                        pltpu.VMEM((n_chain * BLOCKS_PER_TILE, DN_BLOCK, DN_BLOCK), F32),
                        pltpu.VMEM((n_chain, SEQ_TILE, 2 * HEAD_DIM), F32)],
        compiler_params=_cparams("parallel"),
        name="dn_local",
    )(p, p, p, p, cw, pad_row(a_log), pad_row(dt_bias))


def _dn_scan_kernel(uf, pkf, gamf, ub, pkb, gamb, of_ref, ob_ref, s_ref, *, batch):
    @pl.when(pl.program_id(0) == 0)
    def _():
        s_ref[...] = jnp.zeros_like(s_ref)

    dirs = ((uf, pkf, gamf, of_ref), (ub, pkb, gamb, ob_ref))
    chains = [(b, d, h) for b in range(batch) for d in range(2) for h in range(N_HEADS)]
    head = lambda off, h: slice(off + h * HEAD_DIM, off + (h + 1) * HEAD_DIM)

    ws = []
    for ci, (b, d, h) in enumerate(chains):
        pk = dirs[d][1]
        lhs = jnp.concatenate([pk[b, :, head(PK_W, h)], pk[b, :, head(PK_QG, h)]], axis=0)
        ws.append(_dot(lhs, s_ref[ci].astype(BF16)))
    v16 = []
    for ci, (b, d, h) in enumerate(chains):
        v16.append((dirs[d][0][b, :, head(0, h)] - ws[ci][:DN_CHUNK]).astype(BF16))
    for ci, (b, d, h) in enumerate(chains):
        pk, o_ref = dirs[d][1], dirs[d][3]
        qk = pk[b, :, PK_QK + h * HEAD_DIM:PK_QK + h * HEAD_DIM + DN_CHUNK]
        o_ref[b, :, head(0, h)] = ws[ci][DN_CHUNK:] + _dot(qk, v16[ci])
    for ci, (b, d, h) in enumerate(chains):
        pk, gam = dirs[d][1], dirs[d][2]
        gi = d * N_HEADS + h
        s_ref[ci] = s_ref[ci] * gam[b, :, gi * LANES:(gi + 1) * LANES] + _dot_tn(pk[b, :, head(PK_KD, h)], v16[ci])


def _dn_scan(local, *, batch, ctx_len, seq):
    uf, pkf, ub, pkb, gam = local
    cc, xc = ctx_len // DN_CHUNK, seq // DN_CHUNK
    nc = cc + xc
    bwd = lambda s: jnp.where(s < cc, cc - 1 - s, cc + (xc - 1 - (s - cc)))
    by_chunk = lambda a: a.reshape(batch, nc, *a.shape[1:]) if a.ndim == 3 else a.reshape(batch, nc, DN_CHUNK, a.shape[1])

    def specs(idx):
        return [pl.BlockSpec((batch, None, DN_CHUNK, GROUP_W), lambda s: (0, idx(s), 0, 0)),
                pl.BlockSpec((batch, None, DN_CHUNK, PK_WIDTH), lambda s: (0, idx(s), 0, 0)),
                pl.BlockSpec((batch, None, 1, 2 * N_HEADS * LANES), lambda s: (0, idx(s), 0, 0))]

    out = jax.ShapeDtypeStruct((batch, nc, DN_CHUNK, GROUP_W), F32)
    fwd = lambda s: s
    o_f, o_b = pl.pallas_call(
        functools.partial(_dn_scan_kernel, batch=batch),
        grid=(nc,),
        in_specs=specs(fwd) + specs(bwd),
        out_specs=[pl.BlockSpec((batch, None, DN_CHUNK, GROUP_W), lambda s: (0, s, 0, 0)),
                   pl.BlockSpec((batch, None, DN_CHUNK, GROUP_W), lambda s: (0, bwd(s), 0, 0))],
        out_shape=[out, out],
        scratch_shapes=[pltpu.VMEM((batch * 2 * N_HEADS, HEAD_DIM, HEAD_DIM), F32)],
        compiler_params=_cparams("arbitrary"),
        name="dn_scan",
    )(by_chunk(uf), by_chunk(pkf), by_chunk(gam), by_chunk(ub), by_chunk(pkb), by_chunk(gam))
    return o_f.reshape(-1, GROUP_W), o_b.reshape(-1, GROUP_W)


def _layernorm(x, g, b):
    mu = jnp.mean(x, axis=-1, keepdims=True)
    xc = x - mu
    var = jnp.mean(xc * xc, axis=-1, keepdims=True)
    return xc * lax.rsqrt(var + EPS) * g + b


def _mix_kernel(h_ref, of_ref, ob_ref, z_ref, sg_ref, cv_ref, cvp_ref, cvn_ref, pool_ref, poolp_ref, pooln_ref,
                gm_ref, dng_ref, sglg_ref, sglb_ref, sgw_ref, sgbt_ref, pw_ref, ps_ref,
                cvw_ref, cvb_ref, cvlg_ref, cvlb_ref, wout_ref,
                o_ref, y_ref, pext_ref, cext_ref, cacc_ref, *, seg, tile0):
    tpos, tlen = _segment_pos(pl.program_id(0) + tile0, *seg)
    first, last = tpos == 0, tpos == tlen - 1

    o = of_ref[...] + ob_ref[...]
    z = z_ref[...]
    for h in range(N_HEADS):
        sl = slice(h * HEAD_DIM, (h + 1) * HEAD_DIM)
        oh = o[:, sl]
        yh = oh * lax.rsqrt(jnp.mean(oh * oh, axis=-1, keepdims=True) + EPS) * dng_ref[...] * _silu(z[:, sl])
        y_ref[:, sl] = yh.astype(BF16)

    psg = sg_ref[...]
    psg = 0.5 * psg * (1.0 + jnp.tanh(math.sqrt(2.0 / math.pi) * (psg + 0.044715 * (psg * psg * psg))))
    u = psg[:, :GROUP_W]
    v = _layernorm(psg[:, GROUP_W:], sglg_ref[...], sglb_ref[...]).astype(BF16)
    for n in range(SEQ_TILE // SG_CHUNK):
        rs = slice(n * SG_CHUNK, (n + 1) * SG_CHUNK)
        for h in range(N_HEADS):
            sl = slice(h * HEAD_DIM, (h + 1) * HEAD_DIM)
            sv = _dot(sgw_ref[h], v[rs, sl]) + sgbt_ref[:, h:h + 1]
            y_ref[rs, GROUP_W + h * HEAD_DIM:GROUP_W + (h + 1) * HEAD_DIM] = (u[rs, sl] * sv).astype(BF16)

    halo = SUBLANES
    _fill_ext(pext_ref, poolp_ref[...], pool_ref[...], pooln_ref[...], first, last, halo)
    t = tpos * SEQ_TILE + lax.broadcasted_iota(jnp.int32, (SEQ_TILE, 1), 0)
    seg_len = tlen * SEQ_TILE
    for gi, win in enumerate(POOL_WINDOWS):
        sl = slice(gi * LANES, (gi + 1) * LANES)
        tot = pext_ref[halo - win // 2:halo - win // 2 + SEQ_TILE, sl]
        for m in range(1 - win // 2, win // 2):
            tot = tot + pext_ref[halo + m:halo + m + SEQ_TILE, sl]
        cnt = jnp.clip(t + win // 2, 0, seg_len) - jnp.clip(t - win // 2, 0, seg_len)
        yg = tot / cnt.astype(F32) - pool_ref[:, sl]
        yg = _dot(yg.astype(BF16), pw_ref[gi]) * ps_ref[:, sl]
        y_ref[:, 2 * GROUP_W + gi * LANES:2 * GROUP_W + (gi + 1) * LANES] = yg.astype(BF16)

    halo = 2 * SUBLANES
    glu = lambda p: p[:, :GROUP_W] * _sigmoid(p[:, GROUP_W:])
    _fill_ext(cext_ref, glu(cvp_ref[...]), glu(cv_ref[...]), glu(cvn_ref[...]), first, last, halo)
    pad = CONF_CONV // 2
    span = SEQ_TILE + SUBLANES
    acc = None
    for rho in range(SUBLANES):
        part = None
        for j in range(CONF_CONV):
            off = halo - pad + j
            if off % SUBLANES == rho:
                term = cvw_ref[j:j + 1, :] * cext_ref[off - rho:off - rho + span, :]
                part = term if part is None else part + term
        if part is not None:
            cacc_ref[...] = part
            shifted = cacc_ref[rho:rho + SEQ_TILE, :]
            acc = shifted if acc is None else acc + shifted
    yc = _silu(_layernorm(acc + cvb_ref[...], cvlg_ref[...], cvlb_ref[...]))
    y_ref[:, 3 * GROUP_W:] = yc.astype(BF16)

    o_ref[...] = h_ref[...] + gm_ref[...] * _dot(y_ref[...], wout_ref[...])


def _mix(h, p, o_f, o_b, mod, params, wout, *, seg, row_fn, tile0, n_tiles):
    d = h.shape[1]
    (dng, sglg, sglb, sgw, sgb, pw, ps, cvw, cvb, cvlg, cvlb) = params
    rows = p.shape[0]
    row = lambda v: v.reshape(1, -1)
    cvw_p = jnp.zeros((4 * SUBLANES, GROUP_W), F32).at[:CONF_CONV].set(cvw)
    sgbt = jnp.zeros((SG_CHUNK, LANES), F32).at[:, :N_HEADS].set(sgb.T)

    def halo_specs(width, col, halo):
        hb = SEQ_TILE // halo
        last_hb = rows // halo - 1
        return [pl.BlockSpec((SEQ_TILE, width), lambda i: (i + tile0, col)),
                pl.BlockSpec((halo, width), lambda i: (jnp.maximum((i + tile0) * hb - 1, 0), col)),
                pl.BlockSpec((halo, width), lambda i: (jnp.minimum((i + tile0 + 1) * hb, last_hb), col))]

    full = lambda a: pl.BlockSpec(a.shape, lambda i: (0,) * a.ndim)
    consts = [row(dng), row(sglg), row(sglb), sgw.astype(BF16), sgbt, pw.astype(BF16), row(ps),
              cvw_p, row(cvb), row(cvlg), row(cvlb)]
    return pl.pallas_call(
        functools.partial(_mix_kernel, seg=seg, tile0=tile0),
        grid=(n_tiles,),
        in_specs=[pl.BlockSpec((SEQ_TILE, d), lambda i: (i + tile0, 0)),
                  pl.BlockSpec((SEQ_TILE, GROUP_W), lambda i: (_batch_major_tile(i + tile0, *seg), 0)),
                  pl.BlockSpec((SEQ_TILE, GROUP_W), lambda i: (_batch_major_tile(i + tile0, *seg), 0)),
                  pl.BlockSpec((SEQ_TILE, GROUP_W), lambda i: (i + tile0, Z_OFF // GROUP_W)),
                  pl.BlockSpec((SEQ_TILE, 2 * GROUP_W), lambda i: (i + tile0, SG_OFF // (2 * GROUP_W)))]
                 + halo_specs(2 * GROUP_W, CV_OFF // (2 * GROUP_W), 2 * SUBLANES)
                 + halo_specs(GROUP_W, POOL_OFF // GROUP_W, SUBLANES)
                 + [_mod_spec(d, GM, lambda i: row_fn(i + tile0))]
                 + [full(a) for a in consts]
                 + [pl.BlockSpec(wout.shape, lambda i: (0, 0), pipeline_mode=pl.Buffered(1))],
        out_specs=pl.BlockSpec((SEQ_TILE, d), lambda i: (i, 0)),
        out_shape=jax.ShapeDtypeStruct((n_tiles * SEQ_TILE, d), F32),
        scratch_shapes=[pltpu.VMEM((SEQ_TILE, d), BF16),
                        pltpu.VMEM((SEQ_TILE + 2 * SUBLANES, GROUP_W), F32),
                        pltpu.VMEM((SEQ_TILE + 4 * SUBLANES, GROUP_W), F32),
                        pltpu.VMEM((SEQ_TILE + SUBLANES, GROUP_W), F32)],
        compiler_params=_cparams("parallel"),
        name="mixers_out_proj",
    )(h, o_f, o_b, p, p, p, p, p, p, p, p, mod, *consts, wout)


def _ffn_kernel(h_ref, g_ref, sh_ref, sc_ref, gate_ref, w1_ref, w3_ref, w2_ref, o_ref, xn_ref, acc_ref):
    j = pl.program_id(1)

    @pl.when(j == 0)
    def _():
        xn_ref[...] = _norm_modulate(h_ref[...], g_ref[...], sc_ref[...], sh_ref[...]).astype(BF16)
        acc_ref[...] = jnp.zeros_like(acc_ref)

    xn = xn_ref[...]
    mid = _silu(_dot(xn, w1_ref[...])) * _dot(xn, w3_ref[...])
    acc_ref[...] += _dot(mid.astype(BF16), w2_ref[...])

    @pl.when(j == pl.num_programs(1) - 1)
    def _():
        o_ref[...] = h_ref[...] + gate_ref[...] * acc_ref[...]


def _ffn(h, g, mod, w1, w3, w2, row_fn_for, tm):
    m, d = h.shape
    f = w1.shape[1]
    tf = 512
    row_fn = row_fn_for(tm)
    return pl.pallas_call(
        _ffn_kernel,
        grid=(m // tm, f // tf),
        in_specs=[pl.BlockSpec((tm, d), lambda i, j: (i, 0)),
                  pl.BlockSpec((1, d), lambda i, j: (0, 0)),
                  _mod_spec(d, SHF, row_fn), _mod_spec(d, SCF, row_fn), _mod_spec(d, GF, row_fn),
                  pl.BlockSpec((d, tf), lambda i, j: (0, j)),
                  pl.BlockSpec((d, tf), lambda i, j: (0, j)),
                  pl.BlockSpec((tf, d), lambda i, j: (j, 0))],
        out_specs=pl.BlockSpec((tm, d), lambda i, j: (i, 0)),
        out_shape=jax.ShapeDtypeStruct((m, d), F32),
        scratch_shapes=[pltpu.VMEM((tm, d), BF16), pltpu.VMEM((tm, d), F32)],
        compiler_params=_cparams("parallel", "arbitrary"),
        name="dense_swiglu",
    )(h, g.reshape(1, d), mod, mod, mod, w1, w3, w2)


ROUTE_TILE = 256
EXPERT_ROWS = 3072


def _router_kernel(h_ref, g_ref, sh_ref, sc_ref, rw_ref, rb_ref, hn_ref, idx_ref, gate_ref, total_ref, count_ref):
    hn = _norm_modulate(h_ref[...], g_ref[...], sc_ref[...], sh_ref[...])
    hn_ref[...] = hn
    hn_hi = hn.astype(BF16)
    hn_lo = (hn - hn_hi.astype(F32)).astype(BF16)
    rw = rw_ref[...]
    rw_hi = rw.astype(BF16)
    rw_lo = (rw - rw_hi.astype(F32)).astype(BF16)
    logits = _dot(hn_hi, rw_hi) + (_dot(hn_hi, rw_lo) + _dot(hn_lo, rw_hi)) + rb_ref[...]
    lane = lax.broadcasted_iota(jnp.int32, logits.shape, 1)
    lane_f = lane.astype(F32)
    neg = -jnp.inf
    lg = jnp.where(lane < N_EXPERTS, logits, neg)
    m1 = jnp.max(lg, axis=-1, keepdims=True)
    i1 = jnp.min(jnp.where(lg == m1, lane_f, float(LANES)), axis=-1, keepdims=True)
    lg2 = jnp.where(lane_f == i1, neg, lg)
    m2 = jnp.max(lg2, axis=-1, keepdims=True)
    i2 = jnp.min(jnp.where(lg2 == m2, lane_f, float(LANES)), axis=-1, keepdims=True)
    e2 = jnp.exp(m2 - m1)
    g1 = 1.0 / (1.0 + e2)
    g2 = e2 / (1.0 + e2)
    gate_ref[...] = jnp.where(lane == 0, g1, jnp.where(lane == 1, g2, 0.0))

    @pl.when(pl.program_id(0) == 0)
    def _():
        count_ref[...] = jnp.zeros_like(count_ref)

    pick1 = jnp.where(lane_f == i1, 1.0, 0.0)
    pick2 = jnp.where(lane_f == i2, 1.0, 0.0)
    picked = pick1 + pick2
    r = lax.broadcasted_iota(jnp.int32, (ROUTE_TILE, ROUTE_TILE), 0)
    c = lax.broadcasted_iota(jnp.int32, (ROUTE_TILE, ROUTE_TILE), 1)
    earlier = jnp.where(c < r, 1.0, 0.0).astype(BF16)
    before = count_ref[0:1, :] + _dot(earlier, picked.astype(BF16))
    rank1 = jnp.sum(pick1 * before, axis=-1, keepdims=True)
    rank2 = jnp.sum(pick2 * before, axis=-1, keepdims=True)
    count_ref[...] = count_ref[...] + jnp.sum(picked, axis=0, keepdims=True)
    total_ref[...] = count_ref[...]
    idx_ref[...] = jnp.where(lane == 0, i1, jnp.where(lane == 1, i2, jnp.where(
        lane == 2, rank1, jnp.where(lane == 3, rank2, 0.0)))).astype(jnp.int32)


def _router(h, g, mod, rw, rb, row_fn, tile0):
    d = h.shape[1]
    tm = ROUTE_TILE
    n_tiles = h.shape[0] // tm - tile0
    m = n_tiles * tm
    rw_p = jnp.zeros((d, LANES), F32).at[:, :N_EXPERTS].set(rw)
    rb_p = jnp.zeros((1, LANES), F32).at[0, :N_EXPERTS].set(rb)
    shifted = lambda i: row_fn(i + tile0)
    return pl.pallas_call(
        _router_kernel,
        grid=(n_tiles,),
        in_specs=[pl.BlockSpec((tm, d), lambda i: (i + tile0, 0)),
                  pl.BlockSpec((1, d), lambda i: (0, 0)),
                  _mod_spec(d, SHF, shifted), _mod_spec(d, SCF, shifted),
                  pl.BlockSpec((d, LANES), lambda i: (0, 0)),
                  pl.BlockSpec((1, LANES), lambda i: (0, 0))],
        out_specs=[pl.BlockSpec((tm, d), lambda i: (i, 0)),
                   pl.BlockSpec((tm, LANES), lambda i: (i, 0)),
                   pl.BlockSpec((tm, LANES), lambda i: (i, 0)),
                   pl.BlockSpec((SUBLANES, LANES), lambda i: (0, 0))],
        out_shape=[jax.ShapeDtypeStruct((m, d), F32),
                   jax.ShapeDtypeStruct((m, LANES), jnp.int32),
                   jax.ShapeDtypeStruct((m, LANES), F32),
                   jax.ShapeDtypeStruct((SUBLANES, LANES), F32)],
        scratch_shapes=[pltpu.VMEM((SUBLANES, LANES), F32)],
        compiler_params=_cparams("arbitrary"),
        name="moe_router",
    )(h, g.reshape(1, d), mod, mod, rw_p, rb_p)


def _routing_tables(route, totals, n_slots):
    e_flat = route[:, 0:2].reshape(-1)
    rank = route[:, 2:4].reshape(-1)
    counts = totals[0, :N_EXPERTS].astype(jnp.int32)
    passes = (counts + EXPERT_ROWS - 1) // EXPERT_ROWS
    sub_tiles = (counts + ROUTE_TILE - 1) // ROUTE_TILE
    pass_rows = jnp.maximum((sub_tiles + jnp.maximum(passes, 1) - 1) // jnp.maximum(passes, 1), 1) * ROUTE_TILE
    padded = passes * EXPERT_ROWS
    ends = jnp.cumsum(padded)
    starts = ends - padded
    onehot = (e_flat[:, None] == jnp.arange(N_EXPERTS)[None, :]).astype(jnp.int32)
    rank_rows = jnp.sum(onehot * pass_rows[None, :], axis=1)
    rank_pass = rank // rank_rows
    slot = jnp.sum(onehot * starts[None, :], axis=1) + rank_pass * EXPERT_ROWS + (rank - rank_pass * rank_rows)
    slot_token = jnp.zeros((n_slots,), jnp.int32).at[slot].set(jnp.arange(slot.shape[0], dtype=jnp.int32) // 2)
    n_super = n_slots // EXPERT_ROWS
    super_start = jnp.arange(n_super, dtype=jnp.int32) * EXPERT_ROWS
    super_expert = jnp.minimum(jnp.sum(super_start[:, None] >= ends[None, :], axis=1), N_EXPERTS - 1).astype(jnp.int32)
    super_pass = (super_start - starts[super_expert]) // EXPERT_ROWS
    valid_rows = jnp.clip(counts[super_expert] - super_pass * pass_rows[super_expert], 0, pass_rows[super_expert])
    valid_rows = jnp.where(super_start < ends[-1], valid_rows, 0)
    super_sub = ((valid_rows + ROUTE_TILE - 1) // ROUTE_TILE).astype(jnp.int32)
    tile_valid = (jnp.arange(n_slots // ROUTE_TILE, dtype=jnp.int32) % (EXPERT_ROWS // ROUTE_TILE)
                  < jnp.repeat(super_sub, EXPERT_ROWS // ROUTE_TILE)).astype(jnp.int32)
    return slot.astype(jnp.int32), slot_token, super_expert, super_sub, tile_valid


def _row_copy(src_hbm, dst_vmem, src_row, dst_row, sem):
    return pltpu.make_async_copy(src_hbm.at[pl.ds(src_row, 1), :], dst_vmem.at[pl.ds(dst_row, 1), :], sem)


def _gather_kernel(tok_ref, valid_ref, hn_hbm, o_ref, buf_ref, sem):
    i = pl.program_id(0)
    n = pl.num_programs(0)

    def request(tile):
        slot = tile % 2

        def start(r, carry):
            _row_copy(hn_hbm, buf_ref.at[slot], tok_ref[tile * ROUTE_TILE + r], r, sem.at[slot]).start()
            return carry

        lax.fori_loop(0, ROUTE_TILE, start, 0, unroll=8)

    @pl.when((i == 0) & (valid_ref[0] != 0))
    def _():
        request(0)

    nxt = jnp.minimum(i + 1, n - 1)

    @pl.when((i + 1 < n) & (valid_ref[nxt] != 0))
    def _():
        request(nxt)

    @pl.when(valid_ref[i] == 0)
    def _():
        o_ref[...] = jnp.zeros_like(o_ref)

    @pl.when(valid_ref[i] != 0)
    def _():
        slot = i % 2

        pltpu.make_async_copy(hn_hbm.at[pl.ds(0, ROUTE_TILE), :], buf_ref.at[slot], sem.at[slot]).wait()
        o_ref[...] = buf_ref[slot].astype(BF16)


def _gather_rows(hn, slot_token, tile_valid):
    d = hn.shape[1]
    n_slots = slot_token.shape[0]
    return pl.pallas_call(
        _gather_kernel,
        grid_spec=pltpu.PrefetchScalarGridSpec(
            num_scalar_prefetch=2,
            grid=(n_slots // ROUTE_TILE,),
            in_specs=[pl.BlockSpec(memory_space=pl.ANY)],
            out_specs=pl.BlockSpec((ROUTE_TILE, d), lambda i, *_: (i, 0)),
            scratch_shapes=[pltpu.VMEM((2, ROUTE_TILE, d), F32), pltpu.SemaphoreType.DMA((2,))]),
        out_shape=jax.ShapeDtypeStruct((n_slots, d), BF16),
        compiler_params=_cparams("arbitrary"),
        name="moe_gather",
    )(slot_token, tile_valid, hn)


WEIGHT_SPLIT = 4


def _expert_kernel(exp_ref, sub_ref, x_ref, *refs):
    w1_parts = refs[:WEIGHT_SPLIT]
    w3_parts = refs[WEIGHT_SPLIT:2 * WEIGHT_SPLIT]
    w2_parts = refs[2 * WEIGHT_SPLIT:3 * WEIGHT_SPLIT]
    o_ref, w1b_ref, w3b_ref, w2b_ref = refs[3 * WEIGHT_SPLIT:]
    s, j = pl.program_id(0), pl.program_id(1)

    @pl.when(j == 0)
    def _():
        o_ref[...] = jnp.zeros_like(o_ref)

    @pl.when(sub_ref[s] > 0)
    def _():
        for parts, dst in ((w1_parts, w1b_ref), (w3_parts, w3b_ref), (w2_parts, w2b_ref)):
            rows = dst.shape[0] // WEIGHT_SPLIT
            for k, part in enumerate(parts):
                dst[k * rows:(k + 1) * rows, :] = part[...].astype(BF16)

        def swiglu_rows(start, size):
            rows = pl.ds(pl.multiple_of(start, ROUTE_TILE), size)
            xr = x_ref[rows, :]
            mid = _silu(_dot(xr, w1b_ref[...])) * _dot(xr, w3b_ref[...])
            o_ref[rows, :] += _dot(mid.astype(BF16), w2b_ref[...])

        n_sub = sub_ref[s]
        n_pair = n_sub // 2

        def pair(r, carry):
            swiglu_rows(r * (2 * ROUTE_TILE), 2 * ROUTE_TILE)
            return carry

        lax.fori_loop(0, n_pair, pair, 0)

        @pl.when(n_sub % 2 == 1)
        def _():
            swiglu_rows(n_pair * (2 * ROUTE_TILE), ROUTE_TILE)


def _experts(xs, super_expert, super_sub, w1, w3, w2):
    n_slots, d = xs.shape
    f = w1.shape[2]
    tf = 256
    nf = f // tf
    n_super = n_slots // EXPERT_ROWS
    ns = WEIGHT_SPLIT

    def f_eff(s, j, sub):
        return jnp.where(sub[s] > 0, j, nf - 1)

    up_specs = [pl.BlockSpec((None, d // ns, tf), lambda s, j, e, sub, k=k: (e[s], k, f_eff(s, j, sub)))
                for k in range(ns)]
    down_specs = [pl.BlockSpec((None, tf // ns, d), lambda s, j, e, sub, k=k: (e[s], f_eff(s, j, sub) * ns + k, 0))
                  for k in range(ns)]
    row_spec = pl.BlockSpec((EXPERT_ROWS, d), lambda s, j, e, sub: (s, 0), pipeline_mode=pl.Buffered(1))
    return pl.pallas_call(
        _expert_kernel,
        grid_spec=pltpu.PrefetchScalarGridSpec(
            num_scalar_prefetch=2,
            grid=(n_super, nf),
            in_specs=[row_spec] + up_specs + up_specs + down_specs,
            out_specs=row_spec,
            scratch_shapes=[pltpu.VMEM((d, tf), BF16), pltpu.VMEM((d, tf), BF16), pltpu.VMEM((tf, d), BF16)]),
        out_shape=jax.ShapeDtypeStruct((n_slots, d), F32),
        compiler_params=_cparams("arbitrary", "arbitrary"),
        name="moe_experts",
    )(super_expert, super_sub, xs, *([w1] * ns), *([w3] * ns), *([w2] * ns))


def _combine_kernel(slot_ref, ys_hbm, h_ref, gate_ref, gf_ref, fg_ref, o_ref, buf_ref, sem):
    i = pl.program_id(0)
    n = pl.num_programs(0)

    def request(tile):
        slot = tile % 2

        def start(r, carry):
            for k in range(2):
                src = slot_ref[2 * (tile * ROUTE_TILE + r) + k]
                _row_copy(ys_hbm, buf_ref.at[slot, k], src, r, sem.at[slot]).start()
            return carry

        lax.fori_loop(0, ROUTE_TILE, start, 0, unroll=8)

    @pl.when(i == 0)
    def _():
        request(0)

    @pl.when(i + 1 < n)
    def _():
        request(jnp.minimum(i + 1, n - 1))

    slot = i % 2

    for k in range(2):
        pltpu.make_async_copy(ys_hbm.at[pl.ds(0, ROUTE_TILE), :], buf_ref.at[slot, k], sem.at[slot]).wait()
    gates = gate_ref[...]
    y = gates[:, 0:1] * buf_ref[slot, 0] + gates[:, 1:2] * buf_ref[slot, 1]
    x = h_ref[...] + gf_ref[...] * y
    o_ref[...] = x * lax.rsqrt(jnp.mean(x * x, axis=-1, keepdims=True) + EPS) * fg_ref[...]


def _combine(ys, slot, h, gates, mod, final_g, row_fn):
    m, d = h.shape
    return pl.pallas_call(
        _combine_kernel,
        grid_spec=pltpu.PrefetchScalarGridSpec(
            num_scalar_prefetch=1,
            grid=(m // ROUTE_TILE,),
            in_specs=[pl.BlockSpec(memory_space=pl.ANY),
                      pl.BlockSpec((ROUTE_TILE, d), lambda i, *_: (i, 0)),
                      pl.BlockSpec((ROUTE_TILE, LANES), lambda i, *_: (i, 0)),
                      _mod_spec(d, GF, row_fn),
                      pl.BlockSpec((1, d), lambda i, *_: (0, 0))],
            out_specs=pl.BlockSpec((ROUTE_TILE, d), lambda i, *_: (i, 0)),
            scratch_shapes=[pltpu.VMEM((2, 2, ROUTE_TILE, d), F32), pltpu.SemaphoreType.DMA((2,))]),
        out_shape=jax.ShapeDtypeStruct((m, d), F32),
        compiler_params=_cparams("arbitrary"),
        name="moe_combine_final_norm",
    )(slot, ys, h, gates, mod, final_g.reshape(1, d))


def _grid_pos_embed(rows, dim):
    r = np.repeat(np.arange(rows, dtype=np.float32), GRID_W)
    col = np.tile(np.arange(GRID_W, dtype=np.float32), rows)
    quarter = dim // 4
    freq = np.exp(np.float32(-math.log(10000.0)) * np.arange(quarter, dtype=np.float32) / np.float32(quarter))
    ar = r[:, None] * freq
    ac = col[:, None] * freq
    return np.concatenate([np.sin(ar), np.cos(ar), np.sin(ac), np.cos(ac)], axis=-1).astype(np.float32)


def _reorder_w_in(w):
    o_ab = QKV_W
    o_z = o_ab + AB_COLS
    o_sg = o_z + GROUP_W
    o_pool = o_sg + 2 * GROUP_W
    o_cv = o_pool + GROUP_W
    pad = jnp.zeros((w.shape[0], LANES - AB_COLS), w.dtype)
    return jnp.concatenate([w[:, :QKV_W], w[:, o_z:o_sg], w[:, o_sg:o_pool], w[:, o_cv:o_cv + 2 * GROUP_W],
                            w[:, o_pool:o_cv], w[:, o_ab:o_z], pad], axis=1).astype(BF16)


def kernel(x, c, ctx, c_ctx, ada_w, ada_b, norm_mix_g, w_in, dn_conv_w, dn_a_log, dn_dt_bias, dn_norm_g,
           sg_ln_g, sg_ln_b, sg_w, sg_b, pool_w, pool_scale, cv_w, cv_b, cv_ln_g, cv_ln_b, w_out,
           norm_ffn_g, ffn_w1, ffn_w3, ffn_w2, router_w, router_b, moe_w1, moe_w3, moe_w2, final_norm_g):
    batch, seq, d = x.shape
    ctx_len = ctx.shape[1]
    depth = ada_w.shape[0]
    assert depth == 2 and d == 4 * GROUP_W and batch + 1 <= SUBLANES
    assert ctx_len % SEQ_TILE == 0 and seq % SEQ_TILE == 0
    ctx_rows, x_rows = batch * ctx_len, batch * seq
    n_ctx_tiles, n_x_tiles = ctx_rows // SEQ_TILE, x_rows // SEQ_TILE
    seg = (n_ctx_tiles, ctx_len // SEQ_TILE, seq // SEQ_TILE)
    row_fn_for = lambda tile: functools.partial(_mod_row, tile=tile, ctx_rows=ctx_rows, seq=seq, batch=batch)
    row_fn = row_fn_for(SEQ_TILE)

    h = _assemble(ctx.reshape(ctx_rows, d), x.reshape(x_rows, d), jnp.asarray(_grid_pos_embed(seq // GRID_W, d)))
    cc = jnp.zeros((SUBLANES, d), F32).at[:batch].set(c).at[batch].set(c_ctx)
    mod = _ada(cc, ada_w, ada_b).reshape(depth, SUBLANES, 1, 6 * d)

    def mixing(l, h, tile0, n_tiles):
        p = _proj(h, norm_mix_g[l], mod[l], _reorder_w_in(w_in[l]), row_fn)
        local = _dn_local(p, dn_conv_w[l], dn_a_log[l], dn_dt_bias[l], seg)
        o_f, o_b = _dn_scan(local, batch=batch, ctx_len=ctx_len, seq=seq)
        params = (dn_norm_g[l], sg_ln_g[l], sg_ln_b[l], sg_w[l], sg_b[l], pool_w[l], pool_scale[l],
                  cv_w[l], cv_b[l], cv_ln_g[l], cv_ln_b[l])
        return _mix(h, p, o_f, o_b, mod[l], params, w_out[l].astype(BF16),
                    seg=seg, row_fn=row_fn, tile0=tile0, n_tiles=n_tiles)

    h = mixing(0, h, 0, n_ctx_tiles + n_x_tiles)
    h = _ffn(h, norm_ffn_g[0], mod[0], ffn_w1[0].astype(BF16), ffn_w3[0].astype(BF16), ffn_w2[0].astype(BF16),
             row_fn_for, math.gcd(512, ctx_rows, seq))

    hx = mixing(1, h, n_ctx_tiles, n_x_tiles)
    x_row_fn = lambda i: i * ROUTE_TILE // seq
    hn, route, gates, totals = _router(hx, norm_ffn_g[1], mod[1], router_w[0], router_b[0], x_row_fn, 0)
    n_slots = -(-(2 * x_rows + N_EXPERTS * EXPERT_ROWS) // EXPERT_ROWS) * EXPERT_ROWS
    slot, slot_token, super_expert, super_sub, tile_valid = _routing_tables(route, totals, n_slots)
    xs = _gather_rows(hn, slot_token, tile_valid)
    ys = _experts(xs, super_expert, super_sub, moe_w1[0], moe_w3[0], moe_w2[0])
    out = _combine(ys, slot, hx, gates, mod[1], final_norm_g, x_row_fn)
    return out.reshape(batch, seq, d)
```

```python
import functools
import math

import jax
import jax.numpy as jnp
import numpy as np
from jax import lax
from jax.experimental import pallas as pl
from jax.experimental.pallas import tpu as pltpu

F32 = jnp.float32
BF16 = jnp.bfloat16
EPS = 1e-6

GRID_W = 64
N_HEADS = 4
HEAD_DIM = 128
GROUP_W = N_HEADS * HEAD_DIM
DN_CHUNK = 64
SEQ_TILE = 256
CHUNKS_PER_TILE = SEQ_TILE // DN_CHUNK
DN_BLOCK = 128
BLOCKS_PER_TILE = SEQ_TILE // DN_BLOCK
SG_CHUNK = 128
POOL_WINDOWS = (2, 4, 8, 16)
SHORT_CONV = 5
CONF_CONV = 31
N_EXPERTS = 8
LANES = 128
SUBLANES = 8
VMEM_LIMIT = 56 * 1024 * 1024

QKV_OFF, QKV_W = 0, 3 * GROUP_W
Z_OFF = QKV_OFF + QKV_W
SG_OFF = Z_OFF + GROUP_W
CV_OFF = SG_OFF + 2 * GROUP_W
POOL_OFF = CV_OFF + 2 * GROUP_W
AB_OFF = POOL_OFF + GROUP_W
P_WIDTH = AB_OFF + LANES
AB_COLS = 4 * N_HEADS

SHM, SCM, GM, SHF, SCF, GF = range(6)


def _dot(a, b):
    return jnp.dot(a, b, preferred_element_type=F32)


def _dot_nt(a, b):
    return lax.dot_general(a, b, (((1,), (1,)), ((), ())), preferred_element_type=F32)


def _dot_tn(a, b):
    return lax.dot_general(a, b, (((0,), (0,)), ((), ())), preferred_element_type=F32)


def _sigmoid(x):
    return 1.0 / (1.0 + jnp.exp(-x))


def _silu(x):
    return x * _sigmoid(x)


def _cparams(*sem):
    return pltpu.CompilerParams(dimension_semantics=sem, vmem_limit_bytes=VMEM_LIMIT)


def _mod_row(i, *, tile, ctx_rows, seq, batch):
    start = i * tile
    return jnp.where(start < ctx_rows, batch, (start - ctx_rows) // seq)


def _mod_spec(d, chunk, row_fn):
    return pl.BlockSpec((None, 1, d), lambda i, *_: (row_fn(i), 0, chunk))


def _norm_modulate(x, g, scale, shift):
    y = x * lax.rsqrt(jnp.mean(x * x, axis=-1, keepdims=True) + EPS) * g
    return y * (1.0 + scale) + shift


def _assemble_kernel(ctx_ref, x_ref, pos_ref, o_ref, *, n_ctx_tiles):
    i = pl.program_id(0)

    @pl.when(i < n_ctx_tiles)
    def _():
        o_ref[...] = ctx_ref[...]

    @pl.when(i >= n_ctx_tiles)
    def _():
        o_ref[...] = x_ref[...] + pos_ref[...]


def _assemble(ctx2d, x2d, pos):
    ctx_rows, d = ctx2d.shape
    x_rows, seq = x2d.shape[0], pos.shape[0]
    tile = math.gcd(512, ctx_rows, seq)
    nct, seq_tiles = ctx_rows // tile, seq // tile
    x_tile = lambda i: jnp.maximum(i - nct, 0)
    return pl.pallas_call(
        functools.partial(_assemble_kernel, n_ctx_tiles=nct),
        grid=((ctx_rows + x_rows) // tile,),
        in_specs=[pl.BlockSpec((tile, d), lambda i: (jnp.minimum(i, nct - 1), 0)),
                  pl.BlockSpec((tile, d), lambda i: (x_tile(i), 0)),
                  pl.BlockSpec((tile, d), lambda i: (x_tile(i) % seq_tiles, 0))],
        out_specs=pl.BlockSpec((tile, d), lambda i: (i, 0)),
        out_shape=jax.ShapeDtypeStruct((ctx_rows + x_rows, d), F32),
        compiler_params=_cparams("parallel"),
        name="assemble_rows",
    )(ctx2d, x2d, pos)


ADA_SPLIT = 4


def _ada_kernel(c_ref, *refs):
    w_parts, (b_ref, o_ref) = refs[:ADA_SPLIT], refs[ADA_SPLIT:]
    s = _silu(c_ref[...]).astype(BF16)
    rows = s.shape[1] // ADA_SPLIT
    acc = b_ref[...]
    for k, w_ref in enumerate(w_parts):
        acc = acc + _dot(s[:, k * rows:(k + 1) * rows], w_ref[...].astype(BF16))
    o_ref[...] = acc


def _ada(cc, ada_w, ada_b):
    nl, d, n = ada_w.shape
    tn = 1024
    w_specs = [pl.BlockSpec((None, d // ADA_SPLIT, tn), lambda l, j, k=k: (l, k, j)) for k in range(ADA_SPLIT)]
    return pl.pallas_call(
        _ada_kernel,
        grid=(nl, n // tn),
        in_specs=[pl.BlockSpec((SUBLANES, d), lambda l, j: (0, 0))] + w_specs
                 + [pl.BlockSpec((None, 1, tn), lambda l, j: (l, 0, j))],
        out_specs=pl.BlockSpec((None, SUBLANES, tn), lambda l, j: (l, 0, j)),
        out_shape=jax.ShapeDtypeStruct((nl, SUBLANES, n), F32),
        compiler_params=_cparams("parallel", "parallel"),
        name="ada_modulation",
    )(cc, *([ada_w] * ADA_SPLIT), ada_b.reshape(nl, 1, n))


def _proj_kernel(h_ref, g_ref, sh_ref, sc_ref, w_ref, o_ref):
    y = _norm_modulate(h_ref[...], g_ref[...], sc_ref[...], sh_ref[...])
    o_ref[...] = _dot(y.astype(BF16), w_ref[...])


def _proj(h, g, mod, w, row_fn):
    m, d = h.shape
    n = w.shape[1]
    tm = SEQ_TILE
    return pl.pallas_call(
        _proj_kernel,
        grid=(m // tm,),
        in_specs=[pl.BlockSpec((tm, d), lambda i: (i, 0)),
                  pl.BlockSpec((1, d), lambda i: (0, 0)),
                  _mod_spec(d, SHM, row_fn),
                  _mod_spec(d, SCM, row_fn),
                  pl.BlockSpec((d, n), lambda i: (0, 0), pipeline_mode=pl.Buffered(1))],
        out_specs=pl.BlockSpec((tm, n), lambda i: (i, 0)),
        out_shape=jax.ShapeDtypeStruct((m, n), F32),
        compiler_params=_cparams("parallel"),
        name="in_proj",
    )(h, g.reshape(1, d), mod, mod, w)


def _segment_pos(i, n_ctx_tiles, ctx_seg_tiles, x_seg_tiles):
    in_ctx = i < n_ctx_tiles
    tpos = jnp.where(in_ctx, i % ctx_seg_tiles, (i - n_ctx_tiles) % x_seg_tiles)
    tlen = jnp.where(in_ctx, ctx_seg_tiles, x_seg_tiles)
    return tpos, tlen


def _fill_ext(ext_ref, prev, cur, nxt, first, last, halo):
    ext_ref[0:halo, :] = jnp.where(first, 0.0, prev)
    ext_ref[halo:halo + SEQ_TILE, :] = cur
    ext_ref[halo + SEQ_TILE:2 * halo + SEQ_TILE, :] = jnp.where(last, 0.0, nxt)


PK_W, PK_QG, PK_KD, PK_QK = 0, GROUP_W, 2 * GROUP_W, 3 * GROUP_W
PK_WIDTH = 4 * GROUP_W


def _dn_local_kernel(qkv_ref, prev_ref, next_ref, ab_ref, cw_ref, alog_ref, dtb_ref,
                     uf_ref, pkf_ref, ub_ref, pkb_ref, gam_ref,
                     ext_ref, t_ref, l_ref, rhs_ref, *, seg):
    tpos, tlen = _segment_pos(pl.program_id(0), *seg)
    halo = SUBLANES
    _fill_ext(ext_ref, prev_ref[...], qkv_ref[...], next_ref[...], tpos == 0, tpos == tlen - 1, halo)

    pad = SHORT_CONV // 2
    acc = cw_ref[0:1, :] * ext_ref[halo - pad:halo - pad + SEQ_TILE, :]
    for j in range(1, SHORT_CONV):
        acc = acc + cw_ref[j:j + 1, :] * ext_ref[halo - pad + j:halo - pad + j + SEQ_TILE, :]
    s = _silu(acc)

    ab = ab_ref[...]
    lane = lax.broadcasted_iota(jnp.int32, (SEQ_TILE, LANES), 1)
    xs = ab + dtb_ref[...]
    softplus = jnp.maximum(xs, 0.0) + jnp.log1p(jnp.exp(-jnp.abs(xs)))
    la = jnp.where(lane < 2 * N_HEADS, -jnp.exp(alog_ref[...]) * softplus, 0.0)
    beta = _sigmoid(ab)

    def chunk_masks(n):
        rr = lax.broadcasted_iota(jnp.int32, (n, n), 0)
        cc = lax.broadcasted_iota(jnp.int32, (n, n), 1)
        same = (rr // DN_CHUNK) == (cc // DN_CHUNK)
        return rr, cc, (same & (cc <= rr), same & (cc >= rr)), (same & (cc < rr), same & (cc > rr))

    _, _, incl_tile, _ = chunk_masks(SEQ_TILE)
    r, c, incl, strict = chunk_masks(DN_BLOCK)

    def off_block(s, d):
        hi, lo = (r, c) if d == 0 else (c, r)
        return ((hi // s) % 2 == 1) & ((lo // s) == (hi // s) - 1)

    la_hi = la.astype(BF16)
    rem = la - la_hi.astype(F32)
    la_mid = rem.astype(BF16)
    la_lo = (rem - la_mid.astype(F32)).astype(BF16)

    def chunk_sum(mask):
        m = jnp.where(mask, 1.0, 0.0).astype(BF16)
        return _dot(m, la_hi) + _dot(m, la_mid) + _dot(m, la_lo)

    g_fwd = chunk_sum(incl_tile[0])
    g_bwd = chunk_sum(incl_tile[1])
    g = jnp.where(lane < N_HEADS, g_fwd, g_bwd)
    g_tot = g_fwd + g_bwd - la
    g_t = g.T

    outs = ((uf_ref, pkf_ref), (ub_ref, pkb_ref))
    eye = jnp.where(r == c, 1.0, 0.0)
    for h in range(N_HEADS):
        sl = slice(h * HEAD_DIM, (h + 1) * HEAD_DIM)
        qh = s[:, h * HEAD_DIM:(h + 1) * HEAD_DIM]
        kh = s[:, GROUP_W + h * HEAD_DIM:GROUP_W + (h + 1) * HEAD_DIM]
        vh = s[:, 2 * GROUP_W + h * HEAD_DIM:2 * GROUP_W + (h + 1) * HEAD_DIM]
        qh = qh * lax.rsqrt(jnp.sum(qh * qh, axis=-1, keepdims=True) + EPS) * (HEAD_DIM ** -0.5)
        kh = kh * lax.rsqrt(jnp.sum(kh * kh, axis=-1, keepdims=True) + EPS)
        qh16 = qh.astype(BF16)
        kh16 = kh.astype(BF16)
        for d in range(2):
            gi = d * N_HEADS + h
            u_ref, pk_ref = outs[d]
            gcol = g[:, gi:gi + 1]
            bcol = beta[:, 2 * N_HEADS + gi:2 * N_HEADS + gi + 1]
            eg = jnp.exp(gcol)
            kb = kh * bcol
            rhs_ref[gi, :, :HEAD_DIM] = vh * bcol
            rhs_ref[gi, :, HEAD_DIM:] = kb * eg
            gl = g_tot[:, gi:gi + 1]
            pk_ref[:, PK_QG + h * HEAD_DIM:PK_QG + (h + 1) * HEAD_DIM] = (qh * eg).astype(BF16)
            pk_ref[:, PK_KD + h * HEAD_DIM:PK_KD + (h + 1) * HEAD_DIM] = (kh * jnp.exp(gl - gcol)).astype(BF16)
            pk_ref[:, PK_QK + h * HEAD_DIM + DN_CHUNK:PK_QK + (h + 1) * HEAD_DIM] = jnp.zeros(
                (SEQ_TILE, HEAD_DIM - DN_CHUNK), BF16)
            for cb in range(CHUNKS_PER_TILE):
                gam_ref[cb, :, gi * LANES:(gi + 1) * LANES] = jnp.broadcast_to(
                    jnp.exp(gl[cb * DN_CHUNK:cb * DN_CHUNK + 1, :]), (1, LANES))
            for blk in range(BLOCKS_PER_TILE):
                rows = slice(blk * DN_BLOCK, (blk + 1) * DN_BLOCK)
                unit = gi * BLOCKS_PER_TILE + blk
                grow = g_t[gi:gi + 1, rows]
                decay = jnp.where(incl[d], jnp.exp(jnp.where(incl[d], gcol[rows] - grow, 0.0)), 0.0)
                lmat = jnp.where(strict[d], bcol[rows] * _dot_nt(kh16[rows], kh16[rows]) * decay, 0.0)
                l_ref[unit] = lmat
                t_ref[unit] = eye - jnp.where(off_block(1, d), lmat, 0.0)
                qkm = jnp.where(incl[d], _dot_nt(qh16[rows], kh16[rows]) * decay, 0.0)
                qk = qkm[:, 0:DN_CHUNK]
                for cb in range(1, DN_BLOCK // DN_CHUNK):
                    qk = qk + qkm[:, cb * DN_CHUNK:(cb + 1) * DN_CHUNK]
                pk_ref[rows, PK_QK + h * HEAD_DIM:PK_QK + h * HEAD_DIM + DN_CHUNK] = qk.astype(BF16)

    n_chain = 2 * N_HEADS
    n_unit = n_chain * BLOCKS_PER_TILE
    unit_dir = lambda unit: unit // (N_HEADS * BLOCKS_PER_TILE)
    s_blk = 2
    while s_blk < DN_CHUNK:
        lt = []
        for unit in range(n_unit):
            l_s = jnp.where(off_block(s_blk, unit_dir(unit)), l_ref[unit], 0.0).astype(BF16)
            lt.append(_dot(l_s, t_ref[unit].astype(BF16)).astype(BF16))
        for unit in range(n_unit):
            t_inv = t_ref[unit]
            t_ref[unit] = t_inv - _dot(t_inv.astype(BF16), lt[unit])
        s_blk *= 2
    for unit in range(n_unit):
        gi, blk = divmod(unit, BLOCKS_PER_TILE)
        d, h = divmod(gi, N_HEADS)
        rows = slice(blk * DN_BLOCK, (blk + 1) * DN_BLOCK)
        u_ref, pk_ref = outs[d]
        rhs = rhs_ref[gi, rows, :]
        sol = rhs + _dot(jnp.where(strict[d], t_ref[unit], 0.0).astype(BF16), rhs.astype(BF16))
        u_ref[rows, h * HEAD_DIM:(h + 1) * HEAD_DIM] = sol[:, :HEAD_DIM]
        pk_ref[rows, PK_W + h * HEAD_DIM:PK_W + (h + 1) * HEAD_DIM] = sol[:, HEAD_DIM:].astype(BF16)


def _batch_major_tile(i, n_ctx_tiles, ctx_seg_tiles, x_seg_tiles):
    per_batch = ctx_seg_tiles + x_seg_tiles
    j = i - n_ctx_tiles
    return jnp.where(i < n_ctx_tiles,
                     (i // ctx_seg_tiles) * per_batch + i % ctx_seg_tiles,
                     (j // x_seg_tiles) * per_batch + ctx_seg_tiles + j % x_seg_tiles)


def _dn_local(p, conv_w, a_log, dt_bias, seg):
    rows = p.shape[0]
    nt = rows // SEQ_TILE
    hb = SEQ_TILE // SUBLANES
    last_hb = rows // SUBLANES - 1
    cw = jnp.zeros((SUBLANES, QKV_W), F32).at[:SHORT_CONV].set(conv_w)
    pad_row = lambda v: jnp.zeros((1, LANES), F32).at[0, :2 * N_HEADS].set(v.reshape(-1))
    u_out = jax.ShapeDtypeStruct((rows, GROUP_W), F32)
    pk_out = jax.ShapeDtypeStruct((rows, PK_WIDTH), BF16)
    gam_out = jax.ShapeDtypeStruct((rows // DN_CHUNK, 1, 2 * N_HEADS * LANES), F32)
    dst = lambda i: _batch_major_tile(i, *seg)
    u_spec = pl.BlockSpec((SEQ_TILE, GROUP_W), lambda i: (dst(i), 0))
    pk_spec = pl.BlockSpec((SEQ_TILE, PK_WIDTH), lambda i: (dst(i), 0))
    full = lambda shape: pl.BlockSpec(shape, lambda i: (0,) * len(shape))
    n_chain = 2 * N_HEADS
    return pl.pallas_call(
        functools.partial(_dn_local_kernel, seg=seg),
        grid=(nt,),
        in_specs=[pl.BlockSpec((SEQ_TILE, QKV_W), lambda i: (i, QKV_OFF // QKV_W)),
                  pl.BlockSpec((SUBLANES, QKV_W), lambda i: (jnp.maximum(i * hb - 1, 0), 0)),
                  pl.BlockSpec((SUBLANES, QKV_W), lambda i: (jnp.minimum((i + 1) * hb, last_hb), 0)),
                  pl.BlockSpec((SEQ_TILE, LANES), lambda i: (i, AB_OFF // LANES)),
                  full((SUBLANES, QKV_W)), full((1, LANES)), full((1, LANES))],
        out_specs=[u_spec, pk_spec, u_spec, pk_spec,
                   pl.BlockSpec((CHUNKS_PER_TILE, 1, 2 * N_HEADS * LANES), lambda i: (dst(i), 0, 0))],
        out_shape=[u_out, pk_out, u_out, pk_out, gam_out],
        scratch_shapes=[pltpu.VMEM((SEQ_TILE + 2 * SUBLANES, QKV_W), F32),
                        pltpu.VMEM((n_chain * BLOCKS_PER_TILE, DN_BLOCK, DN_BLOCK), F32),
                        pltpu.VMEM((n_chain * BLOCKS_PER_TILE, DN_BLOCK, DN_BLOCK), F32),
                        pltpu.VMEM((n_chain, SEQ_TILE, 2 * HEAD_DIM), F32)],
        compiler_params=_cparams("parallel"),
        name="dn_local",
    )(p, p, p, p, cw, pad_row(a_log), pad_row(dt_bias))


def _dn_scan_kernel(uf, pkf, gamf, ub, pkb, gamb, of_ref, ob_ref, s_ref, *, batch):
    @pl.when(pl.program_id(0) == 0)
    def _():
        s_ref[...] = jnp.zeros_like(s_ref)

    dirs = ((uf, pkf, gamf, of_ref), (ub, pkb, gamb, ob_ref))
    chains = [(b, d, h) for b in range(batch) for d in range(2) for h in range(N_HEADS)]
    head = lambda off, h: slice(off + h * HEAD_DIM, off + (h + 1) * HEAD_DIM)

    ws = []
    for ci, (b, d, h) in enumerate(chains):
        pk = dirs[d][1]
        lhs = jnp.concatenate([pk[b, :, head(PK_W, h)], pk[b, :, head(PK_QG, h)]], axis=0)
        ws.append(_dot(lhs, s_ref[ci].astype(BF16)))
    v16 = []
    for ci, (b, d, h) in enumerate(chains):
        v16.append((dirs[d][0][b, :, head(0, h)] - ws[ci][:DN_CHUNK]).astype(BF16))
    for ci, (b, d, h) in enumerate(chains):
        pk, o_ref = dirs[d][1], dirs[d][3]
        qk = pk[b, :, PK_QK + h * HEAD_DIM:PK_QK + h * HEAD_DIM + DN_CHUNK]
        o_ref[b, :, head(0, h)] = ws[ci][DN_CHUNK:] + _dot(qk, v16[ci])
    for ci, (b, d, h) in enumerate(chains):
        pk, gam = dirs[d][1], dirs[d][2]
        gi = d * N_HEADS + h
        s_ref[ci] = s_ref[ci] * gam[b, :, gi * LANES:(gi + 1) * LANES] + _dot_tn(pk[b, :, head(PK_KD, h)], v16[ci])


def _dn_scan(local, *, batch, ctx_len, seq):
    uf, pkf, ub, pkb, gam = local
    cc, xc = ctx_len // DN_CHUNK, seq // DN_CHUNK
    nc = cc + xc
    bwd = lambda s: jnp.where(s < cc, cc - 1 - s, cc + (xc - 1 - (s - cc)))
    by_chunk = lambda a: a.reshape(batch, nc, *a.shape[1:]) if a.ndim == 3 else a.reshape(batch, nc, DN_CHUNK, a.shape[1])

    def specs(idx):
        return [pl.BlockSpec((batch, None, DN_CHUNK, GROUP_W), lambda s: (0, idx(s), 0, 0)),
                pl.BlockSpec((batch, None, DN_CHUNK, PK_WIDTH), lambda s: (0, idx(s), 0, 0)),
                pl.BlockSpec((batch, None, 1, 2 * N_HEADS * LANES), lambda s: (0, idx(s), 0, 0))]

    out = jax.ShapeDtypeStruct((batch, nc, DN_CHUNK, GROUP_W), F32)
    fwd = lambda s: s
    o_f, o_b = pl.pallas_call(
        functools.partial(_dn_scan_kernel, batch=batch),
        grid=(nc,),
        in_specs=specs(fwd) + specs(bwd),
        out_specs=[pl.BlockSpec((batch, None, DN_CHUNK, GROUP_W), lambda s: (0, s, 0, 0)),
                   pl.BlockSpec((batch, None, DN_CHUNK, GROUP_W), lambda s: (0, bwd(s), 0, 0))],
        out_shape=[out, out],
        scratch_shapes=[pltpu.VMEM((batch * 2 * N_HEADS, HEAD_DIM, HEAD_DIM), F32)],
        compiler_params=_cparams("arbitrary"),
        name="dn_scan",
    )(by_chunk(uf), by_chunk(pkf), by_chunk(gam), by_chunk(ub), by_chunk(pkb), by_chunk(gam))
    return o_f.reshape(-1, GROUP_W), o_b.reshape(-1, GROUP_W)


def _layernorm(x, g, b):
    mu = jnp.mean(x, axis=-1, keepdims=True)
    xc = x - mu
    var = jnp.mean(xc * xc, axis=-1, keepdims=True)
    return xc * lax.rsqrt(var + EPS) * g + b


def _mix_kernel(h_ref, of_ref, ob_ref, z_ref, sg_ref, cv_ref, cvp_ref, cvn_ref, pool_ref, poolp_ref, pooln_ref,
                gm_ref, dng_ref, sglg_ref, sglb_ref, sgw_ref, sgbt_ref, pw_ref, ps_ref,
                cvw_ref, cvb_ref, cvlg_ref, cvlb_ref, wout_ref,
                o_ref, y_ref, pext_ref, cext_ref, cacc_ref, *, seg, tile0):
    tpos, tlen = _segment_pos(pl.program_id(0) + tile0, *seg)
    first, last = tpos == 0, tpos == tlen - 1

    o = of_ref[...] + ob_ref[...]
    z = z_ref[...]
    for h in range(N_HEADS):
        sl = slice(h * HEAD_DIM, (h + 1) * HEAD_DIM)
        oh = o[:, sl]
        yh = oh * lax.rsqrt(jnp.mean(oh * oh, axis=-1, keepdims=True) + EPS) * dng_ref[...] * _silu(z[:, sl])
        y_ref[:, sl] = yh.astype(BF16)

    psg = sg_ref[...]
    psg = 0.5 * psg * (1.0 + jnp.tanh(math.sqrt(2.0 / math.pi) * (psg + 0.044715 * (psg * psg * psg))))
    u = psg[:, :GROUP_W]
    v = _layernorm(psg[:, GROUP_W:], sglg_ref[...], sglb_ref[...]).astype(BF16)
    for n in range(SEQ_TILE // SG_CHUNK):
        rs = slice(n * SG_CHUNK, (n + 1) * SG_CHUNK)
        for h in range(N_HEADS):
            sl = slice(h * HEAD_DIM, (h + 1) * HEAD_DIM)
            sv = _dot(sgw_ref[h], v[rs, sl]) + sgbt_ref[:, h:h + 1]
            y_ref[rs, GROUP_W + h * HEAD_DIM:GROUP_W + (h + 1) * HEAD_DIM] = (u[rs, sl] * sv).astype(BF16)

    halo = SUBLANES
    _fill_ext(pext_ref, poolp_ref[...], pool_ref[...], pooln_ref[...], first, last, halo)
    t = tpos * SEQ_TILE + lax.broadcasted_iota(jnp.int32, (SEQ_TILE, 1), 0)
    seg_len = tlen * SEQ_TILE
    for gi, win in enumerate(POOL_WINDOWS):
        sl = slice(gi * LANES, (gi + 1) * LANES)
        tot = pext_ref[halo - win // 2:halo - win // 2 + SEQ_TILE, sl]
        for m in range(1 - win // 2, win // 2):
            tot = tot + pext_ref[halo + m:halo + m + SEQ_TILE, sl]
        cnt = jnp.clip(t + win // 2, 0, seg_len) - jnp.clip(t - win // 2, 0, seg_len)
        yg = tot / cnt.astype(F32) - pool_ref[:, sl]
        yg = _dot(yg.astype(BF16), pw_ref[gi]) * ps_ref[:, sl]
        y_ref[:, 2 * GROUP_W + gi * LANES:2 * GROUP_W + (gi + 1) * LANES] = yg.astype(BF16)

    halo = 2 * SUBLANES
    glu = lambda p: p[:, :GROUP_W] * _sigmoid(p[:, GROUP_W:])
    _fill_ext(cext_ref, glu(cvp_ref[...]), glu(cv_ref[...]), glu(cvn_ref[...]), first, last, halo)
    pad = CONF_CONV // 2
    span = SEQ_TILE + SUBLANES
    acc = None
    for rho in range(SUBLANES):
        part = None
        for j in range(CONF_CONV):
            off = halo - pad + j
            if off % SUBLANES == rho:
                term = cvw_ref[j:j + 1, :] * cext_ref[off - rho:off - rho + span, :]
                part = term if part is None else part + term
        if part is not None:
            cacc_ref[...] = part
            shifted = cacc_ref[rho:rho + SEQ_TILE, :]
            acc = shifted if acc is None else acc + shifted
    yc = _silu(_layernorm(acc + cvb_ref[...], cvlg_ref[...], cvlb_ref[...]))
    y_ref[:, 3 * GROUP_W:] = yc.astype(BF16)

    o_ref[...] = h_ref[...] + gm_ref[...] * _dot(y_ref[...], wout_ref[...])


def _mix(h, p, o_f, o_b, mod, params, wout, *, seg, row_fn, tile0, n_tiles):
    d = h.shape[1]
    (dng, sglg, sglb, sgw, sgb, pw, ps, cvw, cvb, cvlg, cvlb) = params
    rows = p.shape[0]
    row = lambda v: v.reshape(1, -1)
    cvw_p = jnp.zeros((4 * SUBLANES, GROUP_W), F32).at[:CONF_CONV].set(cvw)
    sgbt = jnp.zeros((SG_CHUNK, LANES), F32).at[:, :N_HEADS].set(sgb.T)

    def halo_specs(width, col, halo):
        hb = SEQ_TILE // halo
        last_hb = rows // halo - 1
        return [pl.BlockSpec((SEQ_TILE, width), lambda i: (i + tile0, col)),
                pl.BlockSpec((halo, width), lambda i: (jnp.maximum((i + tile0) * hb - 1, 0), col)),
                pl.BlockSpec((halo, width), lambda i: (jnp.minimum((i + tile0 + 1) * hb, last_hb), col))]

    full = lambda a: pl.BlockSpec(a.shape, lambda i: (0,) * a.ndim)
    consts = [row(dng), row(sglg), row(sglb), sgw.astype(BF16), sgbt, pw.astype(BF16), row(ps),
              cvw_p, row(cvb), row(cvlg), row(cvlb)]
    return pl.pallas_call(
        functools.partial(_mix_kernel, seg=seg, tile0=tile0),
        grid=(n_tiles,),
        in_specs=[pl.BlockSpec((SEQ_TILE, d), lambda i: (i + tile0, 0)),
                  pl.BlockSpec((SEQ_TILE, GROUP_W), lambda i: (_batch_major_tile(i + tile0, *seg), 0)),
                  pl.BlockSpec((SEQ_TILE, GROUP_W), lambda i: (_batch_major_tile(i + tile0, *seg), 0)),
                  pl.BlockSpec((SEQ_TILE, GROUP_W), lambda i: (i + tile0, Z_OFF // GROUP_W)),
                  pl.BlockSpec((SEQ_TILE, 2 * GROUP_W), lambda i: (i + tile0, SG_OFF // (2 * GROUP_W)))]
                 + halo_specs(2 * GROUP_W, CV_OFF // (2 * GROUP_W), 2 * SUBLANES)
                 + halo_specs(GROUP_W, POOL_OFF // GROUP_W, SUBLANES)
                 + [_mod_spec(d, GM, lambda i: row_fn(i + tile0))]
                 + [full(a) for a in consts]
                 + [pl.BlockSpec(wout.shape, lambda i: (0, 0), pipeline_mode=pl.Buffered(1))],
        out_specs=pl.BlockSpec((SEQ_TILE, d), lambda i: (i, 0)),
        out_shape=jax.ShapeDtypeStruct((n_tiles * SEQ_TILE, d), F32),
        scratch_shapes=[pltpu.VMEM((SEQ_TILE, d), BF16),
                        pltpu.VMEM((SEQ_TILE + 2 * SUBLANES, GROUP_W), F32),
                        pltpu.VMEM((SEQ_TILE + 4 * SUBLANES, GROUP_W), F32),
                        pltpu.VMEM((SEQ_TILE + SUBLANES, GROUP_W), F32)],
        compiler_params=_cparams("parallel"),
        name="mixers_out_proj",
    )(h, o_f, o_b, p, p, p, p, p, p, p, p, mod, *consts, wout)


def _ffn_kernel(h_ref, g_ref, sh_ref, sc_ref, gate_ref, w1_ref, w3_ref, w2_ref, o_ref, xn_ref, acc_ref):
    j = pl.program_id(1)

    @pl.when(j == 0)
    def _():
        xn_ref[...] = _norm_modulate(h_ref[...], g_ref[...], sc_ref[...], sh_ref[...]).astype(BF16)
        acc_ref[...] = jnp.zeros_like(acc_ref)

    xn = xn_ref[...]
    mid = _silu(_dot(xn, w1_ref[...])) * _dot(xn, w3_ref[...])
    acc_ref[...] += _dot(mid.astype(BF16), w2_ref[...])

    @pl.when(j == pl.num_programs(1) - 1)
    def _():
        o_ref[...] = h_ref[...] + gate_ref[...] * acc_ref[...]


def _ffn(h, g, mod, w1, w3, w2, row_fn_for, tm):
    m, d = h.shape
    f = w1.shape[1]
    tf = 512
    row_fn = row_fn_for(tm)
    return pl.pallas_call(
        _ffn_kernel,
        grid=(m // tm, f // tf),
        in_specs=[pl.BlockSpec((tm, d), lambda i, j: (i, 0)),
                  pl.BlockSpec((1, d), lambda i, j: (0, 0)),
                  _mod_spec(d, SHF, row_fn), _mod_spec(d, SCF, row_fn), _mod_spec(d, GF, row_fn),
                  pl.BlockSpec((d, tf), lambda i, j: (0, j)),
                  pl.BlockSpec((d, tf), lambda i, j: (0, j)),
                  pl.BlockSpec((tf, d), lambda i, j: (j, 0))],
        out_specs=pl.BlockSpec((tm, d), lambda i, j: (i, 0)),
        out_shape=jax.ShapeDtypeStruct((m, d), F32),
        scratch_shapes=[pltpu.VMEM((tm, d), BF16), pltpu.VMEM((tm, d), F32)],
        compiler_params=_cparams("parallel", "arbitrary"),
        name="dense_swiglu",
    )(h, g.reshape(1, d), mod, mod, mod, w1, w3, w2)


ROUTE_TILE = 256
EXPERT_ROWS = 3072


def _router_kernel(h_ref, g_ref, sh_ref, sc_ref, rw_ref, rb_ref, hn_ref, idx_ref, gate_ref, total_ref, count_ref):
    hn = _norm_modulate(h_ref[...], g_ref[...], sc_ref[...], sh_ref[...])
    hn_ref[...] = hn
    hn_hi = hn.astype(BF16)
    hn_lo = (hn - hn_hi.astype(F32)).astype(BF16)
    rw = rw_ref[...]
    rw_hi = rw.astype(BF16)
    rw_lo = (rw - rw_hi.astype(F32)).astype(BF16)
    logits = _dot(hn_hi, rw_hi) + (_dot(hn_hi, rw_lo) + _dot(hn_lo, rw_hi)) + rb_ref[...]
    lane = lax.broadcasted_iota(jnp.int32, logits.shape, 1)
    lane_f = lane.astype(F32)
    neg = -jnp.inf
    lg = jnp.where(lane < N_EXPERTS, logits, neg)
    m1 = jnp.max(lg, axis=-1, keepdims=True)
    i1 = jnp.min(jnp.where(lg == m1, lane_f, float(LANES)), axis=-1, keepdims=True)
    lg2 = jnp.where(lane_f == i1, neg, lg)
    m2 = jnp.max(lg2, axis=-1, keepdims=True)
    i2 = jnp.min(jnp.where(lg2 == m2, lane_f, float(LANES)), axis=-1, keepdims=True)
    e2 = jnp.exp(m2 - m1)
    g1 = 1.0 / (1.0 + e2)
    g2 = e2 / (1.0 + e2)
    gate_ref[...] = jnp.where(lane == 0, g1, jnp.where(lane == 1, g2, 0.0))

    @pl.when(pl.program_id(0) == 0)
    def _():
        count_ref[...] = jnp.zeros_like(count_ref)

    pick1 = jnp.where(lane_f == i1, 1.0, 0.0)
    pick2 = jnp.where(lane_f == i2, 1.0, 0.0)
    picked = pick1 + pick2
    r = lax.broadcasted_iota(jnp.int32, (ROUTE_TILE, ROUTE_TILE), 0)
    c = lax.broadcasted_iota(jnp.int32, (ROUTE_TILE, ROUTE_TILE), 1)
    earlier = jnp.where(c < r, 1.0, 0.0).astype(BF16)
    before = count_ref[0:1, :] + _dot(earlier, picked.astype(BF16))
    rank1 = jnp.sum(pick1 * before, axis=-1, keepdims=True)
    rank2 = jnp.sum(pick2 * before, axis=-1, keepdims=True)
    count_ref[...] = count_ref[...] + jnp.sum(picked, axis=0, keepdims=True)
    total_ref[...] = count_ref[...]
    idx_ref[...] = jnp.where(lane == 0, i1, jnp.where(lane == 1, i2, jnp.where(
        lane == 2, rank1, jnp.where(lane == 3, rank2, 0.0)))).astype(jnp.int32)


def _router(h, g, mod, rw, rb, row_fn, tile0):
    d = h.shape[1]
    tm = ROUTE_TILE
    n_tiles = h.shape[0] // tm - tile0
    m = n_tiles * tm
    rw_p = jnp.zeros((d, LANES), F32).at[:, :N_EXPERTS].set(rw)
    rb_p = jnp.zeros((1, LANES), F32).at[0, :N_EXPERTS].set(rb)
    shifted = lambda i: row_fn(i + tile0)
    return pl.pallas_call(
        _router_kernel,
        grid=(n_tiles,),
        in_specs=[pl.BlockSpec((tm, d), lambda i: (i + tile0, 0)),
                  pl.BlockSpec((1, d), lambda i: (0, 0)),
                  _mod_spec(d, SHF, shifted), _mod_spec(d, SCF, shifted),
                  pl.BlockSpec((d, LANES), lambda i: (0, 0)),
                  pl.BlockSpec((1, LANES), lambda i: (0, 0))],
        out_specs=[pl.BlockSpec((tm, d), lambda i: (i, 0)),
                   pl.BlockSpec((tm, LANES), lambda i: (i, 0)),
                   pl.BlockSpec((tm, LANES), lambda i: (i, 0)),
                   pl.BlockSpec((SUBLANES, LANES), lambda i: (0, 0))],
        out_shape=[jax.ShapeDtypeStruct((m, d), F32),
                   jax.ShapeDtypeStruct((m, LANES), jnp.int32),
                   jax.ShapeDtypeStruct((m, LANES), F32),
                   jax.ShapeDtypeStruct((SUBLANES, LANES), F32)],
        scratch_shapes=[pltpu.VMEM((SUBLANES, LANES), F32)],
        compiler_params=_cparams("arbitrary"),
        name="moe_router",
    )(h, g.reshape(1, d), mod, mod, rw_p, rb_p)


def _routing_tables(route, totals, n_slots):
    e_flat = route[:, 0:2].reshape(-1)
    rank = route[:, 2:4].reshape(-1)
    counts = totals[0, :N_EXPERTS].astype(jnp.int32)
    passes = (counts + EXPERT_ROWS - 1) // EXPERT_ROWS
    sub_tiles = (counts + ROUTE_TILE - 1) // ROUTE_TILE
    pass_rows = jnp.maximum((sub_tiles + jnp.maximum(passes, 1) - 1) // jnp.maximum(passes, 1), 1) * ROUTE_TILE
    padded = passes * EXPERT_ROWS
    ends = jnp.cumsum(padded)
    starts = ends - padded
    onehot = (e_flat[:, None] == jnp.arange(N_EXPERTS)[None, :]).astype(jnp.int32)
    rank_rows = jnp.sum(onehot * pass_rows[None, :], axis=1)
    rank_pass = rank // rank_rows
    slot = jnp.sum(onehot * starts[None, :], axis=1) + rank_pass * EXPERT_ROWS + (rank - rank_pass * rank_rows)
    slot_token = jnp.zeros((n_slots,), jnp.int32).at[slot].set(jnp.arange(slot.shape[0], dtype=jnp.int32) // 2)
    n_super = n_slots // EXPERT_ROWS
    super_start = jnp.arange(n_super, dtype=jnp.int32) * EXPERT_ROWS
    super_expert = jnp.minimum(jnp.sum(super_start[:, None] >= ends[None, :], axis=1), N_EXPERTS - 1).astype(jnp.int32)
    super_pass = (super_start - starts[super_expert]) // EXPERT_ROWS
    valid_rows = jnp.clip(counts[super_expert] - super_pass * pass_rows[super_expert], 0, pass_rows[super_expert])
    valid_rows = jnp.where(super_start < ends[-1], valid_rows, 0)
    super_sub = ((valid_rows + ROUTE_TILE - 1) // ROUTE_TILE).astype(jnp.int32)
    tile_valid = (jnp.arange(n_slots // ROUTE_TILE, dtype=jnp.int32) % (EXPERT_ROWS // ROUTE_TILE)
                  < jnp.repeat(super_sub, EXPERT_ROWS // ROUTE_TILE)).astype(jnp.int32)
    return slot.astype(jnp.int32), slot_token, super_expert, super_sub, tile_valid


def _row_copy(src_hbm, dst_vmem, src_row, dst_row, sem):
    return pltpu.make_async_copy(src_hbm.at[pl.ds(src_row, 1), :], dst_vmem.at[pl.ds(dst_row, 1), :], sem)


def _gather_kernel(tok_ref, valid_ref, hn_hbm, o_ref, buf_ref, sem):
    i = pl.program_id(0)
    n = pl.num_programs(0)

    def request(tile):
        slot = tile % 2

        def start(r, carry):
            _row_copy(hn_hbm, buf_ref.at[slot], tok_ref[tile * ROUTE_TILE + r], r, sem.at[slot]).start()
            return carry

        lax.fori_loop(0, ROUTE_TILE, start, 0, unroll=8)

    @pl.when((i == 0) & (valid_ref[0] != 0))
    def _():
        request(0)

    nxt = jnp.minimum(i + 1, n - 1)

    @pl.when((i + 1 < n) & (valid_ref[nxt] != 0))
    def _():
        request(nxt)

    @pl.when(valid_ref[i] == 0)
    def _():
        o_ref[...] = jnp.zeros_like(o_ref)

    @pl.when(valid_ref[i] != 0)
    def _():
        slot = i % 2

        pltpu.make_async_copy(hn_hbm.at[pl.ds(0, ROUTE_TILE), :], buf_ref.at[slot], sem.at[slot]).wait()
        o_ref[...] = buf_ref[slot].astype(BF16)


def _gather_rows(hn, slot_token, tile_valid):
    d = hn.shape[1]
    n_slots = slot_token.shape[0]
    return pl.pallas_call(
        _gather_kernel,
        grid_spec=pltpu.PrefetchScalarGridSpec(
            num_scalar_prefetch=2,
            grid=(n_slots // ROUTE_TILE,),
            in_specs=[pl.BlockSpec(memory_space=pl.ANY)],
            out_specs=pl.BlockSpec((ROUTE_TILE, d), lambda i, *_: (i, 0)),
            scratch_shapes=[pltpu.VMEM((2, ROUTE_TILE, d), F32), pltpu.SemaphoreType.DMA((2,))]),
        out_shape=jax.ShapeDtypeStruct((n_slots, d), BF16),
        compiler_params=_cparams("arbitrary"),
        name="moe_gather",
    )(slot_token, tile_valid, hn)


WEIGHT_SPLIT = 4


def _expert_kernel(exp_ref, sub_ref, x_ref, *refs):
    w1_parts = refs[:WEIGHT_SPLIT]
    w3_parts = refs[WEIGHT_SPLIT:2 * WEIGHT_SPLIT]
    w2_parts = refs[2 * WEIGHT_SPLIT:3 * WEIGHT_SPLIT]
    o_ref, w1b_ref, w3b_ref, w2b_ref = refs[3 * WEIGHT_SPLIT:]
    s, j = pl.program_id(0), pl.program_id(1)

    @pl.when(j == 0)
    def _():
        o_ref[...] = jnp.zeros_like(o_ref)

    @pl.when(sub_ref[s] > 0)
    def _():
        for parts, dst in ((w1_parts, w1b_ref), (w3_parts, w3b_ref), (w2_parts, w2b_ref)):
            rows = dst.shape[0] // WEIGHT_SPLIT
            for k, part in enumerate(parts):
                dst[k * rows:(k + 1) * rows, :] = part[...].astype(BF16)

        def swiglu_rows(start, size):
            rows = pl.ds(pl.multiple_of(start, ROUTE_TILE), size)
            xr = x_ref[rows, :]
            mid = _silu(_dot(xr, w1b_ref[...])) * _dot(xr, w3b_ref[...])
            o_ref[rows, :] += _dot(mid.astype(BF16), w2b_ref[...])

        n_sub = sub_ref[s]
        n_pair = n_sub // 2

        def pair(r, carry):
            swiglu_rows(r * (2 * ROUTE_TILE), 2 * ROUTE_TILE)
            return carry

        lax.fori_loop(0, n_pair, pair, 0)

        @pl.when(n_sub % 2 == 1)
        def _():
            swiglu_rows(n_pair * (2 * ROUTE_TILE), ROUTE_TILE)


def _experts(xs, super_expert, super_sub, w1, w3, w2):
    n_slots, d = xs.shape
    f = w1.shape[2]
    tf = 256
    nf = f // tf
    n_super = n_slots // EXPERT_ROWS
    ns = WEIGHT_SPLIT

    def f_eff(s, j, sub):
        return jnp.where(sub[s] > 0, j, nf - 1)

    up_specs = [pl.BlockSpec((None, d // ns, tf), lambda s, j, e, sub, k=k: (e[s], k, f_eff(s, j, sub)))
                for k in range(ns)]
    down_specs = [pl.BlockSpec((None, tf // ns, d), lambda s, j, e, sub, k=k: (e[s], f_eff(s, j, sub) * ns + k, 0))
                  for k in range(ns)]
    row_spec = pl.BlockSpec((EXPERT_ROWS, d), lambda s, j, e, sub: (s, 0), pipeline_mode=pl.Buffered(1))
    return pl.pallas_call(
        _expert_kernel,
        grid_spec=pltpu.PrefetchScalarGridSpec(
            num_scalar_prefetch=2,
            grid=(n_super, nf),
            in_specs=[row_spec] + up_specs + up_specs + down_specs,
            out_specs=row_spec,
            scratch_shapes=[pltpu.VMEM((d, tf), BF16), pltpu.VMEM((d, tf), BF16), pltpu.VMEM((tf, d), BF16)]),
        out_shape=jax.ShapeDtypeStruct((n_slots, d), F32),
        compiler_params=_cparams("arbitrary", "arbitrary"),
        name="moe_experts",
    )(super_expert, super_sub, xs, *([w1] * ns), *([w3] * ns), *([w2] * ns))


def _combine_kernel(slot_ref, ys_hbm, h_ref, gate_ref, gf_ref, fg_ref, o_ref, buf_ref, sem):
    i = pl.program_id(0)
    n = pl.num_programs(0)

    def request(tile):
        slot = tile % 2

        def start(r, carry):
            for k in range(2):
                src = slot_ref[2 * (tile * ROUTE_TILE + r) + k]
                _row_copy(ys_hbm, buf_ref.at[slot, k], src, r, sem.at[slot]).start()
            return carry

        lax.fori_loop(0, ROUTE_TILE, start, 0, unroll=8)

    @pl.when(i == 0)
    def _():
        request(0)

    @pl.when(i + 1 < n)
    def _():
        request(jnp.minimum(i + 1, n - 1))

    slot = i % 2

    for k in range(2):
        pltpu.make_async_copy(ys_hbm.at[pl.ds(0, ROUTE_TILE), :], buf_ref.at[slot, k], sem.at[slot]).wait()
    gates = gate_ref[...]
    y = gates[:, 0:1] * buf_ref[slot, 0] + gates[:, 1:2] * buf_ref[slot, 1]
    x = h_ref[...] + gf_ref[...] * y
    o_ref[...] = x * lax.rsqrt(jnp.mean(x * x, axis=-1, keepdims=True) + EPS) * fg_ref[...]


def _combine(ys, slot, h, gates, mod, final_g, row_fn):
    m, d = h.shape
    return pl.pallas_call(
        _combine_kernel,
        grid_spec=pltpu.PrefetchScalarGridSpec(
            num_scalar_prefetch=1,
            grid=(m // ROUTE_TILE,),
            in_specs=[pl.BlockSpec(memory_space=pl.ANY),
                      pl.BlockSpec((ROUTE_TILE, d), lambda i, *_: (i, 0)),
                      pl.BlockSpec((ROUTE_TILE, LANES), lambda i, *_: (i, 0)),
                      _mod_spec(d, GF, row_fn),
                      pl.BlockSpec((1, d), lambda i, *_: (0, 0))],
            out_specs=pl.BlockSpec((ROUTE_TILE, d), lambda i, *_: (i, 0)),
            scratch_shapes=[pltpu.VMEM((2, 2, ROUTE_TILE, d), F32), pltpu.SemaphoreType.DMA((2,))]),
        out_shape=jax.ShapeDtypeStruct((m, d), F32),
        compiler_params=_cparams("arbitrary"),
        name="moe_combine_final_norm",
    )(slot, ys, h, gates, mod, final_g.reshape(1, d))


def _grid_pos_embed(rows, dim):
    r = np.repeat(np.arange(rows, dtype=np.float32), GRID_W)
    col = np.tile(np.arange(GRID_W, dtype=np.float32), rows)
    quarter = dim // 4
    freq = np.exp(np.float32(-math.log(10000.0)) * np.arange(quarter, dtype=np.float32) / np.float32(quarter))
    ar = r[:, None] * freq
    ac = col[:, None] * freq
    return np.concatenate([np.sin(ar), np.cos(ar), np.sin(ac), np.cos(ac)], axis=-1).astype(np.float32)


def _reorder_w_in(w):
    o_ab = QKV_W
    o_z = o_ab + AB_COLS
    o_sg = o_z + GROUP_W
    o_pool = o_sg + 2 * GROUP_W
    o_cv = o_pool + GROUP_W
    w = w.astype(BF16)
    pad = jnp.zeros(w.shape[:-1] + (LANES - AB_COLS,), w.dtype)
    return jnp.concatenate([w[..., :QKV_W], w[..., o_z:o_sg], w[..., o_sg:o_pool], w[..., o_cv:o_cv + 2 * GROUP_W],
                            w[..., o_pool:o_cv], w[..., o_ab:o_z], pad], axis=-1)


def kernel(x, c, ctx, c_ctx, ada_w, ada_b, norm_mix_g, w_in, dn_conv_w, dn_a_log, dn_dt_bias, dn_norm_g,
           sg_ln_g, sg_ln_b, sg_w, sg_b, pool_w, pool_scale, cv_w, cv_b, cv_ln_g, cv_ln_b, w_out,
           norm_ffn_g, ffn_w1, ffn_w3, ffn_w2, router_w, router_b, moe_w1, moe_w3, moe_w2, final_norm_g):
    batch, seq, d = x.shape
    ctx_len = ctx.shape[1]
    depth = ada_w.shape[0]
    assert depth == 2 and d == 4 * GROUP_W and batch + 1 <= SUBLANES
    assert ctx_len % SEQ_TILE == 0 and seq % SEQ_TILE == 0
    ctx_rows, x_rows = batch * ctx_len, batch * seq
    n_ctx_tiles, n_x_tiles = ctx_rows // SEQ_TILE, x_rows // SEQ_TILE
    seg = (n_ctx_tiles, ctx_len // SEQ_TILE, seq // SEQ_TILE)
    row_fn_for = lambda tile: functools.partial(_mod_row, tile=tile, ctx_rows=ctx_rows, seq=seq, batch=batch)
    row_fn = row_fn_for(SEQ_TILE)

    h = _assemble(ctx.reshape(ctx_rows, d), x.reshape(x_rows, d), jnp.asarray(_grid_pos_embed(seq // GRID_W, d)))
    cc = jnp.zeros((SUBLANES, d), F32).at[:batch].set(c).at[batch].set(c_ctx)
    mod = _ada(cc, ada_w, ada_b).reshape(depth, SUBLANES, 1, 6 * d)

    w_in16 = _reorder_w_in(w_in)

    def mixing(l, h, tile0, n_tiles):
        p = _proj(h, norm_mix_g[l], mod[l], w_in16[l], row_fn)
        local = _dn_local(p, dn_conv_w[l], dn_a_log[l], dn_dt_bias[l], seg)
        o_f, o_b = _dn_scan(local, batch=batch, ctx_len=ctx_len, seq=seq)
        params = (dn_norm_g[l], sg_ln_g[l], sg_ln_b[l], sg_w[l], sg_b[l], pool_w[l], pool_scale[l],
                  cv_w[l], cv_b[l], cv_ln_g[l], cv_ln_b[l])
        return _mix(h, p, o_f, o_b, mod[l], params, w_out[l].astype(BF16),
                    seg=seg, row_fn=row_fn, tile0=tile0, n_tiles=n_tiles)

    h = mixing(0, h, 0, n_ctx_tiles + n_x_tiles)
    h = _ffn(h, norm_ffn_g[0], mod[0], ffn_w1[0].astype(BF16), ffn_w3[0].astype(BF16), ffn_w2[0].astype(BF16),
             row_fn_for, math.gcd(512, ctx_rows, seq))

    hx = mixing(1, h, n_ctx_tiles, n_x_tiles)
    x_row_fn = lambda i: i * ROUTE_TILE // seq
    hn, route, gates, totals = _router(hx, norm_ffn_g[1], mod[1], router_w[0], router_b[0], x_row_fn, 0)
    n_slots = -(-(2 * x_rows + N_EXPERTS * EXPERT_ROWS) // EXPERT_ROWS) * EXPERT_ROWS
    slot, slot_token, super_expert, super_sub, tile_valid = _routing_tables(route, totals, n_slots)
    xs = _gather_rows(hn, slot_token, tile_valid)
    ys = _experts(xs, super_expert, super_sub, moe_w1[0], moe_w3[0], moe_w2[0])
    out = _combine(ys, slot, hx, gates, mod[1], final_norm_g, x_row_fn)
    return out.reshape(batch, seq, d)
```

```python
import functools
import math

import jax
import jax.numpy as jnp
import numpy as np
from jax import lax
from jax.experimental import pallas as pl
from jax.experimental.pallas import tpu as pltpu

F32 = jnp.float32
BF16 = jnp.bfloat16
EPS = 1e-6

GRID_W = 64
N_HEADS = 4
HEAD_DIM = 128
GROUP_W = N_HEADS * HEAD_DIM
DN_CHUNK = 64
SEQ_TILE = 256
CHUNKS_PER_TILE = SEQ_TILE // DN_CHUNK
DN_BLOCK = 128
BLOCKS_PER_TILE = SEQ_TILE // DN_BLOCK
SG_CHUNK = 128
POOL_WINDOWS = (2, 4, 8, 16)
SHORT_CONV = 5
CONF_CONV = 31
N_EXPERTS = 8
LANES = 128
SUBLANES = 8
VMEM_LIMIT = 56 * 1024 * 1024

QKV_OFF, QKV_W = 0, 3 * GROUP_W
Z_OFF = QKV_OFF + QKV_W
SG_OFF = Z_OFF + GROUP_W
CV_OFF = SG_OFF + 2 * GROUP_W
POOL_OFF = CV_OFF + 2 * GROUP_W
AB_OFF = POOL_OFF + GROUP_W
P_WIDTH = AB_OFF + LANES
AB_COLS = 4 * N_HEADS

SHM, SCM, GM, SHF, SCF, GF = range(6)


def _dot(a, b):
    return jnp.dot(a, b, preferred_element_type=F32)


def _dot_nt(a, b):
    return lax.dot_general(a, b, (((1,), (1,)), ((), ())), preferred_element_type=F32)


def _dot_tn(a, b):
    return lax.dot_general(a, b, (((0,), (0,)), ((), ())), preferred_element_type=F32)


def _sigmoid(x):
    return 1.0 / (1.0 + jnp.exp(-x))


def _silu(x):
    return x * _sigmoid(x)


def _cparams(*sem):
    return pltpu.CompilerParams(dimension_semantics=sem, vmem_limit_bytes=VMEM_LIMIT)


def _mod_row(i, *, tile, ctx_rows, seq, batch):
    start = i * tile
    return jnp.where(start < ctx_rows, batch, (start - ctx_rows) // seq)


def _mod_spec(d, chunk, row_fn):
    return pl.BlockSpec((None, 1, d), lambda i, *_: (row_fn(i), 0, chunk))


def _norm_modulate(x, g, scale, shift):
    y = x * lax.rsqrt(jnp.mean(x * x, axis=-1, keepdims=True) + EPS) * g
    return y * (1.0 + scale) + shift


def _assemble_kernel(ctx_ref, x_ref, pos_ref, o_ref, *, n_ctx_tiles):
    i = pl.program_id(0)

    @pl.when(i < n_ctx_tiles)
    def _():
        o_ref[...] = ctx_ref[...]

    @pl.when(i >= n_ctx_tiles)
    def _():
        o_ref[...] = x_ref[...] + pos_ref[...]


def _assemble(ctx2d, x2d, pos):
    ctx_rows, d = ctx2d.shape
    x_rows, seq = x2d.shape[0], pos.shape[0]
    tile = math.gcd(512, ctx_rows, seq)
    nct, seq_tiles = ctx_rows // tile, seq // tile
    x_tile = lambda i: jnp.maximum(i - nct, 0)
    return pl.pallas_call(
        functools.partial(_assemble_kernel, n_ctx_tiles=nct),
        grid=((ctx_rows + x_rows) // tile,),
        in_specs=[pl.BlockSpec((tile, d), lambda i: (jnp.minimum(i, nct - 1), 0)),
                  pl.BlockSpec((tile, d), lambda i: (x_tile(i), 0)),
                  pl.BlockSpec((tile, d), lambda i: (x_tile(i) % seq_tiles, 0))],
        out_specs=pl.BlockSpec((tile, d), lambda i: (i, 0)),
        out_shape=jax.ShapeDtypeStruct((ctx_rows + x_rows, d), F32),
        compiler_params=_cparams("parallel"),
        name="assemble_rows",
    )(ctx2d, x2d, pos)


ADA_SPLIT = 4


def _ada_kernel(c_ref, *refs):
    w_parts, (b_ref, o_ref) = refs[:ADA_SPLIT], refs[ADA_SPLIT:]
    s = _silu(c_ref[...]).astype(BF16)
    rows = s.shape[1] // ADA_SPLIT
    acc = b_ref[...]
    for k, w_ref in enumerate(w_parts):
        acc = acc + _dot(s[:, k * rows:(k + 1) * rows], w_ref[...].astype(BF16))
    o_ref[...] = acc


def _ada(cc, ada_w, ada_b):
    nl, d, n = ada_w.shape
    tn = 1024
    w_specs = [pl.BlockSpec((None, d // ADA_SPLIT, tn), lambda l, j, k=k: (l, k, j)) for k in range(ADA_SPLIT)]
    return pl.pallas_call(
        _ada_kernel,
        grid=(nl, n // tn),
        in_specs=[pl.BlockSpec((SUBLANES, d), lambda l, j: (0, 0))] + w_specs
                 + [pl.BlockSpec((None, 1, tn), lambda l, j: (l, 0, j))],
        out_specs=pl.BlockSpec((None, SUBLANES, tn), lambda l, j: (l, 0, j)),
        out_shape=jax.ShapeDtypeStruct((nl, SUBLANES, n), F32),
        compiler_params=_cparams("parallel", "parallel"),
        name="ada_modulation",
    )(cc, *([ada_w] * ADA_SPLIT), ada_b.reshape(nl, 1, n))


def _proj_kernel(h_ref, g_ref, sh_ref, sc_ref, w_ref, o_ref):
    y = _norm_modulate(h_ref[...], g_ref[...], sc_ref[...], sh_ref[...])
    o_ref[...] = _dot(y.astype(BF16), w_ref[...])


def _proj(h, g, mod, w, row_fn):
    m, d = h.shape
    n = w.shape[1]
    tm = SEQ_TILE
    return pl.pallas_call(
        _proj_kernel,
        grid=(m // tm,),
        in_specs=[pl.BlockSpec((tm, d), lambda i: (i, 0)),
                  pl.BlockSpec((1, d), lambda i: (0, 0)),
                  _mod_spec(d, SHM, row_fn),
                  _mod_spec(d, SCM, row_fn),
                  pl.BlockSpec((d, n), lambda i: (0, 0), pipeline_mode=pl.Buffered(1))],
        out_specs=pl.BlockSpec((tm, n), lambda i: (i, 0)),
        out_shape=jax.ShapeDtypeStruct((m, n), F32),
        compiler_params=_cparams("parallel"),
        name="in_proj",
    )(h, g.reshape(1, d), mod, mod, w)


def _segment_pos(i, n_ctx_tiles, ctx_seg_tiles, x_seg_tiles):
    in_ctx = i < n_ctx_tiles
    tpos = jnp.where(in_ctx, i % ctx_seg_tiles, (i - n_ctx_tiles) % x_seg_tiles)
    tlen = jnp.where(in_ctx, ctx_seg_tiles, x_seg_tiles)
    return tpos, tlen


def _fill_ext(ext_ref, prev, cur, nxt, first, last, halo):
    ext_ref[0:halo, :] = jnp.where(first, 0.0, prev)
    ext_ref[halo:halo + SEQ_TILE, :] = cur
    ext_ref[halo + SEQ_TILE:2 * halo + SEQ_TILE, :] = jnp.where(last, 0.0, nxt)


PK_W, PK_QG, PK_KD, PK_QK = 0, GROUP_W, 2 * GROUP_W, 3 * GROUP_W
PK_WIDTH = 4 * GROUP_W


def _dn_local_kernel(qkv_ref, prev_ref, next_ref, ab_ref, cw_ref, alog_ref, dtb_ref,
                     uf_ref, pkf_ref, ub_ref, pkb_ref, gam_ref,
                     ext_ref, t_ref, l_ref, rhs_ref, *, seg):
    tpos, tlen = _segment_pos(pl.program_id(0), *seg)
    halo = SUBLANES
    _fill_ext(ext_ref, prev_ref[...], qkv_ref[...], next_ref[...], tpos == 0, tpos == tlen - 1, halo)

    pad = SHORT_CONV // 2
    acc = cw_ref[0:1, :] * ext_ref[halo - pad:halo - pad + SEQ_TILE, :]
    for j in range(1, SHORT_CONV):
        acc = acc + cw_ref[j:j + 1, :] * ext_ref[halo - pad + j:halo - pad + j + SEQ_TILE, :]
    s = _silu(acc)

    ab = ab_ref[...]
    lane = lax.broadcasted_iota(jnp.int32, (SEQ_TILE, LANES), 1)
    xs = ab + dtb_ref[...]
    softplus = jnp.maximum(xs, 0.0) + jnp.log1p(jnp.exp(-jnp.abs(xs)))
    la = jnp.where(lane < 2 * N_HEADS, -jnp.exp(alog_ref[...]) * softplus, 0.0)
    beta = _sigmoid(ab)

    def chunk_masks(n):
        rr = lax.broadcasted_iota(jnp.int32, (n, n), 0)
        cc = lax.broadcasted_iota(jnp.int32, (n, n), 1)
        same = (rr // DN_CHUNK) == (cc // DN_CHUNK)
        return rr, cc, (same & (cc <= rr), same & (cc >= rr)), (same & (cc < rr), same & (cc > rr))

    _, _, incl_tile, _ = chunk_masks(SEQ_TILE)
    r, c, incl, strict = chunk_masks(DN_BLOCK)

    def off_block(s, d):
        hi, lo = (r, c) if d == 0 else (c, r)
        return ((hi // s) % 2 == 1) & ((lo // s) == (hi // s) - 1)

    la_hi = la.astype(BF16)
    rem = la - la_hi.astype(F32)
    la_mid = rem.astype(BF16)
    la_lo = (rem - la_mid.astype(F32)).astype(BF16)

    def chunk_sum(mask):
        m = jnp.where(mask, 1.0, 0.0).astype(BF16)
        return _dot(m, la_hi) + _dot(m, la_mid) + _dot(m, la_lo)

    g_fwd = chunk_sum(incl_tile[0])
    g_bwd = chunk_sum(incl_tile[1])
    g = jnp.where(lane < N_HEADS, g_fwd, g_bwd)
    g_tot = g_fwd + g_bwd - la
    g_t = g.T

    outs = ((uf_ref, pkf_ref), (ub_ref, pkb_ref))
    eye = jnp.where(r == c, 1.0, 0.0)
    for h in range(N_HEADS):
        sl = slice(h * HEAD_DIM, (h + 1) * HEAD_DIM)
        qh = s[:, h * HEAD_DIM:(h + 1) * HEAD_DIM]
        kh = s[:, GROUP_W + h * HEAD_DIM:GROUP_W + (h + 1) * HEAD_DIM]
        vh = s[:, 2 * GROUP_W + h * HEAD_DIM:2 * GROUP_W + (h + 1) * HEAD_DIM]
        qh = qh * lax.rsqrt(jnp.sum(qh * qh, axis=-1, keepdims=True) + EPS) * (HEAD_DIM ** -0.5)
        kh = kh * lax.rsqrt(jnp.sum(kh * kh, axis=-1, keepdims=True) + EPS)
        qh16 = qh.astype(BF16)
        kh16 = kh.astype(BF16)
        for d in range(2):
            gi = d * N_HEADS + h
            u_ref, pk_ref = outs[d]
            gcol = g[:, gi:gi + 1]
            bcol = beta[:, 2 * N_HEADS + gi:2 * N_HEADS + gi + 1]
            eg = jnp.exp(gcol)
            kb = kh * bcol
            rhs_ref[gi, :, :HEAD_DIM] = vh * bcol
            rhs_ref[gi, :, HEAD_DIM:] = kb * eg
            gl = g_tot[:, gi:gi + 1]
            pk_ref[:, PK_QG + h * HEAD_DIM:PK_QG + (h + 1) * HEAD_DIM] = (qh * eg).astype(BF16)
            pk_ref[:, PK_KD + h * HEAD_DIM:PK_KD + (h + 1) * HEAD_DIM] = (kh * jnp.exp(gl - gcol)).astype(BF16)
            pk_ref[:, PK_QK + h * HEAD_DIM + DN_CHUNK:PK_QK + (h + 1) * HEAD_DIM] = jnp.zeros(
                (SEQ_TILE, HEAD_DIM - DN_CHUNK), BF16)
            for cb in range(CHUNKS_PER_TILE):
                gam_ref[cb, :, gi * LANES:(gi + 1) * LANES] = jnp.broadcast_to(
                    jnp.exp(gl[cb * DN_CHUNK:cb * DN_CHUNK + 1, :]), (1, LANES))
            for blk in range(BLOCKS_PER_TILE):
                rows = slice(blk * DN_BLOCK, (blk + 1) * DN_BLOCK)
                unit = gi * BLOCKS_PER_TILE + blk
                grow = g_t[gi:gi + 1, rows]
                decay = jnp.where(incl[d], jnp.exp(jnp.where(incl[d], gcol[rows] - grow, 0.0)), 0.0)
                lmat = jnp.where(strict[d], bcol[rows] * _dot_nt(kh16[rows], kh16[rows]) * decay, 0.0)
                l_ref[unit] = lmat
                t_ref[unit] = eye - jnp.where(off_block(1, d), lmat, 0.0)
                qkm = jnp.where(incl[d], _dot_nt(qh16[rows], kh16[rows]) * decay, 0.0)
                qk = qkm[:, 0:DN_CHUNK]
                for cb in range(1, DN_BLOCK // DN_CHUNK):
                    qk = qk + qkm[:, cb * DN_CHUNK:(cb + 1) * DN_CHUNK]
                pk_ref[rows, PK_QK + h * HEAD_DIM:PK_QK + h * HEAD_DIM + DN_CHUNK] = qk.astype(BF16)

    n_chain = 2 * N_HEADS
    n_unit = n_chain * BLOCKS_PER_TILE
    unit_dir = lambda unit: unit // (N_HEADS * BLOCKS_PER_TILE)
    s_blk = 2
    while s_blk < DN_CHUNK:
        lt = []
        for unit in range(n_unit):
            l_s = jnp.where(off_block(s_blk, unit_dir(unit)), l_ref[unit], 0.0).astype(BF16)
            lt.append(_dot(l_s, t_ref[unit].astype(BF16)).astype(BF16))
        for unit in range(n_unit):
            t_inv = t_ref[unit]
            t_ref[unit] = t_inv - _dot(t_inv.astype(BF16), lt[unit])
        s_blk *= 2
    for unit in range(n_unit):
        gi, blk = divmod(unit, BLOCKS_PER_TILE)
        d, h = divmod(gi, N_HEADS)
        rows = slice(blk * DN_BLOCK, (blk + 1) * DN_BLOCK)
        u_ref, pk_ref = outs[d]
        rhs = rhs_ref[gi, rows, :]
        sol = rhs + _dot(jnp.where(strict[d], t_ref[unit], 0.0).astype(BF16), rhs.astype(BF16))
        u_ref[rows, h * HEAD_DIM:(h + 1) * HEAD_DIM] = sol[:, :HEAD_DIM]
        pk_ref[rows, PK_W + h * HEAD_DIM:PK_W + (h + 1) * HEAD_DIM] = sol[:, HEAD_DIM:].astype(BF16)


def _batch_major_tile(i, n_ctx_tiles, ctx_seg_tiles, x_seg_tiles):
    per_batch = ctx_seg_tiles + x_seg_tiles
    j = i - n_ctx_tiles
    return jnp.where(i < n_ctx_tiles,
                     (i // ctx_seg_tiles) * per_batch + i % ctx_seg_tiles,
                     (j // x_seg_tiles) * per_batch + ctx_seg_tiles + j % x_seg_tiles)


def _dn_local(p, conv_w, a_log, dt_bias, seg):
    rows = p.shape[0]
    nt = rows // SEQ_TILE
    hb = SEQ_TILE // SUBLANES
    last_hb = rows // SUBLANES - 1
    cw = jnp.zeros((SUBLANES, QKV_W), F32).at[:SHORT_CONV].set(conv_w)
    pad_row = lambda v: jnp.zeros((1, LANES), F32).at[0, :2 * N_HEADS].set(v.reshape(-1))
    u_out = jax.ShapeDtypeStruct((rows, GROUP_W), F32)
    pk_out = jax.ShapeDtypeStruct((rows, PK_WIDTH), BF16)
    gam_out = jax.ShapeDtypeStruct((rows // DN_CHUNK, 1, 2 * N_HEADS * LANES), F32)
    dst = lambda i: _batch_major_tile(i, *seg)
    u_spec = pl.BlockSpec((SEQ_TILE, GROUP_W), lambda i: (dst(i), 0))
    pk_spec = pl.BlockSpec((SEQ_TILE, PK_WIDTH), lambda i: (dst(i), 0))
    full = lambda shape: pl.BlockSpec(shape, lambda i: (0,) * len(shape))
    n_chain = 2 * N_HEADS
    return pl.pallas_call(
        functools.partial(_dn_local_kernel, seg=seg),
        grid=(nt,),
        in_specs=[pl.BlockSpec((SEQ_TILE, QKV_W), lambda i: (i, QKV_OFF // QKV_W)),
                  pl.BlockSpec((SUBLANES, QKV_W), lambda i: (jnp.maximum(i * hb - 1, 0), 0)),
                  pl.BlockSpec((SUBLANES, QKV_W), lambda i: (jnp.minimum((i + 1) * hb, last_hb), 0)),
                  pl.BlockSpec((SEQ_TILE, LANES), lambda i: (i, AB_OFF // LANES)),
                  full((SUBLANES, QKV_W)), full((1, LANES)), full((1, LANES))],
        out_specs=[u_spec, pk_spec, u_spec, pk_spec,
                   pl.BlockSpec((CHUNKS_PER_TILE, 1, 2 * N_HEADS * LANES), lambda i: (dst(i), 0, 0))],
        out_shape=[u_out, pk_out, u_out, pk_out, gam_out],
        scratch_shapes=[pltpu.VMEM((SEQ_TILE + 2 * SUBLANES, QKV_W), F32),
                        pltpu.VMEM((n_chain * BLOCKS_PER_TILE, DN_BLOCK, DN_BLOCK), F32),
                        pltpu.VMEM((n_chain * BLOCKS_PER_TILE, DN_BLOCK, DN_BLOCK), F32),
                        pltpu.VMEM((n_chain, SEQ_TILE, 2 * HEAD_DIM), F32)],
        compiler_params=_cparams("parallel"),
        name="dn_local",
    )(p, p, p, p, cw, pad_row(a_log), pad_row(dt_bias))


def _dn_scan_kernel(uf, pkf, gamf, ub, pkb, gamb, of_ref, ob_ref, s_ref, *, batch):
    @pl.when(pl.program_id(0) == 0)
    def _():
        s_ref[...] = jnp.zeros_like(s_ref)

    dirs = ((uf, pkf, gamf, of_ref), (ub, pkb, gamb, ob_ref))
    chains = [(b, d, h) for b in range(batch) for d in range(2) for h in range(N_HEADS)]
    head = lambda off, h: slice(off + h * HEAD_DIM, off + (h + 1) * HEAD_DIM)

    ws = []
    for ci, (b, d, h) in enumerate(chains):
        pk = dirs[d][1]
        lhs = jnp.concatenate([pk[b, :, head(PK_W, h)], pk[b, :, head(PK_QG, h)]], axis=0)
        ws.append(_dot(lhs, s_ref[ci].astype(BF16)))
    v16 = []
    for ci, (b, d, h) in enumerate(chains):
        v16.append((dirs[d][0][b, :, head(0, h)] - ws[ci][:DN_CHUNK]).astype(BF16))
    for ci, (b, d, h) in enumerate(chains):
        pk, o_ref = dirs[d][1], dirs[d][3]
        qk = pk[b, :, PK_QK + h * HEAD_DIM:PK_QK + h * HEAD_DIM + DN_CHUNK]
        o_ref[b, :, head(0, h)] = ws[ci][DN_CHUNK:] + _dot(qk, v16[ci])
    for ci, (b, d, h) in enumerate(chains):
        pk, gam = dirs[d][1], dirs[d][2]
        gi = d * N_HEADS + h
        s_ref[ci] = s_ref[ci] * gam[b, :, gi * LANES:(gi + 1) * LANES] + _dot_tn(pk[b, :, head(PK_KD, h)], v16[ci])


def _dn_scan(local, *, batch, ctx_len, seq):
    uf, pkf, ub, pkb, gam = local
    cc, xc = ctx_len // DN_CHUNK, seq // DN_CHUNK
    nc = cc + xc
    bwd = lambda s: jnp.where(s < cc, cc - 1 - s, cc + (xc - 1 - (s - cc)))
    by_chunk = lambda a: a.reshape(batch, nc, *a.shape[1:]) if a.ndim == 3 else a.reshape(batch, nc, DN_CHUNK, a.shape[1])

    def specs(idx):
        return [pl.BlockSpec((batch, None, DN_CHUNK, GROUP_W), lambda s: (0, idx(s), 0, 0)),
                pl.BlockSpec((batch, None, DN_CHUNK, PK_WIDTH), lambda s: (0, idx(s), 0, 0)),
                pl.BlockSpec((batch, None, 1, 2 * N_HEADS * LANES), lambda s: (0, idx(s), 0, 0))]

    out = jax.ShapeDtypeStruct((batch, nc, DN_CHUNK, GROUP_W), F32)
    fwd = lambda s: s
    o_f, o_b = pl.pallas_call(
        functools.partial(_dn_scan_kernel, batch=batch),
        grid=(nc,),
        in_specs=specs(fwd) + specs(bwd),
        out_specs=[pl.BlockSpec((batch, None, DN_CHUNK, GROUP_W), lambda s: (0, s, 0, 0)),
                   pl.BlockSpec((batch, None, DN_CHUNK, GROUP_W), lambda s: (0, bwd(s), 0, 0))],
        out_shape=[out, out],
        scratch_shapes=[pltpu.VMEM((batch * 2 * N_HEADS, HEAD_DIM, HEAD_DIM), F32)],
        compiler_params=_cparams("arbitrary"),
        name="dn_scan",
    )(by_chunk(uf), by_chunk(pkf), by_chunk(gam), by_chunk(ub), by_chunk(pkb), by_chunk(gam))
    return o_f.reshape(-1, GROUP_W), o_b.reshape(-1, GROUP_W)


def _layernorm(x, g, b):
    mu = jnp.mean(x, axis=-1, keepdims=True)
    xc = x - mu
    var = jnp.mean(xc * xc, axis=-1, keepdims=True)
    return xc * lax.rsqrt(var + EPS) * g + b


def _mix_kernel(h_ref, of_ref, ob_ref, z_ref, sg_ref, cv_ref, cvp_ref, cvn_ref, pool_ref, poolp_ref, pooln_ref,
                gm_ref, dng_ref, sglg_ref, sglb_ref, sgw_ref, sgbt_ref, pw_ref, ps_ref,
                cvw_ref, cvb_ref, cvlg_ref, cvlb_ref, wout_ref,
                o_ref, y_ref, pext_ref, cext_ref, cacc_ref, *, seg, tile0):
    tpos, tlen = _segment_pos(pl.program_id(0) + tile0, *seg)
    first, last = tpos == 0, tpos == tlen - 1

    o = of_ref[...] + ob_ref[...]
    z = z_ref[...]
    for h in range(N_HEADS):
        sl = slice(h * HEAD_DIM, (h + 1) * HEAD_DIM)
        oh = o[:, sl]
        yh = oh * lax.rsqrt(jnp.mean(oh * oh, axis=-1, keepdims=True) + EPS) * dng_ref[...] * _silu(z[:, sl])
        y_ref[:, sl] = yh.astype(BF16)

    psg = sg_ref[...]
    psg = 0.5 * psg * (1.0 + jnp.tanh(math.sqrt(2.0 / math.pi) * (psg + 0.044715 * (psg * psg * psg))))
    u = psg[:, :GROUP_W]
    v = _layernorm(psg[:, GROUP_W:], sglg_ref[...], sglb_ref[...]).astype(BF16)
    for n in range(SEQ_TILE // SG_CHUNK):
        rs = slice(n * SG_CHUNK, (n + 1) * SG_CHUNK)
        for h in range(N_HEADS):
            sl = slice(h * HEAD_DIM, (h + 1) * HEAD_DIM)
            sv = _dot(sgw_ref[h], v[rs, sl]) + sgbt_ref[:, h:h + 1]
            y_ref[rs, GROUP_W + h * HEAD_DIM:GROUP_W + (h + 1) * HEAD_DIM] = (u[rs, sl] * sv).astype(BF16)

    halo = SUBLANES
    _fill_ext(pext_ref, poolp_ref[...], pool_ref[...], pooln_ref[...], first, last, halo)
    t = tpos * SEQ_TILE + lax.broadcasted_iota(jnp.int32, (SEQ_TILE, 1), 0)
    seg_len = tlen * SEQ_TILE
    for gi, win in enumerate(POOL_WINDOWS):
        sl = slice(gi * LANES, (gi + 1) * LANES)
        tot = pext_ref[halo - win // 2:halo - win // 2 + SEQ_TILE, sl]
        for m in range(1 - win // 2, win // 2):
            tot = tot + pext_ref[halo + m:halo + m + SEQ_TILE, sl]
        cnt = jnp.clip(t + win // 2, 0, seg_len) - jnp.clip(t - win // 2, 0, seg_len)
        yg = tot / cnt.astype(F32) - pool_ref[:, sl]
        yg = _dot(yg.astype(BF16), pw_ref[gi]) * ps_ref[:, sl]
        y_ref[:, 2 * GROUP_W + gi * LANES:2 * GROUP_W + (gi + 1) * LANES] = yg.astype(BF16)

    halo = 2 * SUBLANES
    glu = lambda p: p[:, :GROUP_W] * _sigmoid(p[:, GROUP_W:])
    _fill_ext(cext_ref, glu(cvp_ref[...]), glu(cv_ref[...]), glu(cvn_ref[...]), first, last, halo)
    pad = CONF_CONV // 2
    span = SEQ_TILE + SUBLANES
    acc = None
    for rho in range(SUBLANES):
        part = None
        for j in range(CONF_CONV):
            off = halo - pad + j
            if off % SUBLANES == rho:
                term = cvw_ref[j:j + 1, :] * cext_ref[off - rho:off - rho + span, :]
                part = term if part is None else part + term
        if part is not None:
            cacc_ref[...] = part
            shifted = cacc_ref[rho:rho + SEQ_TILE, :]
            acc = shifted if acc is None else acc + shifted
    yc = _silu(_layernorm(acc + cvb_ref[...], cvlg_ref[...], cvlb_ref[...]))
    y_ref[:, 3 * GROUP_W:] = yc.astype(BF16)

    o_ref[...] = h_ref[...] + gm_ref[...] * _dot(y_ref[...], wout_ref[...])


def _mix(h, p, o_f, o_b, mod, params, wout, *, seg, row_fn, tile0, n_tiles):
    d = h.shape[1]
    (dng, sglg, sglb, sgw, sgb, pw, ps, cvw, cvb, cvlg, cvlb) = params
    rows = p.shape[0]
    row = lambda v: v.reshape(1, -1)
    cvw_p = jnp.zeros((4 * SUBLANES, GROUP_W), F32).at[:CONF_CONV].set(cvw)
    sgbt = jnp.zeros((SG_CHUNK, LANES), F32).at[:, :N_HEADS].set(sgb.T)

    def halo_specs(width, col, halo):
        hb = SEQ_TILE // halo
        last_hb = rows // halo - 1
        return [pl.BlockSpec((SEQ_TILE, width), lambda i: (i + tile0, col)),
                pl.BlockSpec((halo, width), lambda i: (jnp.maximum((i + tile0) * hb - 1, 0), col)),
                pl.BlockSpec((halo, width), lambda i: (jnp.minimum((i + tile0 + 1) * hb, last_hb), col))]

    full = lambda a: pl.BlockSpec(a.shape, lambda i: (0,) * a.ndim)
    consts = [row(dng), row(sglg), row(sglb), sgw.astype(BF16), sgbt, pw.astype(BF16), row(ps),
              cvw_p, row(cvb), row(cvlg), row(cvlb)]
    return pl.pallas_call(
        functools.partial(_mix_kernel, seg=seg, tile0=tile0),
        grid=(n_tiles,),
        in_specs=[pl.BlockSpec((SEQ_TILE, d), lambda i: (i + tile0, 0)),
                  pl.BlockSpec((SEQ_TILE, GROUP_W), lambda i: (_batch_major_tile(i + tile0, *seg), 0)),
                  pl.BlockSpec((SEQ_TILE, GROUP_W), lambda i: (_batch_major_tile(i + tile0, *seg), 0)),
                  pl.BlockSpec((SEQ_TILE, GROUP_W), lambda i: (i + tile0, Z_OFF // GROUP_W)),
                  pl.BlockSpec((SEQ_TILE, 2 * GROUP_W), lambda i: (i + tile0, SG_OFF // (2 * GROUP_W)))]
                 + halo_specs(2 * GROUP_W, CV_OFF // (2 * GROUP_W), 2 * SUBLANES)
                 + halo_specs(GROUP_W, POOL_OFF // GROUP_W, SUBLANES)
                 + [_mod_spec(d, GM, lambda i: row_fn(i + tile0))]
                 + [full(a) for a in consts]
                 + [pl.BlockSpec(wout.shape, lambda i: (0, 0), pipeline_mode=pl.Buffered(1))],
        out_specs=pl.BlockSpec((SEQ_TILE, d), lambda i: (i, 0)),
        out_shape=jax.ShapeDtypeStruct((n_tiles * SEQ_TILE, d), F32),
        scratch_shapes=[pltpu.VMEM((SEQ_TILE, d), BF16),
                        pltpu.VMEM((SEQ_TILE + 2 * SUBLANES, GROUP_W), F32),
                        pltpu.VMEM((SEQ_TILE + 4 * SUBLANES, GROUP_W), F32),
                        pltpu.VMEM((SEQ_TILE + SUBLANES, GROUP_W), F32)],
        compiler_params=_cparams("parallel"),
        name="mixers_out_proj",
    )(h, o_f, o_b, p, p, p, p, p, p, p, p, mod, *consts, wout)


def _ffn_kernel(h_ref, g_ref, sh_ref, sc_ref, gate_ref, w1_ref, w3_ref, w2_ref, o_ref, xn_ref, acc_ref):
    j = pl.program_id(1)

    @pl.when(j == 0)
    def _():
        xn_ref[...] = _norm_modulate(h_ref[...], g_ref[...], sc_ref[...], sh_ref[...]).astype(BF16)
        acc_ref[...] = jnp.zeros_like(acc_ref)

    xn = xn_ref[...]
    mid = _silu(_dot(xn, w1_ref[...])) * _dot(xn, w3_ref[...])
    acc_ref[...] += _dot(mid.astype(BF16), w2_ref[...])

    @pl.when(j == pl.num_programs(1) - 1)
    def _():
        o_ref[...] = h_ref[...] + gate_ref[...] * acc_ref[...]


def _ffn(h, g, mod, w1, w3, w2, row_fn_for, tm):
    m, d = h.shape
    f = w1.shape[1]
    tf = 512
    row_fn = row_fn_for(tm)
    return pl.pallas_call(
        _ffn_kernel,
        grid=(m // tm, f // tf),
        in_specs=[pl.BlockSpec((tm, d), lambda i, j: (i, 0)),
                  pl.BlockSpec((1, d), lambda i, j: (0, 0)),
                  _mod_spec(d, SHF, row_fn), _mod_spec(d, SCF, row_fn), _mod_spec(d, GF, row_fn),
                  pl.BlockSpec((d, tf), lambda i, j: (0, j)),
                  pl.BlockSpec((d, tf), lambda i, j: (0, j)),
                  pl.BlockSpec((tf, d), lambda i, j: (j, 0))],
        out_specs=pl.BlockSpec((tm, d), lambda i, j: (i, 0)),
        out_shape=jax.ShapeDtypeStruct((m, d), F32),
        scratch_shapes=[pltpu.VMEM((tm, d), BF16), pltpu.VMEM((tm, d), F32)],
        compiler_params=_cparams("parallel", "arbitrary"),
        name="dense_swiglu",
    )(h, g.reshape(1, d), mod, mod, mod, w1, w3, w2)


ROUTE_TILE = 256
EXPERT_ROWS = 3072


def _router_kernel(h_ref, g_ref, sh_ref, sc_ref, rw_ref, rb_ref, hn_ref, idx_ref, gate_ref, total_ref, count_ref):
    hn = _norm_modulate(h_ref[...], g_ref[...], sc_ref[...], sh_ref[...])
    hn_ref[...] = hn
    hn_hi = hn.astype(BF16)
    hn_lo = (hn - hn_hi.astype(F32)).astype(BF16)
    rw = rw_ref[...]
    rw_hi = rw.astype(BF16)
    rw_lo = (rw - rw_hi.astype(F32)).astype(BF16)
    logits = _dot(hn_hi, rw_hi) + (_dot(hn_hi, rw_lo) + _dot(hn_lo, rw_hi)) + rb_ref[...]
    lane = lax.broadcasted_iota(jnp.int32, logits.shape, 1)
    lane_f = lane.astype(F32)
    neg = -jnp.inf
    lg = jnp.where(lane < N_EXPERTS, logits, neg)
    m1 = jnp.max(lg, axis=-1, keepdims=True)
    i1 = jnp.min(jnp.where(lg == m1, lane_f, float(LANES)), axis=-1, keepdims=True)
    lg2 = jnp.where(lane_f == i1, neg, lg)
    m2 = jnp.max(lg2, axis=-1, keepdims=True)
    i2 = jnp.min(jnp.where(lg2 == m2, lane_f, float(LANES)), axis=-1, keepdims=True)
    e2 = jnp.exp(m2 - m1)
    g1 = 1.0 / (1.0 + e2)
    g2 = e2 / (1.0 + e2)
    gate_ref[...] = jnp.where(lane == 0, g1, jnp.where(lane == 1, g2, 0.0))

    @pl.when(pl.program_id(0) == 0)
    def _():
        count_ref[...] = jnp.zeros_like(count_ref)

    pick1 = jnp.where(lane_f == i1, 1.0, 0.0)
    pick2 = jnp.where(lane_f == i2, 1.0, 0.0)
    picked = pick1 + pick2
    r = lax.broadcasted_iota(jnp.int32, (ROUTE_TILE, ROUTE_TILE), 0)
    c = lax.broadcasted_iota(jnp.int32, (ROUTE_TILE, ROUTE_TILE), 1)
    earlier = jnp.where(c < r, 1.0, 0.0).astype(BF16)
    before = count_ref[0:1, :] + _dot(earlier, picked.astype(BF16))
    rank1 = jnp.sum(pick1 * before, axis=-1, keepdims=True)
    rank2 = jnp.sum(pick2 * before, axis=-1, keepdims=True)
    count_ref[...] = count_ref[...] + jnp.sum(picked, axis=0, keepdims=True)
    total_ref[...] = count_ref[...]
    idx_ref[...] = jnp.where(lane == 0, i1, jnp.where(lane == 1, i2, jnp.where(
        lane == 2, rank1, jnp.where(lane == 3, rank2, 0.0)))).astype(jnp.int32)


def _router(h, g, mod, rw, rb, row_fn, tile0):
    d = h.shape[1]
    tm = ROUTE_TILE
    n_tiles = h.shape[0] // tm - tile0
    m = n_tiles * tm
    rw_p = jnp.zeros((d, LANES), F32).at[:, :N_EXPERTS].set(rw)
    rb_p = jnp.zeros((1, LANES), F32).at[0, :N_EXPERTS].set(rb)
    shifted = lambda i: row_fn(i + tile0)
    return pl.pallas_call(
        _router_kernel,
        grid=(n_tiles,),
        in_specs=[pl.BlockSpec((tm, d), lambda i: (i + tile0, 0)),
                  pl.BlockSpec((1, d), lambda i: (0, 0)),
                  _mod_spec(d, SHF, shifted), _mod_spec(d, SCF, shifted),
                  pl.BlockSpec((d, LANES), lambda i: (0, 0)),
                  pl.BlockSpec((1, LANES), lambda i: (0, 0))],
        out_specs=[pl.BlockSpec((tm, d), lambda i: (i, 0)),
                   pl.BlockSpec((tm, LANES), lambda i: (i, 0)),
                   pl.BlockSpec((tm, LANES), lambda i: (i, 0)),
                   pl.BlockSpec((SUBLANES, LANES), lambda i: (0, 0))],
        out_shape=[jax.ShapeDtypeStruct((m, d), F32),
                   jax.ShapeDtypeStruct((m, LANES), jnp.int32),
                   jax.ShapeDtypeStruct((m, LANES), F32),
                   jax.ShapeDtypeStruct((SUBLANES, LANES), F32)],
        scratch_shapes=[pltpu.VMEM((SUBLANES, LANES), F32)],
        compiler_params=_cparams("arbitrary"),
        name="moe_router",
    )(h, g.reshape(1, d), mod, mod, rw_p, rb_p)


def _routing_tables(route, totals, n_slots):
    e_flat = route[:, 0:2].reshape(-1)
    rank = route[:, 2:4].reshape(-1)
    counts = totals[0, :N_EXPERTS].astype(jnp.int32)
    passes = (counts + EXPERT_ROWS - 1) // EXPERT_ROWS
    sub_tiles = (counts + ROUTE_TILE - 1) // ROUTE_TILE
    pass_rows = jnp.maximum((sub_tiles + jnp.maximum(passes, 1) - 1) // jnp.maximum(passes, 1), 1) * ROUTE_TILE
    padded = passes * EXPERT_ROWS
    ends = jnp.cumsum(padded)
    starts = ends - padded
    onehot = (e_flat[:, None] == jnp.arange(N_EXPERTS)[None, :]).astype(jnp.int32)
    rank_rows = jnp.sum(onehot * pass_rows[None, :], axis=1)
    rank_pass = rank // rank_rows
    slot = jnp.sum(onehot * starts[None, :], axis=1) + rank_pass * EXPERT_ROWS + (rank - rank_pass * rank_rows)
    slot_token = jnp.zeros((n_slots,), jnp.int32).at[slot].set(jnp.arange(slot.shape[0], dtype=jnp.int32) // 2)
    n_super = n_slots // EXPERT_ROWS
    super_start = jnp.arange(n_super, dtype=jnp.int32) * EXPERT_ROWS
    super_expert = jnp.minimum(jnp.sum(super_start[:, None] >= ends[None, :], axis=1), N_EXPERTS - 1).astype(jnp.int32)
    super_pass = (super_start - starts[super_expert]) // EXPERT_ROWS
    valid_rows = jnp.clip(counts[super_expert] - super_pass * pass_rows[super_expert], 0, pass_rows[super_expert])
    valid_rows = jnp.where(super_start < ends[-1], valid_rows, 0)
    super_sub = ((valid_rows + ROUTE_TILE - 1) // ROUTE_TILE).astype(jnp.int32)
    tile_valid = (jnp.arange(n_slots // ROUTE_TILE, dtype=jnp.int32) % (EXPERT_ROWS // ROUTE_TILE)
                  < jnp.repeat(super_sub, EXPERT_ROWS // ROUTE_TILE)).astype(jnp.int32)
    return slot.astype(jnp.int32), slot_token, super_expert, super_sub, tile_valid


def _row_copy(src_hbm, dst_vmem, src_row, dst_row, sem):
    return pltpu.make_async_copy(src_hbm.at[pl.ds(src_row, 1), :], dst_vmem.at[pl.ds(dst_row, 1), :], sem)


def _gather_kernel(tok_ref, valid_ref, hn_hbm, o_ref, buf_ref, sem):
    i = pl.program_id(0)
    n = pl.num_programs(0)

    def request(tile):
        slot = tile % 2

        def start(r, carry):
            _row_copy(hn_hbm, buf_ref.at[slot], tok_ref[tile * ROUTE_TILE + r], r, sem.at[slot]).start()
            return carry

        lax.fori_loop(0, ROUTE_TILE, start, 0, unroll=8)

    @pl.when((i == 0) & (valid_ref[0] != 0))
    def _():
        request(0)

    nxt = jnp.minimum(i + 1, n - 1)

    @pl.when((i + 1 < n) & (valid_ref[nxt] != 0))
    def _():
        request(nxt)

    @pl.when(valid_ref[i] == 0)
    def _():
        o_ref[...] = jnp.zeros_like(o_ref)

    @pl.when(valid_ref[i] != 0)
    def _():
        slot = i % 2

        pltpu.make_async_copy(hn_hbm.at[pl.ds(0, ROUTE_TILE), :], buf_ref.at[slot], sem.at[slot]).wait()
        o_ref[...] = buf_ref[slot].astype(BF16)


def _gather_rows(hn, slot_token, tile_valid):
    d = hn.shape[1]
    n_slots = slot_token.shape[0]
    return pl.pallas_call(
        _gather_kernel,
        grid_spec=pltpu.PrefetchScalarGridSpec(
            num_scalar_prefetch=2,
            grid=(n_slots // ROUTE_TILE,),
            in_specs=[pl.BlockSpec(memory_space=pl.ANY)],
            out_specs=pl.BlockSpec((ROUTE_TILE, d), lambda i, *_: (i, 0)),
            scratch_shapes=[pltpu.VMEM((2, ROUTE_TILE, d), F32), pltpu.SemaphoreType.DMA((2,))]),
        out_shape=jax.ShapeDtypeStruct((n_slots, d), BF16),
        compiler_params=_cparams("arbitrary"),
        name="moe_gather",
    )(slot_token, tile_valid, hn)


WEIGHT_SPLIT = 4


def _expert_kernel(exp_ref, sub_ref, x_ref, *refs):
    w1_parts = refs[:WEIGHT_SPLIT]
    w3_parts = refs[WEIGHT_SPLIT:2 * WEIGHT_SPLIT]
    w2_parts = refs[2 * WEIGHT_SPLIT:3 * WEIGHT_SPLIT]
    o_ref, w1b_ref, w3b_ref, w2b_ref = refs[3 * WEIGHT_SPLIT:]
    s, j = pl.program_id(0), pl.program_id(1)

    @pl.when(j == 0)
    def _():
        o_ref[...] = jnp.zeros_like(o_ref)

    @pl.when(sub_ref[s] > 0)
    def _():
        for parts, dst in ((w1_parts, w1b_ref), (w3_parts, w3b_ref), (w2_parts, w2b_ref)):
            rows = dst.shape[0] // WEIGHT_SPLIT
            for k, part in enumerate(parts):
                dst[k * rows:(k + 1) * rows, :] = part[...].astype(BF16)

        def swiglu_rows(start, size):
            rows = pl.ds(pl.multiple_of(start, ROUTE_TILE), size)
            xr = x_ref[rows, :]
            mid = _silu(_dot(xr, w1b_ref[...])) * _dot(xr, w3b_ref[...])
            o_ref[rows, :] += _dot(mid.astype(BF16), w2b_ref[...])

        n_sub = sub_ref[s]
        n_pair = n_sub // 2

        def pair(r, carry):
            swiglu_rows(r * (2 * ROUTE_TILE), 2 * ROUTE_TILE)
            return carry

        lax.fori_loop(0, n_pair, pair, 0)

        @pl.when(n_sub % 2 == 1)
        def _():
            swiglu_rows(n_pair * (2 * ROUTE_TILE), ROUTE_TILE)


def _experts(xs, super_expert, super_sub, w1, w3, w2):
    n_slots, d = xs.shape
    f = w1.shape[2]
    tf = 256
    nf = f // tf
    n_super = n_slots // EXPERT_ROWS
    ns = WEIGHT_SPLIT

    def f_eff(s, j, sub):
        return jnp.where(sub[s] > 0, j, nf - 1)

    up_specs = [pl.BlockSpec((None, d // ns, tf), lambda s, j, e, sub, k=k: (e[s], k, f_eff(s, j, sub)))
                for k in range(ns)]
    down_specs = [pl.BlockSpec((None, tf // ns, d), lambda s, j, e, sub, k=k: (e[s], f_eff(s, j, sub) * ns + k, 0))
                  for k in range(ns)]
    row_spec = pl.BlockSpec((EXPERT_ROWS, d), lambda s, j, e, sub: (s, 0), pipeline_mode=pl.Buffered(1))
    return pl.pallas_call(
        _expert_kernel,
        grid_spec=pltpu.PrefetchScalarGridSpec(
            num_scalar_prefetch=2,
            grid=(n_super, nf),
            in_specs=[row_spec] + up_specs + up_specs + down_specs,
            out_specs=row_spec,
            scratch_shapes=[pltpu.VMEM((d, tf), BF16), pltpu.VMEM((d, tf), BF16), pltpu.VMEM((tf, d), BF16)]),
        out_shape=jax.ShapeDtypeStruct((n_slots, d), F32),
        compiler_params=_cparams("arbitrary", "arbitrary"),
        name="moe_experts",
    )(super_expert, super_sub, xs, *([w1] * ns), *([w3] * ns), *([w2] * ns))


def _combine_kernel(slot_ref, ys_hbm, h_ref, gate_ref, gf_ref, fg_ref, o_ref, buf_ref, sem):
    i = pl.program_id(0)
    n = pl.num_programs(0)

    def request(tile):
        slot = tile % 2

        def start(r, carry):
            for k in range(2):
                src = slot_ref[2 * (tile * ROUTE_TILE + r) + k]
                _row_copy(ys_hbm, buf_ref.at[slot, k], src, r, sem.at[slot]).start()
            return carry

        lax.fori_loop(0, ROUTE_TILE, start, 0, unroll=8)

    @pl.when(i == 0)
    def _():
        request(0)

    @pl.when(i + 1 < n)
    def _():
        request(jnp.minimum(i + 1, n - 1))

    slot = i % 2

    for k in range(2):
        pltpu.make_async_copy(ys_hbm.at[pl.ds(0, ROUTE_TILE), :], buf_ref.at[slot, k], sem.at[slot]).wait()
    gates = gate_ref[...]
    y = gates[:, 0:1] * buf_ref[slot, 0] + gates[:, 1:2] * buf_ref[slot, 1]
    x = h_ref[...] + gf_ref[...] * y
    o_ref[...] = x * lax.rsqrt(jnp.mean(x * x, axis=-1, keepdims=True) + EPS) * fg_ref[...]


def _combine(ys, slot, h, gates, mod, final_g, row_fn):
    m, d = h.shape
    return pl.pallas_call(
        _combine_kernel,
        grid_spec=pltpu.PrefetchScalarGridSpec(
            num_scalar_prefetch=1,
            grid=(m // ROUTE_TILE,),
            in_specs=[pl.BlockSpec(memory_space=pl.ANY),
                      pl.BlockSpec((ROUTE_TILE, d), lambda i, *_: (i, 0)),
                      pl.BlockSpec((ROUTE_TILE, LANES), lambda i, *_: (i, 0)),
                      _mod_spec(d, GF, row_fn),
                      pl.BlockSpec((1, d), lambda i, *_: (0, 0))],
            out_specs=pl.BlockSpec((ROUTE_TILE, d), lambda i, *_: (i, 0)),
            scratch_shapes=[pltpu.VMEM((2, 2, ROUTE_TILE, d), F32), pltpu.SemaphoreType.DMA((2,))]),
        out_shape=jax.ShapeDtypeStruct((m, d), F32),
        compiler_params=_cparams("arbitrary"),
        name="moe_combine_final_norm",
    )(slot, ys, h, gates, mod, final_g.reshape(1, d))


def _grid_pos_embed(rows, dim):
    r = np.repeat(np.arange(rows, dtype=np.float32), GRID_W)
    col = np.tile(np.arange(GRID_W, dtype=np.float32), rows)
    quarter = dim // 4
    freq = np.exp(np.float32(-math.log(10000.0)) * np.arange(quarter, dtype=np.float32) / np.float32(quarter))
    ar = r[:, None] * freq
    ac = col[:, None] * freq
    return np.concatenate([np.sin(ar), np.cos(ar), np.sin(ac), np.cos(ac)], axis=-1).astype(np.float32)


def _reorder_w_in(w):
    o_ab = QKV_W
    o_z = o_ab + AB_COLS
    o_sg = o_z + GROUP_W
    o_pool = o_sg + 2 * GROUP_W
    o_cv = o_pool + GROUP_W
    pad = jnp.zeros((w.shape[0], LANES - AB_COLS), w.dtype)
    return jnp.concatenate([w[:, :QKV_W], w[:, o_z:o_sg], w[:, o_sg:o_pool], w[:, o_cv:o_cv + 2 * GROUP_W],
                            w[:, o_pool:o_cv], w[:, o_ab:o_z], pad], axis=1).astype(BF16)


def kernel(x, c, ctx, c_ctx, ada_w, ada_b, norm_mix_g, w_in, dn_conv_w, dn_a_log, dn_dt_bias, dn_norm_g,
           sg_ln_g, sg_ln_b, sg_w, sg_b, pool_w, pool_scale, cv_w, cv_b, cv_ln_g, cv_ln_b, w_out,
           norm_ffn_g, ffn_w1, ffn_w3, ffn_w2, router_w, router_b, moe_w1, moe_w3, moe_w2, final_norm_g):
    batch, seq, d = x.shape
    ctx_len = ctx.shape[1]
    depth = ada_w.shape[0]
    assert depth == 2 and d == 4 * GROUP_W and batch + 1 <= SUBLANES
    assert ctx_len % SEQ_TILE == 0 and seq % SEQ_TILE == 0
    ctx_rows, x_rows = batch * ctx_len, batch * seq
    n_ctx_tiles, n_x_tiles = ctx_rows // SEQ_TILE, x_rows // SEQ_TILE
    seg = (n_ctx_tiles, ctx_len // SEQ_TILE, seq // SEQ_TILE)
    row_fn_for = lambda tile: functools.partial(_mod_row, tile=tile, ctx_rows=ctx_rows, seq=seq, batch=batch)
    row_fn = row_fn_for(SEQ_TILE)

    h = _assemble(ctx.reshape(ctx_rows, d), x.reshape(x_rows, d), jnp.asarray(_grid_pos_embed(seq // GRID_W, d)))
    cc = jnp.zeros((SUBLANES, d), F32).at[:batch].set(c).at[batch].set(c_ctx)
    mod = _ada(cc, ada_w, ada_b).reshape(depth, SUBLANES, 1, 6 * d)

    def mixing(l, h, tile0, n_tiles):
        p = _proj(h, norm_mix_g[l], mod[l], _reorder_w_in(w_in[l]), row_fn)
        local = _dn_local(p, dn_conv_w[l], dn_a_log[l], dn_dt_bias[l], seg)
        o_f, o_b = _dn_scan(local, batch=batch, ctx_len=ctx_len, seq=seq)
        params = (dn_norm_g[l], sg_ln_g[l], sg_ln_b[l], sg_w[l], sg_b[l], pool_w[l], pool_scale[l],
                  cv_w[l], cv_b[l], cv_ln_g[l], cv_ln_b[l])
        return _mix(h, p, o_f, o_b, mod[l], params, w_out[l].astype(BF16),
                    seg=seg, row_fn=row_fn, tile0=tile0, n_tiles=n_tiles)

    h = mixing(0, h, 0, n_ctx_tiles + n_x_tiles)
    h = _ffn(h, norm_ffn_g[0], mod[0], ffn_w1[0].astype(BF16), ffn_w3[0].astype(BF16), ffn_w2[0].astype(BF16),
             row_fn_for, math.gcd(512, ctx_rows, seq))

    hx = mixing(1, h, n_ctx_tiles, n_x_tiles)
    x_row_fn = lambda i: i * ROUTE_TILE // seq
    hn, route, gates, totals = _router(hx, norm_ffn_g[1], mod[1], router_w[0], router_b[0], x_row_fn, 0)
    n_slots = -(-(2 * x_rows + N_EXPERTS * EXPERT_ROWS) // EXPERT_ROWS) * EXPERT_ROWS
    slot, slot_token, super_expert, super_sub, tile_valid = _routing_tables(route, totals, n_slots)
    xs = _gather_rows(hn, slot_token, tile_valid)
    ys = _experts(xs, super_expert, super_sub, moe_w1[0], moe_w3[0], moe_w2[0])
    out = _combine(ys, slot, hx, gates, mod[1], final_norm_g, x_row_fn)
    return out.reshape(batch, seq, d)
```
